```python
import jax, jax.numpy as jnp
from jax import lax
import numpy as np

D_MODEL = 1024
BATCH = 8
SEQ = 4096
DEPTH = 4

HEAD_DIM = 64
Q_BLOCK = 128
ROPE_THETA = 10000.0
H_A = D_MODEL // (2 * HEAD_DIM)
MOBA_BLOCK = 256
MOBA_TOPK = 3
MOBA_QCHUNK = 16
H_B = D_MODEL // (2 * HEAD_DIM)
MLA_NOPE = 64
MLA_ROPE = 32
MLA_V = 64
MLA_Q_LORA = D_MODEL // 4
MLA_KV_LORA = D_MODEL // 8
H_C = D_MODEL // (2 * HEAD_DIM)
H_C_KV = H_C // 4
SWA_WINDOW = 128
H_D = D_MODEL // (4 * HEAD_DIM)
N_EXPERTS = 64
N_GROUPS = 8
TOPK_GROUPS = 4
TOP_K = 8
D_EXPERT = D_MODEL // 4
D_SHARED = D_EXPERT
ROUTED_SCALE = 2.5
EXPERT_BLOCK = 128
N_EVEN = (DEPTH + 1) // 2
N_ODD = DEPTH // 2
DN_ALPHA = (2 * DEPTH) ** 0.25
DN_BETA = (8 * DEPTH) ** -0.25
EV_SPLITS = (H_A * HEAD_DIM, H_A * HEAD_DIM, H_A * HEAD_DIM, MLA_Q_LORA, MLA_KV_LORA, MLA_ROPE)
OD_SPLITS = (H_C * HEAD_DIM, H_C_KV * HEAD_DIM, H_C_KV * HEAD_DIM,
             2 * H_D * HEAD_DIM, 2 * H_D * HEAD_DIM, 2 * H_D * HEAD_DIM)
EV_IN = sum(EV_SPLITS)
OD_IN = sum(OD_SPLITS)
EV_MIX = H_A * HEAD_DIM + H_B * MLA_V
OD_MIX = H_C * HEAD_DIM + H_D * 2 * HEAD_DIM

kernel_name = 'hybrid_moba_mla_swa_diff_moe_deepnorm'

F32 = jnp.float32


def _split(h, widths):
    return jnp.split(h, [int(i) for i in np.cumsum(widths)[:-1]], axis=-1)


def layer_norm(x, g, b, eps=1e-5):
    xf = x.astype(F32)
    mu = jnp.mean(xf, -1, keepdims=True)
    var = jnp.mean(jnp.square(xf - mu), -1, keepdims=True)
    return ((xf - mu) * lax.rsqrt(var + eps) * g + b).astype(x.dtype)


def rms_norm(x, g, eps=1e-6):
    xf = x.astype(F32)
    return (xf * lax.rsqrt(jnp.mean(jnp.square(xf), -1, keepdims=True) + eps) * g).astype(x.dtype)


def rope(x, positions):
    d = x.shape[-1]
    inv = ROPE_THETA ** (-jnp.arange(0, d, 2, dtype=F32) / d)
    ang = positions.astype(F32)[..., None] * inv
    cos, sin = jnp.cos(ang)[:, :, None, :], jnp.sin(ang)[:, :, None, :]
    x1, x2 = jnp.split(x.astype(F32), 2, axis=-1)
    return jnp.concatenate([x1 * cos - x2 * sin, x2 * cos + x1 * sin], -1).astype(x.dtype)


def _heads(t, h):
    b, s = t.shape[:2]
    return t.reshape(b, s, h, -1).transpose(0, 2, 1, 3)


def _merge(o):
    b, h, s, d = o.shape
    return o.transpose(0, 2, 1, 3).reshape(b, s, h * d)


def _map_query_blocks(fn, qs, block):
    b, h, s = qs[0].shape[:3]
    nq = s // block
    xs = tuple(jnp.moveaxis(q.reshape(b, h, nq, block, q.shape[-1]), 2, 0) for q in qs)
    out = lax.map(lambda a: fn(a[0] * block, *a[1:]), (jnp.arange(nq, dtype=jnp.int32),) + xs)
    return jnp.moveaxis(out, 0, 2).reshape(b, h, s, out.shape[-1])


def causal_attention(q, k, v, scale):
    k_pos = jnp.arange(k.shape[2])

    def blk(start, qb):
        q_pos = start + jnp.arange(Q_BLOCK)
        logits = jnp.einsum('bhqd,bhkd->bhqk', qb, k, preferred_element_type=F32) * scale
        logits = jnp.where(k_pos[None, :] <= q_pos[:, None], logits, -jnp.inf)
        p = jax.nn.softmax(logits, axis=-1).astype(v.dtype)
        return jnp.einsum('bhqk,bhkd->bhqd', p, v)

    return _map_query_blocks(blk, (q,), Q_BLOCK)


def moba_attention(q, k, v):
    b, h, s, d = q.shape
    nb = -(-s // MOBA_BLOCK)
    pad = nb * MOBA_BLOCK - s
    kb = jnp.pad(k, ((0, 0), (0, 0), (0, pad), (0, 0))).reshape(b, h, nb, MOBA_BLOCK, d)
    vb = jnp.pad(v, ((0, 0), (0, 0), (0, pad), (0, 0))).reshape(b, h, nb, MOBA_BLOCK, d)
    k_mean = jnp.mean(kb.astype(F32), axis=3)
    gate = jnp.einsum('bhsd,bhnd->bhsn', q.astype(F32), k_mean)
    q_blk = jnp.arange(s) // MOBA_BLOCK
    past = jnp.arange(nb)[None, :] < q_blk[:, None]
    gate = jnp.where(past, gate, -jnp.inf)
    n_sel = min(MOBA_TOPK, nb)
    _, sel = lax.top_k(gate, n_sel)
    bi = jnp.arange(b)[:, None, None, None]
    hi = jnp.arange(h)[None, :, None, None]
    scale = d ** -0.5

    def blk(start, qc, sc):
        qblk = start // MOBA_BLOCK
        q_pos = start + jnp.arange(MOBA_QCHUNK)
        k_sel = kb[bi, hi, sc]
        v_sel = vb[bi, hi, sc]
        l_sel = jnp.einsum('bhqd,bhqnkd->bhqnk', qc, k_sel, preferred_element_type=F32) * scale
        l_sel = jnp.where((jnp.arange(n_sel) < qblk)[:, None], l_sel, -jnp.inf)
        k_own = lax.dynamic_index_in_dim(kb, qblk, axis=2, keepdims=False)
        v_own = lax.dynamic_index_in_dim(vb, qblk, axis=2, keepdims=False)
        l_own = jnp.einsum('bhqd,bhkd->bhqk', qc, k_own, preferred_element_type=F32) * scale
        own_pos = qblk * MOBA_BLOCK + jnp.arange(MOBA_BLOCK)
        l_own = jnp.where(own_pos[None, :] <= q_pos[:, None], l_own, -jnp.inf)
        logits = jnp.concatenate([l_sel.reshape(b, h, MOBA_QCHUNK, n_sel * MOBA_BLOCK), l_own], -1)
        p = jax.nn.softmax(logits, axis=-1).astype(v.dtype)
        p_sel = p[..., :n_sel * MOBA_BLOCK].reshape(b, h, MOBA_QCHUNK, n_sel, MOBA_BLOCK)
        p_own = p[..., n_sel * MOBA_BLOCK:]
        return (jnp.einsum('bhqnk,bhqnkd->bhqd', p_sel, v_sel)
                + jnp.einsum('bhqk,bhkd->bhqd', p_own, v_own))

    return _map_query_blocks(blk, (q, sel), MOBA_QCHUNK)


def swa_sinks_attention(q, k, v, sinks):
    b, hq, s, d = q.shape
    hkv = k.shape[1]
    g = hq // hkv
    w = SWA_WINDOW
    nb = s // w
    qb = q.reshape(b, hkv, g, nb, w, d)

    def band(t):
        tb = t.reshape(b, hkv, nb, w, d)
        prev = jnp.pad(tb, ((0, 0), (0, 0), (1, 0), (0, 0), (0, 0)))[:, :, :-1]
        return jnp.concatenate([prev, tb], axis=3)

    kk, vv = band(k), band(v)
    logits = jnp.einsum('bhgnqd,bhnkd->bhgnqk', qb, kk, preferred_element_type=F32) * d ** -0.5
    k_off = jnp.arange(2 * w) - w
    rel = jnp.arange(w)[:, None] - k_off[None, :]
    key_abs = jnp.arange(nb)[:, None, None] * w + k_off[None, None, :]
    mask = (rel >= 0) & (rel < SWA_WINDOW) & (key_abs >= 0)
    logits = jnp.where(mask, logits, -jnp.inf)
    sink = jnp.broadcast_to(sinks.astype(F32).reshape(hkv, g)[None, :, :, None, None, None],
                            logits.shape[:-1] + (1,))
    p = jax.nn.softmax(jnp.concatenate([logits, sink], -1), axis=-1)[..., :-1].astype(v.dtype)
    out = jnp.einsum('bhgnqk,bhnkd->bhgnqd', p, vv)
    return out.reshape(b, hq, s, d)


def diff_attention(q1, q2, k1, k2, v, lam, scale):
    k_pos = jnp.arange(k1.shape[2])

    def blk(start, q1b, q2b):
        q_pos = start + jnp.arange(Q_BLOCK)
        causal = k_pos[None, :] <= q_pos[:, None]
        l1 = jnp.einsum('bhqd,bhkd->bhqk', q1b, k1, preferred_element_type=F32) * scale
        l2 = jnp.einsum('bhqd,bhkd->bhqk', q2b, k2, preferred_element_type=F32) * scale
        p = (jax.nn.softmax(jnp.where(causal, l1, -jnp.inf), axis=-1)
             - lam * jax.nn.softmax(jnp.where(causal, l2, -jnp.inf), axis=-1))
        return jnp.einsum('bhqk,bhkd->bhqd', p.astype(v.dtype), v)

    return _map_query_blocks(blk, (q1, q2), Q_BLOCK)


def even_mixer(x, positions, w_in, w_out, g_qa, w_qb, g_kva, w_kvb):
    b, s, _ = x.shape
    h = x @ w_in
    qa, ka, va, q_lat, kv_lat, k_pe = _split(h, EV_SPLITS)
    qa = rope(qa.reshape(b, s, H_A, HEAD_DIM), positions).transpose(0, 2, 1, 3)
    ka = rope(ka.reshape(b, s, H_A, HEAD_DIM), positions).transpose(0, 2, 1, 3)
    o_a = moba_attention(qa, ka, _heads(va, H_A))
    q = (rms_norm(q_lat, g_qa) @ w_qb).reshape(b, s, H_B, MLA_NOPE + MLA_ROPE)
    q_nope, q_pe = jnp.split(q, [MLA_NOPE], axis=-1)
    kv = (rms_norm(kv_lat, g_kva) @ w_kvb).reshape(b, s, H_B, MLA_NOPE + MLA_V)
    k_nope, v_b = jnp.split(kv, [MLA_NOPE], axis=-1)
    k_pe = rope(k_pe[:, :, None, :], positions)
    q_b = jnp.concatenate([q_nope, rope(q_pe, positions)], -1)
    k_b = jnp.concatenate([k_nope, jnp.broadcast_to(k_pe, (b, s, H_B, MLA_ROPE))], -1)
    o_b = causal_attention(q_b.transpose(0, 2, 1, 3), k_b.transpose(0, 2, 1, 3),
                           v_b.transpose(0, 2, 1, 3), (MLA_NOPE + MLA_ROPE) ** -0.5)
    return jnp.concatenate([_merge(o_a), _merge(o_b)], -1) @ w_out


def odd_mixer(x, positions, w_in, w_out, sinks, lam_p, g_subln, lambda_init):
    b, s, _ = x.shape
    h = x @ w_in
    qc, kc, vc, qd, kd, vd = _split(h, OD_SPLITS)
    qc = rope(qc.reshape(b, s, H_C, HEAD_DIM), positions).transpose(0, 2, 1, 3)
    kc = rope(kc.reshape(b, s, H_C_KV, HEAD_DIM), positions).transpose(0, 2, 1, 3)
    o_c = swa_sinks_attention(qc, kc, _heads(vc, H_C_KV), sinks)
    qd = rope(qd.reshape(b, s, 2 * H_D, HEAD_DIM), positions).reshape(b, s, H_D, 2, HEAD_DIM)
    kd = rope(kd.reshape(b, s, 2 * H_D, HEAD_DIM), positions).reshape(b, s, H_D, 2, HEAD_DIM)
    lp = lam_p.astype(F32)
    lam = jnp.exp(jnp.sum(lp[0] * lp[1])) - jnp.exp(jnp.sum(lp[2] * lp[3])) + lambda_init
    t = lambda a: a.transpose(0, 2, 1, 3)
    o_d = diff_attention(t(qd[:, :, :, 0]), t(qd[:, :, :, 1]), t(kd[:, :, :, 0]), t(kd[:, :, :, 1]),
                         _heads(vd, H_D), lam, HEAD_DIM ** -0.5)
    o_d = rms_norm(o_d, g_subln, eps=1e-5) * (1.0 - lambda_init)
    return jnp.concatenate([_merge(o_c), _merge(o_d)], -1) @ w_out


def _swiglu(x, wg, wu, wd):
    return (jax.nn.silu(x @ wg) * (x @ wu)) @ wd


def routed_experts(xf, eidx, wts, w_gate, w_up, w_down):
    n, d = xf.shape
    a = n * TOP_K
    flat_e = eidx.reshape(-1)
    order = jnp.argsort(flat_e)
    sorted_e = flat_e[order]
    counts = jnp.bincount(flat_e, length=N_EXPERTS)
    padded = (counts + EXPERT_BLOCK - 1) // EXPERT_BLOCK * EXPERT_BLOCK
    start = jnp.cumsum(counts) - counts
    pend = jnp.cumsum(padded)
    pstart = pend - padded
    dest = pstart[sorted_e] + jnp.arange(a) - start[sorted_e]
    p_rows = (-(-a // EXPERT_BLOCK)) * EXPERT_BLOCK + N_EXPERTS * EXPERT_BLOCK
    nblk = p_rows // EXPERT_BLOCK
    tok = jnp.full((p_rows,), n, jnp.int32).at[dest].set((order // TOP_K).astype(jnp.int32))
    gate = jnp.zeros((p_rows,), xf.dtype).at[dest].set(wts.reshape(-1)[order].astype(xf.dtype))
    blk_expert = jnp.minimum(jnp.searchsorted(pend, jnp.arange(nblk) * EXPERT_BLOCK, side='right'),
                             N_EXPERTS - 1)
    xpad = jnp.concatenate([xf, jnp.zeros((1, d), xf.dtype)], 0)

    def one_block(args):
        t, g, e = args
        return _swiglu(xpad[t], w_gate[e], w_up[e], w_down[e]) * g[:, None]

    yb = lax.map(one_block, (tok.reshape(nblk, EXPERT_BLOCK), gate.reshape(nblk, EXPERT_BLOCK), blk_expert))
    y = jnp.zeros((n + 1, d), xf.dtype).at[tok].add(yb.reshape(p_rows, d))
    return y[:n]


def moe_ffn(x, w_router, e_bias, w_gate, w_up, w_down, ws_gate, ws_up, ws_down):
    b, s, d = x.shape
    xf = x.reshape(b * s, d)
    n = xf.shape[0]
    scores = jax.nn.sigmoid(jnp.einsum('nd,de->ne', xf, w_router, preferred_element_type=F32))
    biased = scores + e_bias.astype(F32)
    grp = biased.reshape(n, N_GROUPS, N_EXPERTS // N_GROUPS)
    grp_score = jnp.sum(lax.top_k(grp, 2)[0], -1)
    _, top_grp = lax.top_k(grp_score, TOPK_GROUPS)
    grp_ok = jnp.sum(jax.nn.one_hot(top_grp, N_GROUPS, dtype=F32), axis=1) > 0
    expert_ok = jnp.repeat(grp_ok, N_EXPERTS // N_GROUPS, axis=1)
    _, eidx = lax.top_k(jnp.where(expert_ok, biased, -jnp.inf), TOP_K)
    wts = jnp.take_along_axis(scores, eidx, axis=1)
    wts = wts / jnp.sum(wts, -1, keepdims=True) * ROUTED_SCALE
    y = routed_experts(xf, eidx, wts, w_gate, w_up, w_down) + _swiglu(xf, ws_gate, ws_up, ws_down)
    return y.reshape(b, s, d)


def setup_inputs(seed: int = 0) -> dict:
    key = jax.random.key(seed)
    ks = jax.random.split(key, 22)

    def nrm(k, shape, scale):
        return jax.random.normal(k, shape, F32) * scale

    return {
        'x': nrm(ks[0], (BATCH, SEQ, D_MODEL), 1.0),
        'positions': jnp.broadcast_to(jnp.arange(SEQ, dtype=jnp.int32), (BATCH, SEQ)),
        'ev_w_in': nrm(ks[1], (N_EVEN, D_MODEL, EV_IN), D_MODEL ** -0.5),
        'ev_w_out': nrm(ks[2], (N_EVEN, EV_MIX, D_MODEL), DN_BETA * EV_MIX ** -0.5),
        'mla_g_qa': 1.0 + nrm(ks[3], (N_EVEN, MLA_Q_LORA), 0.02),
        'mla_w_qb': nrm(ks[4], (N_EVEN, MLA_Q_LORA, H_B * (MLA_NOPE + MLA_ROPE)), MLA_Q_LORA ** -0.5),
        'mla_g_kva': 1.0 + nrm(ks[5], (N_EVEN, MLA_KV_LORA), 0.02),
        'mla_w_kvb': nrm(ks[6], (N_EVEN, MLA_KV_LORA, H_B * (MLA_NOPE + MLA_V)), MLA_KV_LORA ** -0.5),
        'od_w_in': nrm(ks[7], (N_ODD, D_MODEL, OD_IN), D_MODEL ** -0.5),
        'od_w_out': nrm(ks[8], (N_ODD, OD_MIX, D_MODEL), DN_BETA * OD_MIX ** -0.5),
        'swa_sinks': nrm(ks[9], (N_ODD, H_C), 0.5),
        'diff_lambda': nrm(ks[10], (N_ODD, 4, HEAD_DIM), 0.1),
        'diff_g_subln': 1.0 + nrm(ks[11], (N_ODD, 2 * HEAD_DIM), 0.02),
        'ln_g': 1.0 + nrm(ks[12], (DEPTH, 2, D_MODEL), 0.02),
        'ln_b': nrm(ks[13], (DEPTH, 2, D_MODEL), 0.02),
        'moe_w_router': nrm(ks[14], (DEPTH, D_MODEL, N_EXPERTS), D_MODEL ** -0.5),
        'moe_bias': nrm(ks[15], (DEPTH, N_EXPERTS), 0.01),
        'moe_w_gate': nrm(ks[16], (DEPTH, N_EXPERTS, D_MODEL, D_EXPERT), D_MODEL ** -0.5),
        'moe_w_up': nrm(ks[17], (DEPTH, N_EXPERTS, D_MODEL, D_EXPERT), D_MODEL ** -0.5),
        'moe_w_down': nrm(ks[18], (DEPTH, N_EXPERTS, D_EXPERT, D_MODEL), DN_BETA * D_EXPERT ** -0.5),
        'sh_w_gate': nrm(ks[19], (DEPTH, D_MODEL, D_SHARED), D_MODEL ** -0.5),
        'sh_w_up': nrm(ks[20], (DEPTH, D_MODEL, D_SHARED), D_MODEL ** -0.5),
        'sh_w_down': nrm(ks[21], (DEPTH, D_SHARED, D_MODEL), DN_BETA * D_SHARED ** -0.5),
    }


def reference(x, positions, ev_w_in, ev_w_out, mla_g_qa, mla_w_qb, mla_g_kva, mla_w_kvb,
              od_w_in, od_w_out, swa_sinks, diff_lambda, diff_g_subln, ln_g, ln_b,
              moe_w_router, moe_bias, moe_w_gate, moe_w_up, moe_w_down, sh_w_gate, sh_w_up, sh_w_down):
    for l in range(DEPTH):
        i = l // 2
        if l % 2 == 0:
            mix = even_mixer(x, positions, ev_w_in[i], ev_w_out[i], mla_g_qa[i], mla_w_qb[i],
                             mla_g_kva[i], mla_w_kvb[i])
        else:
            lambda_init = 0.8 - 0.6 * float(np.exp(-0.3 * l))
            mix = odd_mixer(x, positions, od_w_in[i], od_w_out[i], swa_sinks[i], diff_lambda[i],
                            diff_g_subln[i], lambda_init)
        x = layer_norm(DN_ALPHA * x + mix, ln_g[l, 0], ln_b[l, 0])
        ffn = moe_ffn(x, moe_w_router[l], moe_bias[l], moe_w_gate[l], moe_w_up[l], moe_w_down[l],
                      sh_w_gate[l], sh_w_up[l], sh_w_down[l])
        x = layer_norm(DN_ALPHA * x + ffn, ln_g[l, 1], ln_b[l, 1])
    return x
```

```python
import functools

import numpy as np
import jax
import jax.numpy as jnp
from jax import lax
from jax.experimental import pallas as pl
from jax.experimental.pallas import tpu as pltpu

F32 = jnp.float32
BF16 = jnp.bfloat16
I32 = jnp.int32

LANES = 128
NEG = -1e30

HEAD_DIM = 64
MOBA_BLOCK = 256
MOBA_TOPK = 3
MLA_NOPE = 64
MLA_ROPE = 32
MLA_V = 64
SWA_WINDOW = 128
N_EXPERTS = 64
N_GROUPS = 8
TOPK_GROUPS = 4
TOP_K = 8
ROUTED_SCALE = 2.5
ROPE_THETA = 10000.0

PROJ_ROWS = 512
ATTN_TILE = 256
SWA_ROWS = 512
ROUTER_ROWS = 512
MOE_CHUNK = 2048
MOE_ROWS = 256
VMEM_LIMIT = 56 * 1024 * 1024

_NT = (((1,), (1,)), ((), ()))


def _cparams(*sem):
    return pltpu.CompilerParams(dimension_semantics=sem, vmem_limit_bytes=VMEM_LIMIT)


def _lane_iota(rows):
    return lax.broadcasted_iota(I32, (rows, LANES), 1)


def _rope(x, c, s, first_half, half):
    rot = jnp.where(first_half, pltpu.roll(x, LANES - half, 1), pltpu.roll(x, half, 1))
    return x * c + rot * s


def _rms(x, g, eps):
    return x * lax.rsqrt(jnp.mean(x * x, -1, keepdims=True) + eps) * g


def _proj_kernel(segs, mla, x_ref, w_ref, c64_ref, s64_ref, *refs):
    xb = x_ref[...].astype(BF16)
    rows = xb.shape[0]
    lane = _lane_iota(rows)
    fh64 = (lane % HEAD_DIM) < HEAD_DIM // 2
    c64 = c64_ref[...]
    s64 = s64_ref[...]

    def seg(lo, hi):
        return jnp.dot(xb, w_ref[:, lo:hi], preferred_element_type=F32)

    if mla:
        cm_ref, sm_ref, gq_ref, wqb_ref, gkv_ref, wkk_ref, wkv_ref = refs[:7]
        refs = refs[7:]
    for (lo, hi, rope), o_ref in zip(segs, refs):
        h = seg(lo, hi)
        if rope:
            for g in range((hi - lo) // LANES):
                sl = slice(g * LANES, (g + 1) * LANES)
                o_ref[:, sl] = _rope(h[:, sl], c64, s64, fh64, HEAD_DIM // 2).astype(BF16)
        else:
            o_ref[...] = h.astype(BF16)
    if mla:
        qm_ref, km_ref, vm_ref = refs[len(segs):]
        cm = cm_ref[...]
        sm = sm_ref[...]
        fhm = (lane >= MLA_NOPE) & (lane < MLA_NOPE + MLA_ROPE // 2)
        lo = segs[-1][1]
        qn = _rms(seg(lo, lo + 256), gq_ref[...], 1e-6).astype(BF16)
        qm = jnp.dot(qn, wqb_ref[...], preferred_element_type=F32)
        kvn = _rms(seg(lo + 256, lo + 384), gkv_ref[...], 1e-6).astype(BF16)
        kk = jnp.dot(kvn, wkk_ref[...], preferred_element_type=F32)
        kpe = _rope(seg(lo + 384, lo + 512), cm, sm, fhm, MLA_ROPE // 2)
        for g in range(qm.shape[1] // LANES):
            sl = slice(g * LANES, (g + 1) * LANES)
            qm_ref[:, sl] = _rope(qm[:, sl], cm, sm, fhm, MLA_ROPE // 2).astype(BF16)
            km_ref[:, sl] = (kk[:, sl] + kpe).astype(BF16)
        vm_ref[...] = jnp.dot(kvn, wkv_ref[...], preferred_element_type=F32).astype(BF16)


def _projection(xf, w, segs, tabs, mla_args=None, mla_out_widths=()):
    n, d = xf.shape
    tm = min(PROJ_ROWS, n)
    row = lambda width: pl.BlockSpec((tm, width), lambda i: (i, 0))
    full = lambda a: pl.BlockSpec(a.shape, lambda i: (0,) * a.ndim)
    ins = [xf, w, tabs[0], tabs[1]]
    in_specs = [row(d), full(w), row(LANES), row(LANES)]
    if mla_args is not None:
        cm, sm, gq, wqb, gkv, wkk, wkv = mla_args
        ins += [cm, sm, gq, wqb, gkv, wkk, wkv]
        in_specs += [row(LANES), row(LANES), full(gq), full(wqb), full(gkv), full(wkk), full(wkv)]
    widths = [hi - lo for lo, hi, _ in segs] + list(mla_out_widths)
    return pl.pallas_call(
        functools.partial(_proj_kernel, tuple(segs), mla_args is not None),
        grid=(n // tm,),
        in_specs=in_specs,
        out_specs=[row(wd) for wd in widths],
        out_shape=[jax.ShapeDtypeStruct((n, wd), BF16) for wd in widths],
        compiler_params=_cparams("parallel"),
        name="projection",
    )(*ins)


def _softmax_step(q, k, v, carry, bias):
    m, l, acc = carry
    s = lax.dot_general(q, k, _NT, preferred_element_type=F32)
    if bias is not None:
        s = s + bias
    m_new = jnp.maximum(m, jnp.max(s, -1, keepdims=True))
    alpha = jnp.exp(m - m_new)
    p = jnp.exp(s - m_new)
    l = alpha * l + jnp.sum(p, -1, keepdims=True)
    acc = alpha * acc + jnp.dot(p.astype(BF16), v, preferred_element_type=F32)
    return m_new, l, acc


def _softmax_init(rows):
    return (jnp.full((rows, 1), NEG, F32), jnp.zeros((rows, 1), F32), jnp.zeros((rows, LANES), F32))


def _causal_bias(t):
    r = lax.broadcasted_iota(I32, (t, t), 0)
    c = lax.broadcasted_iota(I32, (t, t), 1)
    return jnp.where(c <= r, 0.0, NEG).astype(F32)


def _mla_kernel(q_ref, k_ref, v_ref, o_ref, *, t, scale):
    qi = pl.program_id(2)
    dbias = _causal_bias(t)
    lane = _lane_iota(t)
    res = None
    for h in range(2):
        hs = slice(h * LANES, (h + 1) * LANES)
        q = (q_ref[:, hs].astype(F32) * scale).astype(BF16)

        def step(j, carry, bias):
            ks = pl.multiple_of(j * t, t)
            return _softmax_step(q, k_ref[pl.ds(ks, t), hs], v_ref[pl.ds(ks, t), :], carry, bias)

        carry = step(qi, _softmax_init(t), dbias)
        m, l, acc = lax.fori_loop(0, qi, lambda j, cr: step(j, cr, None), carry)
        o = acc / l
        res = o if h == 0 else jnp.where(lane < MLA_V, res, o)
    o_ref[...] = res.astype(BF16)


def _mla_attention(qm, km, vm, b, s):
    t = min(ATTN_TILE, s)
    nq = s // t
    hp = qm.shape[1] // (2 * LANES)
    return pl.pallas_call(
        functools.partial(_mla_kernel, t=t, scale=float((MLA_NOPE + MLA_ROPE) ** -0.5)),
        grid=(b, hp, nq),
        in_specs=[pl.BlockSpec((t, 2 * LANES), lambda bi, h, qi: (bi * nq + qi, h)),
                  pl.BlockSpec((s, 2 * LANES), lambda bi, h, qi: (bi, h)),
                  pl.BlockSpec((s, LANES), lambda bi, h, qi: (bi, h))],
        out_specs=pl.BlockSpec((t, LANES), lambda bi, h, qi: (bi * nq + qi, h)),
        out_shape=jax.ShapeDtypeStruct((b * s, hp * LANES), BF16),
        compiler_params=_cparams("parallel", "parallel", "arbitrary"),
        name="mla_attention",
    )(qm, km, vm)


def _moba_kernel(q_ref, k_ref, v_ref, o_ref, kmean_ref, *, nb):
    j = pl.program_id(2)
    bs = MOBA_BLOCK

    @pl.when(j == 0)
    def _():
        kmean_ref[...] = jnp.zeros_like(kmean_ref)
        for n in range(nb):
            blk = k_ref[n * bs:(n + 1) * bs, :].astype(F32)
            kmean_ref[n:n + 1, :] = jnp.sum(blk, 0, keepdims=True) * (1.0 / bs)

    lane = _lane_iota(bs)
    dbias = _causal_bias(bs)
    qf = q_ref[...].astype(F32) * (HEAD_DIM ** -0.5)
    kmean = kmean_ref[...]
    res = None
    for h in range(2):
        qh = jnp.where((lane >= h * HEAD_DIM) & (lane < (h + 1) * HEAD_DIM), qf, 0.0)
        gate = lax.dot_general(qh, kmean, _NT, preferred_element_type=F32, precision=lax.Precision.HIGHEST)
        g = jnp.where(lane < j, gate, -jnp.inf)
        sel = lane < 0
        for _ in range(MOBA_TOPK):
            mx = jnp.max(g, -1, keepdims=True)
            idx = jnp.min(jnp.where(g == mx, lane, LANES), -1, keepdims=True)
            hit = lane == idx
            sel = sel | (hit & (lane < j))
            g = jnp.where(hit, -jnp.inf, g)
        selb = jnp.where(sel, 0.0, NEG).astype(F32)
        qb = qh.astype(BF16)

        def step(n, carry, bias):
            ks = pl.multiple_of(n * bs, bs)
            return _softmax_step(qb, k_ref[pl.ds(ks, bs), :], v_ref[pl.ds(ks, bs), :], carry, bias)

        def past(n, carry):
            bn = jnp.sum(jnp.where(lane == n, selb, 0.0), -1, keepdims=True)
            return step(n, carry, bn)

        carry = step(j, _softmax_init(bs), dbias)
        m, l, acc = lax.fori_loop(0, j, past, carry)
        o = acc / l
        res = o if h == 0 else jnp.where(lane < HEAD_DIM, res, o)
    o_ref[...] = res.astype(BF16)


def _moba_attention(q, k, v, b, s):
    bs = MOBA_BLOCK
    nb = s // bs
    hp = q.shape[1] // LANES
    return pl.pallas_call(
        functools.partial(_moba_kernel, nb=nb),
        grid=(b, hp, nb),
        in_specs=[pl.BlockSpec((bs, LANES), lambda bi, h, qi: (bi * nb + qi, h)),
                  pl.BlockSpec((s, LANES), lambda bi, h, qi: (bi, h)),
                  pl.BlockSpec((s, LANES), lambda bi, h, qi: (bi, h))],
        out_specs=pl.BlockSpec((bs, LANES), lambda bi, h, qi: (bi * nb + qi, h)),
        out_shape=jax.ShapeDtypeStruct((b * s, hp * LANES), BF16),
        scratch_shapes=[pltpu.VMEM((LANES, LANES), F32)],
        compiler_params=_cparams("parallel", "parallel", "arbitrary"),
        name="moba_attention",
    )(q, k, v)


def _diff_kernel(q_ref, k_ref, v_ref, lam_ref, g_ref, o_ref, *, t, lambda_init):
    qi = pl.program_id(2)
    dbias = _causal_bias(t)
    lane = _lane_iota(t)
    qf = q_ref[...].astype(F32) * (HEAD_DIM ** -0.5)
    q1 = jnp.where(lane < HEAD_DIM, qf, 0.0).astype(BF16)
    q2 = jnp.where(lane >= HEAD_DIM, qf, 0.0).astype(BF16)

    def step(j, carry, bias):
        ks = pl.multiple_of(j * t, t)
        k = k_ref[pl.ds(ks, t), :]
        v = v_ref[pl.ds(ks, t), :]
        return (_softmax_step(q1, k, v, carry[0], bias), _softmax_step(q2, k, v, carry[1], bias))

    carry = step(qi, (_softmax_init(t), _softmax_init(t)), dbias)
    (_, l1, a1), (_, l2, a2) = lax.fori_loop(0, qi, lambda j, cr: step(j, cr, None), carry)
    lp = lam_ref[...]
    lam = (jnp.exp(jnp.sum(lp[0:1] * lp[1:2], -1, keepdims=True))
           - jnp.exp(jnp.sum(lp[2:3] * lp[3:4], -1, keepdims=True)) + lambda_init)
    o = a1 / l1 - lam * (a2 / l2)
    o_ref[...] = (_rms(o, g_ref[...], 1e-5) * (1.0 - lambda_init)).astype(BF16)


def _diff_attention(q, k, v, lam_p, g_subln, lambda_init, b, s):
    t = min(ATTN_TILE, s)
    nq = s // t
    nh = q.shape[1] // LANES
    return pl.pallas_call(
        functools.partial(_diff_kernel, t=t, lambda_init=lambda_init),
        grid=(b, nh, nq),
        in_specs=[pl.BlockSpec((t, LANES), lambda bi, h, qi: (bi * nq + qi, h)),
                  pl.BlockSpec((s, LANES), lambda bi, h, qi: (bi, h)),
                  pl.BlockSpec((s, LANES), lambda bi, h, qi: (bi, h)),
                  pl.BlockSpec(lam_p.shape, lambda bi, h, qi: (0, 0)),
                  pl.BlockSpec(g_subln.shape, lambda bi, h, qi: (0, 0))],
        out_specs=pl.BlockSpec((t, LANES), lambda bi, h, qi: (bi * nq + qi, h)),
        out_shape=jax.ShapeDtypeStruct((b * s, nh * LANES), BF16),
        compiler_params=_cparams("parallel", "parallel", "arbitrary"),
        name="diff_attention",
    )(q, k, v, lam_p, g_subln)


def _swa_kernel(sink_ref, q_ref, k_ref, v_ref, o_ref, *, rows):
    ti = pl.program_id(1)
    w = SWA_WINDOW
    nsub = rows // w
    n_q = q_ref.shape[1] // HEAD_DIM
    n_kv = k_ref.shape[1] // HEAD_DIM
    grp = n_q // n_kv
    lane_k = lax.broadcasted_iota(I32, (2 * w, LANES), 1)
    lane_o = _lane_iota(w)
    qrow = lax.broadcasted_iota(I32, (grp * w, 2 * w), 0) % w
    kcol = lax.broadcasted_iota(I32, (grp * w, 2 * w), 1)
    for i in range(nsub):
        sb = ti * nsub + i
        start = pl.multiple_of(jnp.maximum(sb - 1, 0) * w, w)
        kb = k_ref[pl.ds(start, 2 * w), :]
        vb = v_ref[pl.ds(start, 2 * w), :]
        rel = qrow + (sb * w - start) - kcol
        bias = jnp.where((rel >= 0) & (rel < w), 0.0, NEG).astype(F32)
        qf = q_ref[i * w:(i + 1) * w, :].astype(F32) * (HEAD_DIM ** -0.5)
        head_out = [None] * n_q
        for kvh in range(n_kv):
            km = jnp.where(lane_k // HEAD_DIM == kvh, kb.astype(F32), 0.0).astype(BF16)
            qs, sinks = [], []
            for a in range(grp):
                h = kvh * grp + a
                qp = qf[:, (h // 2) * LANES:(h // 2 + 1) * LANES]
                if h % 2 != kvh:
                    qp = pltpu.roll(qp, HEAD_DIM, 1)
                qs.append(qp.astype(BF16))
                sinks.append(jnp.full((w, 1), sink_ref[h], F32))
            sink = jnp.concatenate(sinks, 0)
            s = lax.dot_general(jnp.concatenate(qs, 0), km, _NT, preferred_element_type=F32) + bias
            m = jnp.maximum(jnp.max(s, -1, keepdims=True), sink)
            p = jnp.exp(s - m)
            den = jnp.sum(p, -1, keepdims=True) + jnp.exp(sink - m)
            o = jnp.dot(p.astype(BF16), vb, preferred_element_type=F32) / den
            for a in range(grp):
                h = kvh * grp + a
                oh = o[a * w:(a + 1) * w, :]
                if h % 2 != kvh:
                    oh = pltpu.roll(oh, HEAD_DIM, 1)
                head_out[h] = oh
        for hp in range(n_q // 2):
            pair = jnp.where(lane_o < HEAD_DIM, head_out[2 * hp], head_out[2 * hp + 1])
            o_ref[i * w:(i + 1) * w, hp * LANES:(hp + 1) * LANES] = pair.astype(BF16)


def _swa_attention(q, k, v, sinks, b, s):
    rows = min(SWA_ROWS, s)
    nt = s // rows
    return pl.pallas_call(
        functools.partial(_swa_kernel, rows=rows),
        grid=(b, nt),
        in_specs=[pl.BlockSpec(memory_space=pltpu.SMEM),
                  pl.BlockSpec((rows, q.shape[1]), lambda bi, ti: (bi * nt + ti, 0)),
                  pl.BlockSpec((s, k.shape[1]), lambda bi, ti: (bi, 0)),
                  pl.BlockSpec((s, v.shape[1]), lambda bi, ti: (bi, 0))],
        out_specs=pl.BlockSpec((rows, q.shape[1]), lambda bi, ti: (bi * nt + ti, 0)),
        out_shape=jax.ShapeDtypeStruct(q.shape, BF16),
        compiler_params=_cparams("parallel", "arbitrary"),
        name="swa_attention",
    )(sinks, q, k, v)


def _layer_norm(y, g, b):
    mu = jnp.mean(y, -1, keepdims=True)
    yc = y - mu
    var = jnp.mean(yc * yc, -1, keepdims=True)
    return yc * lax.rsqrt(var + 1e-5) * g + b


def _out_ln_kernel(x_ref, a_ref, b_ref, w_ref, g_ref, beta_ref, o_ref, *, alpha):
    ka = a_ref.shape[1]
    mix = (jnp.dot(a_ref[...], w_ref[:ka, :], preferred_element_type=F32)
           + jnp.dot(b_ref[...], w_ref[ka:, :], preferred_element_type=F32))
    o_ref[...] = _layer_norm(alpha * x_ref[...] + mix, g_ref[...], beta_ref[...])


def _out_ln(xf, a, b, w, g, beta, alpha):
    n, d = xf.shape
    tm = min(PROJ_ROWS, n)
    row = lambda width: pl.BlockSpec((tm, width), lambda i: (i, 0))
    full = lambda arr: pl.BlockSpec(arr.shape, lambda i: (0, 0))
    return pl.pallas_call(
        functools.partial(_out_ln_kernel, alpha=alpha),
        grid=(n // tm,),
        in_specs=[row(d), row(a.shape[1]), row(b.shape[1]), full(w), full(g), full(beta)],
        out_specs=row(d),
        out_shape=jax.ShapeDtypeStruct((n, d), F32),
        compiler_params=_cparams("parallel"),
        name="out_proj_ln",
    )(xf, a, b, w, g, beta)


def _res_ln_kernel(x_ref, y_ref, g_ref, beta_ref, o_ref, *, alpha):
    o_ref[...] = _layer_norm(alpha * x_ref[...] + y_ref[...], g_ref[...], beta_ref[...])


def _res_ln(xf, y, g, beta, alpha):
    n, d = xf.shape
    tm = min(2 * PROJ_ROWS, n)
    row = pl.BlockSpec((tm, d), lambda i: (i, 0))
    full = lambda arr: pl.BlockSpec(arr.shape, lambda i: (0, 0))
    return pl.pallas_call(
        functools.partial(_res_ln_kernel, alpha=alpha),
        grid=(n // tm,),
        in_specs=[row, row, full(g), full(beta)],
        out_specs=row,
        out_shape=jax.ShapeDtypeStruct((n, d), F32),
        compiler_params=_cparams("parallel"),
        name="residual_ln",
    )(xf, y, g, beta)


def _lane_argmax(v, lane):
    mx = jnp.max(v, -1, keepdims=True)
    idx = jnp.min(jnp.where(v == mx, lane, LANES), -1, keepdims=True)
    return mx, idx


def _router_kernel(x_ref, w_ref, bias_ref, eidx_ref, wts_ref):
    logits = jnp.dot(x_ref[...], w_ref[...], preferred_element_type=F32, precision=lax.Precision.HIGHEST)
    rows = logits.shape[0]
    lane = _lane_iota(rows)
    scores = jax.nn.sigmoid(logits)
    biased = jnp.where(lane < N_EXPERTS, scores + bias_ref[...], -jnp.inf)
    per_group = N_EXPERTS // N_GROUPS
    grp = lane // per_group
    gscore = jnp.full((rows, LANES), -jnp.inf, F32)
    for g in range(N_GROUPS):
        v = jnp.where(grp == g, biased, -jnp.inf)
        m1, i1 = _lane_argmax(v, lane)
        m2 = jnp.max(jnp.where(lane == i1, -jnp.inf, v), -1, keepdims=True)
        gscore = jnp.where(lane == g, m1 + m2, gscore)
    ok = lane < 0
    for _ in range(TOPK_GROUPS):
        _, ig = _lane_argmax(gscore, lane)
        ok = ok | (grp == ig)
        gscore = jnp.where(lane == ig, -jnp.inf, gscore)
    cand = jnp.where(ok & (lane < N_EXPERTS), biased, -jnp.inf)
    eidx = jnp.zeros((rows, LANES), I32)
    w = jnp.zeros((rows, LANES), F32)
    wsum = jnp.zeros((rows, 1), F32)
    for k in range(TOP_K):
        _, ie = _lane_argmax(cand, lane)
        hit = lane == ie
        sc = jnp.sum(jnp.where(hit, scores, 0.0), -1, keepdims=True)
        eidx = jnp.where(lane == k, ie, eidx)
        w = jnp.where(lane == k, sc, w)
        wsum = wsum + sc
        cand = jnp.where(hit, -jnp.inf, cand)
    eidx_ref[...] = eidx
    wts_ref[...] = w / wsum * ROUTED_SCALE


def _router(xf, w_router, e_bias):
    n, d = xf.shape
    tm = min(ROUTER_ROWS, n)
    wpad = jnp.pad(w_router, ((0, 0), (0, LANES - N_EXPERTS)))
    bpad = jnp.pad(e_bias.astype(F32), (0, LANES - N_EXPERTS)).reshape(1, LANES)
    row = lambda width: pl.BlockSpec((tm, width), lambda i: (i, 0))
    eidx, wts = pl.pallas_call(
        _router_kernel,
        grid=(n // tm,),
        in_specs=[row(d), pl.BlockSpec(wpad.shape, lambda i: (0, 0)), pl.BlockSpec(bpad.shape, lambda i: (0, 0))],
        out_specs=[row(LANES), row(LANES)],
        out_shape=[jax.ShapeDtypeStruct((n, LANES), I32), jax.ShapeDtypeStruct((n, LANES), F32)],
        compiler_params=_cparams("parallel"),
        name="router",
    )(xf, wpad, bpad)
    return eidx[:, :TOP_K], wts[:, :TOP_K]


def _dispatch(eidx, wts, chunk, rows):
    n = eidx.shape[0]
    nc = n // chunk
    a = chunk * TOP_K
    nblk = a // rows + N_EXPERTS
    ec = eidx.reshape(nc, a)
    wc = wts.reshape(nc, a)
    order = jnp.argsort(ec, axis=1, stable=True).astype(I32)
    sorted_e = jnp.take_along_axis(ec, order, axis=1)
    ends = jax.vmap(lambda srt: jnp.searchsorted(srt, jnp.arange(N_EXPERTS, dtype=I32), side="right"))(sorted_e)
    ends = ends.astype(I32)
    starts = jnp.concatenate([jnp.zeros((nc, 1), I32), ends[:, :-1]], axis=1)
    counts = ends - starts
    nb_e = (counts + rows - 1) // rows
    bend = jnp.cumsum(nb_e, axis=1).astype(I32)
    bstart = bend - nb_e
    used = bend[:, -1:]
    barange = jnp.arange(nblk, dtype=I32)[None, :]
    be = jax.vmap(lambda e: jnp.searchsorted(e, barange[0], side="right"))(bend).astype(I32)
    be = jnp.minimum(be, N_EXPERTS - 1)
    be_last = jnp.take_along_axis(be, used - 1, axis=1)
    live = barange < used
    be = jnp.where(live, be, be_last)
    within = barange - jnp.take_along_axis(bstart, be, axis=1)
    brows = jnp.clip(jnp.take_along_axis(counts, be, axis=1) - within * rows, 0, rows)
    brows = jnp.where(live, brows, 0).astype(I32)
    src0 = jnp.take_along_axis(starts, be, axis=1) + within * rows
    r = jnp.arange(rows, dtype=I32)[None, None, :]
    valid = r < brows[:, :, None]
    src = jnp.where(valid, src0[:, :, None] + r, 0).reshape(nc, nblk * rows)
    slot = jnp.take_along_axis(order, src, axis=1)
    valid = valid.reshape(nc, nblk * rows)
    tok = jnp.where(valid, slot // TOP_K, chunk).astype(I32)
    gate = jnp.where(valid, jnp.take_along_axis(wc, slot, axis=1), 0.0).astype(F32)
    return (be.reshape(-1), brows.reshape(-1), tok.reshape(nc * nblk, 1, rows), gate.reshape(nc * nblk, 1, rows), nblk)


def _swiglu(xb, wg, wu):
    hg = jnp.dot(xb, wg, preferred_element_type=F32)
    hu = jnp.dot(xb, wu, preferred_element_type=F32)
    return hg * jax.nn.sigmoid(hg) * hu


def _moe_kernel(be_ref, br_ref, tok_ref, gate_ref, x_ref, wg_ref, wu_ref, wd_ref, sg_ref, su_ref, sd_ref,
                y_ref, acc_ref, xg_ref, og_ref, *, chunk, rows):
    c = pl.program_id(0)
    b = pl.program_id(1)
    nblk = pl.num_programs(1)
    group = 8

    @pl.when((c == 0) & (b == 0))
    def _():
        xg_ref[...] = jnp.zeros_like(xg_ref)

    @pl.when(b == 0)
    def _():
        sub = min(512, chunk)
        for s0 in range(0, chunk, sub):
            xb = x_ref[s0:s0 + sub, :].astype(BF16)
            h = _swiglu(xb, sg_ref[...], su_ref[...]).astype(BF16)
            acc_ref[s0:s0 + sub, :] = jnp.dot(h, sd_ref[...], preferred_element_type=F32)
        acc_ref[chunk:chunk + group, :] = jnp.zeros((group, acc_ref.shape[1]), F32)

    nrow = br_ref[c * nblk + b]

    @pl.when(nrow > 0)
    def _():
        ngrp = (nrow + group - 1) // group

        def gather(gi, carry):
            base = pl.multiple_of(gi * group, group)
            for i in range(group):
                t = jnp.minimum(tok_ref[0, 0, base + i], chunk - 1)
                xg_ref[pl.ds(base + i, 1), :] = x_ref[pl.ds(t, 1), :]
            return carry

        lax.fori_loop(0, ngrp, gather, 0)
        xb = xg_ref[...].astype(BF16)
        h = _swiglu(xb, wg_ref[0], wu_ref[0])
        eye = lax.broadcasted_iota(I32, (rows, rows), 0) == lax.broadcasted_iota(I32, (rows, rows), 1)
        gcol = jnp.sum(jnp.where(eye, gate_ref[0], 0.0), -1, keepdims=True)
        og_ref[...] = jnp.dot((h * gcol).astype(BF16), wd_ref[0], preferred_element_type=F32)

        def scatter(gi, carry):
            base = pl.multiple_of(gi * group, group)
            toks = [tok_ref[0, 0, base + i] for i in range(group)]
            vals = [acc_ref[pl.ds(toks[i], 1), :] + og_ref[pl.ds(base + i, 1), :] for i in range(group)]
            for i in range(group):
                acc_ref[pl.ds(toks[i], 1), :] = vals[i]
            return carry

        lax.fori_loop(0, ngrp, scatter, 0)

    @pl.when(b == nblk - 1)
    def _():
        y_ref[...] = acc_ref[0:chunk, :]


def _moe(xf, eidx, wts, wg, wu, wd, sg, su, sd):
    n, d = xf.shape
    chunk = min(MOE_CHUNK, n)
    rows = MOE_ROWS
    be, brows, tok, gate, nblk = _dispatch(eidx, wts, chunk, rows)
    nc = n // chunk
    de = wg.shape[2]
    full = lambda arr: pl.BlockSpec(arr.shape, lambda c, b, be_r, br_r: (0,) * arr.ndim)
    grid_spec = pltpu.PrefetchScalarGridSpec(
        num_scalar_prefetch=2,
        grid=(nc, nblk),
        in_specs=[
            pl.BlockSpec((1, 1, rows), lambda c, b, be_r, br_r: (c * nblk + b, 0, 0), memory_space=pltpu.SMEM),
            pl.BlockSpec((1, 1, rows), lambda c, b, be_r, br_r: (c * nblk + b, 0, 0)),
            pl.BlockSpec((chunk, d), lambda c, b, be_r, br_r: (c, 0)),
            pl.BlockSpec((1, d, de), lambda c, b, be_r, br_r: (be_r[c * nblk + b], 0, 0)),
            pl.BlockSpec((1, d, de), lambda c, b, be_r, br_r: (be_r[c * nblk + b], 0, 0)),
            pl.BlockSpec((1, de, d), lambda c, b, be_r, br_r: (be_r[c * nblk + b], 0, 0)),
            full(sg), full(su), full(sd),
        ],
        out_specs=pl.BlockSpec((chunk, d), lambda c, b, be_r, br_r: (c, 0)),
        scratch_shapes=[pltpu.VMEM((chunk + 8, d), F32), pltpu.VMEM((rows, d), F32), pltpu.VMEM((rows, d), F32)],
    )
    return pl.pallas_call(
        functools.partial(_moe_kernel, chunk=chunk, rows=rows),
        grid_spec=grid_spec,
        out_shape=jax.ShapeDtypeStruct((n, d), F32),
        compiler_params=_cparams("arbitrary", "arbitrary"),
        name="moe_experts",
    )(be, brows, tok, gate, xf, wg, wu, wd, sg, su, sd)


def _rope_tables(positions):
    pos = positions.reshape(-1).astype(F32)[:, None]
    n = pos.shape[0]

    def cs(dim):
        inv = ROPE_THETA ** (-jnp.arange(0, dim, 2, dtype=F32) / dim)
        ang = pos * inv
        return jnp.cos(ang), jnp.sin(ang)

    c, s = cs(HEAD_DIM)
    c64 = jnp.concatenate([c, c, c, c], -1)
    s64 = jnp.concatenate([-s, s, -s, s], -1)
    c, s = cs(MLA_ROPE)
    one = jnp.ones((n, MLA_NOPE), F32)
    pad = LANES - MLA_NOPE - MLA_ROPE
    cm = jnp.concatenate([one, c, c, jnp.ones((n, pad), F32)], -1)
    sm = jnp.concatenate([0 * one, -s, s, jnp.zeros((n, pad), F32)], -1)
    return (c64, s64), (cm, sm)


def _even_mixer(xf, b, s, tabs, w_in, g_qa, w_qb, g_kva, w_kvb):
    d = xf.shape[1]
    tab64, tabm = tabs
    na = 3 * 512
    kpe_lo = na + 256 + 128
    wpad = jnp.concatenate([w_in[:, :kpe_lo], jnp.zeros((d, MLA_NOPE), F32), w_in[:, kpe_lo:],
                            jnp.zeros((d, LANES - MLA_NOPE - MLA_ROPE), F32)], 1).astype(BF16)
    nh = w_qb.shape[1] // (MLA_NOPE + MLA_ROPE)
    wqb = jnp.pad(w_qb.reshape(-1, nh, MLA_NOPE + MLA_ROPE), ((0, 0), (0, 0), (0, LANES - MLA_NOPE - MLA_ROPE)))
    wqb = wqb.reshape(-1, nh * LANES).astype(BF16)
    wkv3 = w_kvb.reshape(-1, nh, MLA_NOPE + MLA_V)
    wkk = jnp.pad(wkv3[:, :, :MLA_NOPE], ((0, 0), (0, 0), (0, LANES - MLA_NOPE))).reshape(-1, nh * LANES).astype(BF16)
    wkv = wkv3[:, :, MLA_NOPE:].reshape(-1, nh * MLA_V).astype(BF16)
    segs = [(0, 512, True), (512, 1024, True), (1024, 1536, False)]
    qa, ka, va, qm, km, vm = _projection(
        xf, wpad, segs, tab64,
        mla_args=(tabm[0], tabm[1], g_qa.reshape(1, -1), wqb, g_kva.reshape(1, -1), wkk, wkv),
        mla_out_widths=(nh * LANES, nh * LANES, nh * MLA_V))
    o_a = _moba_attention(qa, ka, va, b, s)
    o_b = _mla_attention(qm, km, vm, b, s)
    return o_a, o_b


def _odd_mixer(xf, b, s, tabs, w_in, sinks, lam_p, g_subln, lambda_init):
    segs = [(0, 512, True), (512, 640, True), (640, 768, False),
            (768, 1280, True), (1280, 1792, True), (1792, 2304, False)]
    qc, kc, vc, qd, kd, vd = _projection(xf, w_in.astype(BF16), segs, tabs[0])
    o_c = _swa_attention(qc, kc, vc, sinks.astype(F32), b, s)
    o_d = _diff_attention(qd, kd, vd, lam_p.astype(F32), g_subln.reshape(1, -1), lambda_init, b, s)
    return o_c, o_d


def kernel(x, positions, ev_w_in, ev_w_out, mla_g_qa, mla_w_qb, mla_g_kva, mla_w_kvb, od_w_in, od_w_out, swa_sinks, diff_lambda, diff_g_subln, ln_g, ln_b, moe_w_router, moe_bias, moe_w_gate, moe_w_up, moe_w_down, sh_w_gate, sh_w_up, sh_w_down):
    b, s, d = x.shape
    depth = ln_g.shape[0]
    alpha = float((2 * depth) ** 0.25)
    xf = x.reshape(b * s, d)
    tabs = _rope_tables(positions)
    for l in range(depth):
        i = l // 2
        if l % 2 == 0:
            o1, o2 = _even_mixer(xf, b, s, tabs, ev_w_in[i], mla_g_qa[i], mla_w_qb[i], mla_g_kva[i], mla_w_kvb[i])
            w_out = ev_w_out[i]
        else:
            lambda_init = 0.8 - 0.6 * float(np.exp(-0.3 * l))
            o1, o2 = _odd_mixer(xf, b, s, tabs, od_w_in[i], swa_sinks[i], diff_lambda[i], diff_g_subln[i], lambda_init)
            w_out = od_w_out[i]
        xf = _out_ln(xf, o1, o2, w_out.astype(BF16), ln_g[l, 0].reshape(1, d), ln_b[l, 0].reshape(1, d), alpha)
        eidx, wts = _router(xf, moe_w_router[l], moe_bias[l])
        y = _moe(xf, eidx, wts, moe_w_gate[l].astype(BF16), moe_w_up[l].astype(BF16), moe_w_down[l].astype(BF16),
                 sh_w_gate[l].astype(BF16), sh_w_up[l].astype(BF16), sh_w_down[l].astype(BF16))
        xf = _res_ln(xf, y, ln_g[l, 1].reshape(1, d), ln_b[l, 1].reshape(1, d), alpha)
    return xf.reshape(b, s, d)
```

```python
import functools

import numpy as np
import jax
import jax.numpy as jnp
from jax import lax
from jax.experimental import pallas as pl
from jax.experimental.pallas import tpu as pltpu

F32 = jnp.float32
BF16 = jnp.bfloat16
I32 = jnp.int32

LANES = 128
NEG = -1e30

HEAD_DIM = 64
MOBA_BLOCK = 256
MOBA_TOPK = 3
MLA_NOPE = 64
MLA_ROPE = 32
MLA_V = 64
SWA_WINDOW = 128
N_EXPERTS = 64
N_GROUPS = 8
TOPK_GROUPS = 4
TOP_K = 8
ROUTED_SCALE = 2.5
ROPE_THETA = 10000.0

PROJ_ROWS = 512
ATTN_TILE = 256
SWA_ROWS = 512
ROUTER_ROWS = 512
MOE_CHUNK = 2048
MOE_ROWS = 256
VMEM_LIMIT = 56 * 1024 * 1024

_NT = (((1,), (1,)), ((), ()))


def _cparams(*sem):
    return pltpu.CompilerParams(dimension_semantics=sem, vmem_limit_bytes=VMEM_LIMIT)


def _lane_iota(rows):
    return lax.broadcasted_iota(I32, (rows, LANES), 1)


def _rope(x, c, s, first_half, half):
    rot = jnp.where(first_half, pltpu.roll(x, LANES - half, 1), pltpu.roll(x, half, 1))
    return x * c + rot * s


def _rms(x, g, eps):
    return x * lax.rsqrt(jnp.mean(x * x, -1, keepdims=True) + eps) * g


def _proj_kernel(segs, mla, x_ref, w_ref, c64_ref, s64_ref, *refs):
    xb = x_ref[...].astype(BF16)
    rows = xb.shape[0]
    lane = _lane_iota(rows)
    fh64 = (lane % HEAD_DIM) < HEAD_DIM // 2
    c64 = c64_ref[...]
    s64 = s64_ref[...]

    def seg(lo, hi):
        return jnp.dot(xb, w_ref[:, lo:hi], preferred_element_type=F32)

    if mla:
        cm_ref, sm_ref, gq_ref, wqb_ref, gkv_ref, wkk_ref, wkv_ref = refs[:7]
        refs = refs[7:]
    for (lo, hi, rope), o_ref in zip(segs, refs):
        h = seg(lo, hi)
        if rope:
            for g in range((hi - lo) // LANES):
                sl = slice(g * LANES, (g + 1) * LANES)
                o_ref[:, sl] = _rope(h[:, sl], c64, s64, fh64, HEAD_DIM // 2).astype(BF16)
        else:
            o_ref[...] = h.astype(BF16)
    if mla:
        qm_ref, km_ref, vm_ref = refs[len(segs):]
        cm = cm_ref[...]
        sm = sm_ref[...]
        fhm = (lane >= MLA_NOPE) & (lane < MLA_NOPE + MLA_ROPE // 2)
        lo = segs[-1][1]
        qn = _rms(seg(lo, lo + 256), gq_ref[...], 1e-6).astype(BF16)
        qm = jnp.dot(qn, wqb_ref[...], preferred_element_type=F32)
        kvn = _rms(seg(lo + 256, lo + 384), gkv_ref[...], 1e-6).astype(BF16)
        kk = jnp.dot(kvn, wkk_ref[...], preferred_element_type=F32)
        kpe = _rope(seg(lo + 384, lo + 512), cm, sm, fhm, MLA_ROPE // 2)
        for g in range(qm.shape[1] // LANES):
            sl = slice(g * LANES, (g + 1) * LANES)
            qm_ref[:, sl] = _rope(qm[:, sl], cm, sm, fhm, MLA_ROPE // 2).astype(BF16)
            km_ref[:, sl] = (kk[:, sl] + kpe).astype(BF16)
        vm_ref[...] = jnp.dot(kvn, wkv_ref[...], preferred_element_type=F32).astype(BF16)


def _projection(xf, w, segs, tabs, mla_args=None, mla_out_widths=()):
    n, d = xf.shape
    tm = min(PROJ_ROWS, n)
    row = lambda width: pl.BlockSpec((tm, width), lambda i: (i, 0))
    full = lambda a: pl.BlockSpec(a.shape, lambda i: (0,) * a.ndim)
    ins = [xf, w, tabs[0], tabs[1]]
    in_specs = [row(d), full(w), row(LANES), row(LANES)]
    if mla_args is not None:
        cm, sm, gq, wqb, gkv, wkk, wkv = mla_args
        ins += [cm, sm, gq, wqb, gkv, wkk, wkv]
        in_specs += [row(LANES), row(LANES), full(gq), full(wqb), full(gkv), full(wkk), full(wkv)]
    widths = [hi - lo for lo, hi, _ in segs] + list(mla_out_widths)
    return pl.pallas_call(
        functools.partial(_proj_kernel, tuple(segs), mla_args is not None),
        grid=(n // tm,),
        in_specs=in_specs,
        out_specs=[row(wd) for wd in widths],
        out_shape=[jax.ShapeDtypeStruct((n, wd), BF16) for wd in widths],
        compiler_params=_cparams("parallel"),
        name="projection",
    )(*ins)


def _softmax_step(q, k, v, carry, bias):
    m, l, acc = carry
    s = lax.dot_general(q, k, _NT, preferred_element_type=F32)
    if bias is not None:
        s = s + bias
    m_new = jnp.maximum(m, jnp.max(s, -1, keepdims=True))
    alpha = jnp.exp(m - m_new)
    p = jnp.exp(s - m_new)
    l = alpha * l + jnp.sum(p, -1, keepdims=True)
    acc = alpha * acc + jnp.dot(p.astype(BF16), v, preferred_element_type=F32)
    return m_new, l, acc


def _softmax_init(rows):
    return (jnp.full((rows, 1), NEG, F32), jnp.zeros((rows, 1), F32), jnp.zeros((rows, LANES), F32))


def _causal_bias(t):
    r = lax.broadcasted_iota(I32, (t, t), 0)
    c = lax.broadcasted_iota(I32, (t, t), 1)
    return jnp.where(c <= r, 0.0, NEG).astype(F32)


def _mla_kernel(q_ref, k_ref, v_ref, o_ref, *, t, scale):
    qi = pl.program_id(2)
    dbias = _causal_bias(t)
    lane = _lane_iota(t)
    res = None
    for h in range(2):
        hs = slice(h * LANES, (h + 1) * LANES)
        q = (q_ref[:, hs].astype(F32) * scale).astype(BF16)

        def step(j, carry, bias):
            ks = pl.multiple_of(j * t, t)
            return _softmax_step(q, k_ref[pl.ds(ks, t), hs], v_ref[pl.ds(ks, t), :], carry, bias)

        carry = step(qi, _softmax_init(t), dbias)
        m, l, acc = lax.fori_loop(0, qi, lambda j, cr: step(j, cr, None), carry)
        o = acc / l
        res = o if h == 0 else jnp.where(lane < MLA_V, res, o)
    o_ref[...] = res.astype(BF16)


def _mla_attention(qm, km, vm, b, s):
    t = min(ATTN_TILE, s)
    nq = s // t
    hp = qm.shape[1] // (2 * LANES)
    return pl.pallas_call(
        functools.partial(_mla_kernel, t=t, scale=float((MLA_NOPE + MLA_ROPE) ** -0.5)),
        grid=(b, hp, nq),
        in_specs=[pl.BlockSpec((t, 2 * LANES), lambda bi, h, qi: (bi * nq + qi, h)),
                  pl.BlockSpec((s, 2 * LANES), lambda bi, h, qi: (bi, h)),
                  pl.BlockSpec((s, LANES), lambda bi, h, qi: (bi, h))],
        out_specs=pl.BlockSpec((t, LANES), lambda bi, h, qi: (bi * nq + qi, h)),
        out_shape=jax.ShapeDtypeStruct((b * s, hp * LANES), BF16),
        compiler_params=_cparams("parallel", "parallel", "arbitrary"),
        name="mla_attention",
    )(qm, km, vm)


def _moba_kernel(q_ref, k_ref, v_ref, o_ref, kmean_ref, *, nb):
    j = pl.program_id(2)
    bs = MOBA_BLOCK

    @pl.when(j == 0)
    def _():
        kmean_ref[...] = jnp.zeros_like(kmean_ref)
        for n in range(nb):
            blk = k_ref[n * bs:(n + 1) * bs, :].astype(F32)
            kmean_ref[n:n + 1, :] = jnp.sum(blk, 0, keepdims=True) * (1.0 / bs)

    lane = _lane_iota(bs)
    dbias = _causal_bias(bs)
    qf = q_ref[...].astype(F32) * (HEAD_DIM ** -0.5)
    kmean = kmean_ref[...]
    res = None
    for h in range(2):
        qh = jnp.where((lane >= h * HEAD_DIM) & (lane < (h + 1) * HEAD_DIM), qf, 0.0)
        gate = lax.dot_general(qh, kmean, _NT, preferred_element_type=F32, precision=lax.Precision.HIGHEST)
        g = jnp.where(lane < j, gate, -jnp.inf)
        sel = lane < 0
        for _ in range(MOBA_TOPK):
            mx = jnp.max(g, -1, keepdims=True)
            idx = jnp.min(jnp.where(g == mx, lane, LANES), -1, keepdims=True)
            hit = lane == idx
            sel = sel | (hit & (lane < j))
            g = jnp.where(hit, -jnp.inf, g)
        selb = jnp.where(sel, 0.0, NEG).astype(F32)
        qb = qh.astype(BF16)

        def step(n, carry, bias):
            ks = pl.multiple_of(n * bs, bs)
            return _softmax_step(qb, k_ref[pl.ds(ks, bs), :], v_ref[pl.ds(ks, bs), :], carry, bias)

        def past(n, carry):
            bn = jnp.sum(jnp.where(lane == n, selb, 0.0), -1, keepdims=True)
            return step(n, carry, bn)

        carry = step(j, _softmax_init(bs), dbias)
        m, l, acc = lax.fori_loop(0, j, past, carry)
        o = acc / l
        res = o if h == 0 else jnp.where(lane < HEAD_DIM, res, o)
    o_ref[...] = res.astype(BF16)


def _moba_attention(q, k, v, b, s):
    bs = MOBA_BLOCK
    nb = s // bs
    hp = q.shape[1] // LANES
    return pl.pallas_call(
        functools.partial(_moba_kernel, nb=nb),
        grid=(b, hp, nb),
        in_specs=[pl.BlockSpec((bs, LANES), lambda bi, h, qi: (bi * nb + qi, h)),
                  pl.BlockSpec((s, LANES), lambda bi, h, qi: (bi, h)),
                  pl.BlockSpec((s, LANES), lambda bi, h, qi: (bi, h))],
        out_specs=pl.BlockSpec((bs, LANES), lambda bi, h, qi: (bi * nb + qi, h)),
        out_shape=jax.ShapeDtypeStruct((b * s, hp * LANES), BF16),
        scratch_shapes=[pltpu.VMEM((LANES, LANES), F32)],
        compiler_params=_cparams("parallel", "parallel", "arbitrary"),
        name="moba_attention",
    )(q, k, v)


def _diff_kernel(q_ref, k_ref, v_ref, lam_ref, g_ref, o_ref, *, t, lambda_init):
    qi = pl.program_id(2)
    dbias = _causal_bias(t)
    lane = _lane_iota(t)
    qf = q_ref[...].astype(F32) * (HEAD_DIM ** -0.5)
    q1 = jnp.where(lane < HEAD_DIM, qf, 0.0).astype(BF16)
    q2 = jnp.where(lane >= HEAD_DIM, qf, 0.0).astype(BF16)

    def step(j, carry, bias):
        ks = pl.multiple_of(j * t, t)
        k = k_ref[pl.ds(ks, t), :]
        v = v_ref[pl.ds(ks, t), :]
        return (_softmax_step(q1, k, v, carry[0], bias), _softmax_step(q2, k, v, carry[1], bias))

    carry = step(qi, (_softmax_init(t), _softmax_init(t)), dbias)
    (_, l1, a1), (_, l2, a2) = lax.fori_loop(0, qi, lambda j, cr: step(j, cr, None), carry)
    lp = lam_ref[...]
    lam = (jnp.exp(jnp.sum(lp[0:1] * lp[1:2], -1, keepdims=True))
           - jnp.exp(jnp.sum(lp[2:3] * lp[3:4], -1, keepdims=True)) + lambda_init)
    o = a1 / l1 - lam * (a2 / l2)
    o_ref[...] = (_rms(o, g_ref[...], 1e-5) * (1.0 - lambda_init)).astype(BF16)


def _diff_attention(q, k, v, lam_p, g_subln, lambda_init, b, s):
    t = min(ATTN_TILE, s)
    nq = s // t
    nh = q.shape[1] // LANES
    return pl.pallas_call(
        functools.partial(_diff_kernel, t=t, lambda_init=lambda_init),
        grid=(b, nh, nq),
        in_specs=[pl.BlockSpec((t, LANES), lambda bi, h, qi: (bi * nq + qi, h)),
                  pl.BlockSpec((s, LANES), lambda bi, h, qi: (bi, h)),
                  pl.BlockSpec((s, LANES), lambda bi, h, qi: (bi, h)),
                  pl.BlockSpec(lam_p.shape, lambda bi, h, qi: (0, 0)),
                  pl.BlockSpec(g_subln.shape, lambda bi, h, qi: (0, 0))],
        out_specs=pl.BlockSpec((t, LANES), lambda bi, h, qi: (bi * nq + qi, h)),
        out_shape=jax.ShapeDtypeStruct((b * s, nh * LANES), BF16),
        compiler_params=_cparams("parallel", "parallel", "arbitrary"),
        name="diff_attention",
    )(q, k, v, lam_p, g_subln)


def _swa_kernel(sink_ref, q_ref, k_ref, v_ref, o_ref, *, rows):
    ti = pl.program_id(1)
    w = SWA_WINDOW
    nsub = rows // w
    n_q = q_ref.shape[1] // HEAD_DIM
    n_kv = k_ref.shape[1] // HEAD_DIM
    grp = n_q // n_kv
    lane_k = lax.broadcasted_iota(I32, (2 * w, LANES), 1)
    lane_o = _lane_iota(w)
    qrow = lax.broadcasted_iota(I32, (grp * w, 2 * w), 0) % w
    kcol = lax.broadcasted_iota(I32, (grp * w, 2 * w), 1)
    for i in range(nsub):
        sb = ti * nsub + i
        start = pl.multiple_of(jnp.maximum(sb - 1, 0) * w, w)
        kb = k_ref[pl.ds(start, 2 * w), :]
        vb = v_ref[pl.ds(start, 2 * w), :]
        rel = qrow + (sb * w - start) - kcol
        bias = jnp.where((rel >= 0) & (rel < w), 0.0, NEG).astype(F32)
        qf = q_ref[i * w:(i + 1) * w, :].astype(F32) * (HEAD_DIM ** -0.5)
        head_out = [None] * n_q
        for kvh in range(n_kv):
            km = jnp.where(lane_k // HEAD_DIM == kvh, kb.astype(F32), 0.0).astype(BF16)
            qs, sinks = [], []
            for a in range(grp):
                h = kvh * grp + a
                qp = qf[:, (h // 2) * LANES:(h // 2 + 1) * LANES]
                if h % 2 != kvh:
                    qp = pltpu.roll(qp, HEAD_DIM, 1)
                qs.append(qp.astype(BF16))
                sinks.append(jnp.full((w, 1), sink_ref[h], F32))
            sink = jnp.concatenate(sinks, 0)
            s = lax.dot_general(jnp.concatenate(qs, 0), km, _NT, preferred_element_type=F32) + bias
            m = jnp.maximum(jnp.max(s, -1, keepdims=True), sink)
            p = jnp.exp(s - m)
            den = jnp.sum(p, -1, keepdims=True) + jnp.exp(sink - m)
            o = jnp.dot(p.astype(BF16), vb, preferred_element_type=F32) / den
            for a in range(grp):
                h = kvh * grp + a
                oh = o[a * w:(a + 1) * w, :]
                if h % 2 != kvh:
                    oh = pltpu.roll(oh, HEAD_DIM, 1)
                head_out[h] = oh
        for hp in range(n_q // 2):
            pair = jnp.where(lane_o < HEAD_DIM, head_out[2 * hp], head_out[2 * hp + 1])
            o_ref[i * w:(i + 1) * w, hp * LANES:(hp + 1) * LANES] = pair.astype(BF16)


def _swa_attention(q, k, v, sinks, b, s):
    rows = min(SWA_ROWS, s)
    nt = s // rows
    return pl.pallas_call(
        functools.partial(_swa_kernel, rows=rows),
        grid=(b, nt),
        in_specs=[pl.BlockSpec(memory_space=pltpu.SMEM),
                  pl.BlockSpec((rows, q.shape[1]), lambda bi, ti: (bi * nt + ti, 0)),
                  pl.BlockSpec((s, k.shape[1]), lambda bi, ti: (bi, 0)),
                  pl.BlockSpec((s, v.shape[1]), lambda bi, ti: (bi, 0))],
        out_specs=pl.BlockSpec((rows, q.shape[1]), lambda bi, ti: (bi * nt + ti, 0)),
        out_shape=jax.ShapeDtypeStruct(q.shape, BF16),
        compiler_params=_cparams("parallel", "arbitrary"),
        name="swa_attention",
    )(sinks, q, k, v)


def _layer_norm(y, g, b):
    mu = jnp.mean(y, -1, keepdims=True)
    yc = y - mu
    var = jnp.mean(yc * yc, -1, keepdims=True)
    return yc * lax.rsqrt(var + 1e-5) * g + b


def _out_ln_kernel(x_ref, a_ref, b_ref, w_ref, g_ref, beta_ref, o_ref, *, alpha):
    ka = a_ref.shape[1]
    mix = (jnp.dot(a_ref[...], w_ref[:ka, :], preferred_element_type=F32)
           + jnp.dot(b_ref[...], w_ref[ka:, :], preferred_element_type=F32))
    o_ref[...] = _layer_norm(alpha * x_ref[...] + mix, g_ref[...], beta_ref[...])


def _out_ln(xf, a, b, w, g, beta, alpha):
    n, d = xf.shape
    tm = min(PROJ_ROWS, n)
    row = lambda width: pl.BlockSpec((tm, width), lambda i: (i, 0))
    full = lambda arr: pl.BlockSpec(arr.shape, lambda i: (0, 0))
    return pl.pallas_call(
        functools.partial(_out_ln_kernel, alpha=alpha),
        grid=(n // tm,),
        in_specs=[row(d), row(a.shape[1]), row(b.shape[1]), full(w), full(g), full(beta)],
        out_specs=row(d),
        out_shape=jax.ShapeDtypeStruct((n, d), F32),
        compiler_params=_cparams("parallel"),
        name="out_proj_ln",
    )(xf, a, b, w, g, beta)


def _res_ln_kernel(x_ref, y_ref, g_ref, beta_ref, o_ref, *, alpha):
    o_ref[...] = _layer_norm(alpha * x_ref[...] + y_ref[...], g_ref[...], beta_ref[...])


def _res_ln(xf, y, g, beta, alpha):
    n, d = xf.shape
    tm = min(2 * PROJ_ROWS, n)
    row = pl.BlockSpec((tm, d), lambda i: (i, 0))
    full = lambda arr: pl.BlockSpec(arr.shape, lambda i: (0, 0))
    return pl.pallas_call(
        functools.partial(_res_ln_kernel, alpha=alpha),
        grid=(n // tm,),
        in_specs=[row, row, full(g), full(beta)],
        out_specs=row,
        out_shape=jax.ShapeDtypeStruct((n, d), F32),
        compiler_params=_cparams("parallel"),
        name="residual_ln",
    )(xf, y, g, beta)


def _lane_argmax(v, lane):
    mx = jnp.max(v, -1, keepdims=True)
    idx = jnp.min(jnp.where(v == mx, lane, LANES), -1, keepdims=True)
    return mx, idx


def _router_kernel(x_ref, w_ref, bias_ref, eidx_ref, wts_ref):
    logits = jnp.dot(x_ref[...], w_ref[...], preferred_element_type=F32, precision=lax.Precision.HIGHEST)
    rows = logits.shape[0]
    lane = _lane_iota(rows)
    scores = jax.nn.sigmoid(logits)
    biased = jnp.where(lane < N_EXPERTS, scores + bias_ref[...], -jnp.inf)
    per_group = N_EXPERTS // N_GROUPS
    grp = lane // per_group
    gscore = jnp.full((rows, LANES), -jnp.inf, F32)
    for g in range(N_GROUPS):
        v = jnp.where(grp == g, biased, -jnp.inf)
        m1, i1 = _lane_argmax(v, lane)
        m2 = jnp.max(jnp.where(lane == i1, -jnp.inf, v), -1, keepdims=True)
        gscore = jnp.where(lane == g, m1 + m2, gscore)
    ok = lane < 0
    for _ in range(TOPK_GROUPS):
        _, ig = _lane_argmax(gscore, lane)
        ok = ok | (grp == ig)
        gscore = jnp.where(lane == ig, -jnp.inf, gscore)
    cand = jnp.where(ok & (lane < N_EXPERTS), biased, -jnp.inf)
    eidx = jnp.zeros((rows, LANES), I32)
    w = jnp.zeros((rows, LANES), F32)
    wsum = jnp.zeros((rows, 1), F32)
    for k in range(TOP_K):
        _, ie = _lane_argmax(cand, lane)
        hit = lane == ie
        sc = jnp.sum(jnp.where(hit, scores, 0.0), -1, keepdims=True)
        eidx = jnp.where(lane == k, ie, eidx)
        w = jnp.where(lane == k, sc, w)
        wsum = wsum + sc
        cand = jnp.where(hit, -jnp.inf, cand)
    eidx_ref[...] = eidx
    wts_ref[...] = w / wsum * ROUTED_SCALE


def _router(xf, w_router, e_bias):
    n, d = xf.shape
    tm = min(ROUTER_ROWS, n)
    wpad = jnp.pad(w_router, ((0, 0), (0, LANES - N_EXPERTS)))
    bpad = jnp.pad(e_bias.astype(F32), (0, LANES - N_EXPERTS)).reshape(1, LANES)
    row = lambda width: pl.BlockSpec((tm, width), lambda i: (i, 0))
    eidx, wts = pl.pallas_call(
        _router_kernel,
        grid=(n // tm,),
        in_specs=[row(d), pl.BlockSpec(wpad.shape, lambda i: (0, 0)), pl.BlockSpec(bpad.shape, lambda i: (0, 0))],
        out_specs=[row(LANES), row(LANES)],
        out_shape=[jax.ShapeDtypeStruct((n, LANES), I32), jax.ShapeDtypeStruct((n, LANES), F32)],
        compiler_params=_cparams("parallel"),
        name="router",
    )(xf, wpad, bpad)
    return eidx[:, :TOP_K], wts[:, :TOP_K]


def _dispatch(eidx, wts, chunk, rows):
    n = eidx.shape[0]
    nc = n // chunk
    a = chunk * TOP_K
    nblk = a // rows + N_EXPERTS
    ec = eidx.reshape(nc, a)
    wc = wts.reshape(nc, a)
    ex = jnp.arange(N_EXPERTS, dtype=I32)
    counts = jnp.sum((ec[:, :, None] == ex[None, None, :]).astype(I32), axis=1)
    nb_e = (counts + rows - 1) // rows
    npad = nb_e * rows - counts
    bend = jnp.cumsum(nb_e, axis=1).astype(I32)
    bstart = bend - nb_e
    used = bend[:, -1:]
    barange = jnp.arange(nblk, dtype=I32)[None, :]
    be = jnp.sum((bend[:, None, :] <= barange[:, :, None]).astype(I32), axis=-1)
    live = barange < used
    be = jnp.minimum(be, N_EXPERTS - 1)
    be = jnp.where(live, be, jnp.max(jnp.where(live, be, 0), axis=1, keepdims=True))
    onehot = be[:, :, None] == ex[None, None, :]
    pick = lambda t: jnp.sum(jnp.where(onehot, t[:, None, :], 0), axis=-1)
    within = barange - pick(bstart)
    brows = jnp.where(live, jnp.clip(pick(counts) - within * rows, 0, rows), 0).astype(I32)
    mult = 1 << int(chunk).bit_length()
    fill = jnp.arange(rows, dtype=I32)[None, None, :] < npad[:, :, None]
    fill_key = jnp.where(fill, 2 * ex[None, :, None] + 1, 2 * N_EXPERTS).reshape(nc, N_EXPERTS * rows)
    tok_ids = jnp.broadcast_to(jnp.arange(a, dtype=I32)[None, :] // TOP_K, (nc, a))
    keys = jnp.concatenate([2 * ec * mult + tok_ids, fill_key * mult + chunk], axis=1)
    gates = jnp.concatenate([wc, jnp.zeros((nc, N_EXPERTS * rows), F32)], axis=1)
    keys, gates = lax.sort((keys, gates), dimension=1, num_keys=1)
    tok = keys % mult
    return (be.reshape(-1), brows.reshape(-1), tok.reshape(nc * nblk, 1, rows), gates.reshape(nc * nblk, 1, rows), nblk)


def _swiglu(xb, wg, wu):
    hg = jnp.dot(xb, wg, preferred_element_type=F32)
    hu = jnp.dot(xb, wu, preferred_element_type=F32)
    return hg * jax.nn.sigmoid(hg) * hu


def _moe_kernel(be_ref, br_ref, tok_ref, gate_ref, x_ref, wg_ref, wu_ref, wd_ref, sg_ref, su_ref, sd_ref,
                y_ref, acc_ref, xg_ref, og_ref, *, chunk, rows):
    c = pl.program_id(0)
    b = pl.program_id(1)
    nblk = pl.num_programs(1)
    group = 8

    @pl.when((c == 0) & (b == 0))
    def _():
        xg_ref[...] = jnp.zeros_like(xg_ref)

    @pl.when(b == 0)
    def _():
        sub = min(512, chunk)
        for s0 in range(0, chunk, sub):
            xb = x_ref[s0:s0 + sub, :].astype(BF16)
            h = _swiglu(xb, sg_ref[...], su_ref[...]).astype(BF16)
            acc_ref[s0:s0 + sub, :] = jnp.dot(h, sd_ref[...], preferred_element_type=F32)
        acc_ref[chunk:chunk + group, :] = jnp.zeros((group, acc_ref.shape[1]), F32)

    nrow = br_ref[c * nblk + b]

    @pl.when(nrow > 0)
    def _():
        ngrp = (nrow + group - 1) // group

        def gather(gi, carry):
            base = pl.multiple_of(gi * group, group)
            for i in range(group):
                t = jnp.minimum(tok_ref[0, 0, base + i], chunk - 1)
                xg_ref[pl.ds(base + i, 1), :] = x_ref[pl.ds(t, 1), :]
            return carry

        lax.fori_loop(0, ngrp, gather, 0)
        xb = xg_ref[...].astype(BF16)
        h = _swiglu(xb, wg_ref[0], wu_ref[0])
        eye = lax.broadcasted_iota(I32, (rows, rows), 0) == lax.broadcasted_iota(I32, (rows, rows), 1)
        gcol = jnp.sum(jnp.where(eye, gate_ref[0], 0.0), -1, keepdims=True)
        og_ref[...] = jnp.dot((h * gcol).astype(BF16), wd_ref[0], preferred_element_type=F32)

        def scatter(gi, carry):
            base = pl.multiple_of(gi * group, group)
            toks = [tok_ref[0, 0, base + i] for i in range(group)]
            vals = [acc_ref[pl.ds(toks[i], 1), :] + og_ref[pl.ds(base + i, 1), :] for i in range(group)]
            for i in range(group):
                acc_ref[pl.ds(toks[i], 1), :] = vals[i]
            return carry

        lax.fori_loop(0, ngrp, scatter, 0)

    @pl.when(b == nblk - 1)
    def _():
        y_ref[...] = acc_ref[0:chunk, :]


def _moe(xf, eidx, wts, wg, wu, wd, sg, su, sd):
    n, d = xf.shape
    chunk = min(MOE_CHUNK, n)
    rows = MOE_ROWS
    be, brows, tok, gate, nblk = _dispatch(eidx, wts, chunk, rows)
    nc = n // chunk
    de = wg.shape[2]
    full = lambda arr: pl.BlockSpec(arr.shape, lambda c, b, be_r, br_r: (0,) * arr.ndim)
    grid_spec = pltpu.PrefetchScalarGridSpec(
        num_scalar_prefetch=2,
        grid=(nc, nblk),
        in_specs=[
            pl.BlockSpec((1, 1, rows), lambda c, b, be_r, br_r: (c * nblk + b, 0, 0), memory_space=pltpu.SMEM),
            pl.BlockSpec((1, 1, rows), lambda c, b, be_r, br_r: (c * nblk + b, 0, 0)),
            pl.BlockSpec((chunk, d), lambda c, b, be_r, br_r: (c, 0)),
            pl.BlockSpec((1, d, de), lambda c, b, be_r, br_r: (be_r[c * nblk + b], 0, 0)),
            pl.BlockSpec((1, d, de), lambda c, b, be_r, br_r: (be_r[c * nblk + b], 0, 0)),
            pl.BlockSpec((1, de, d), lambda c, b, be_r, br_r: (be_r[c * nblk + b], 0, 0)),
            full(sg), full(su), full(sd),
        ],
        out_specs=pl.BlockSpec((chunk, d), lambda c, b, be_r, br_r: (c, 0)),
        scratch_shapes=[pltpu.VMEM((chunk + 8, d), F32), pltpu.VMEM((rows, d), F32), pltpu.VMEM((rows, d), F32)],
    )
    return pl.pallas_call(
        functools.partial(_moe_kernel, chunk=chunk, rows=rows),
        grid_spec=grid_spec,
        out_shape=jax.ShapeDtypeStruct((n, d), F32),
        compiler_params=_cparams("arbitrary", "arbitrary"),
        name="moe_experts",
    )(be, brows, tok, gate, xf, wg, wu, wd, sg, su, sd)


def _rope_tables(positions):
    pos = positions.reshape(-1).astype(F32)[:, None]
    n = pos.shape[0]

    def cs(dim):
        inv = ROPE_THETA ** (-jnp.arange(0, dim, 2, dtype=F32) / dim)
        ang = pos * inv
        return jnp.cos(ang), jnp.sin(ang)

    c, s = cs(HEAD_DIM)
    c64 = jnp.concatenate([c, c, c, c], -1)
    s64 = jnp.concatenate([-s, s, -s, s], -1)
    c, s = cs(MLA_ROPE)
    one = jnp.ones((n, MLA_NOPE), F32)
    pad = LANES - MLA_NOPE - MLA_ROPE
    cm = jnp.concatenate([one, c, c, jnp.ones((n, pad), F32)], -1)
    sm = jnp.concatenate([0 * one, -s, s, jnp.zeros((n, pad), F32)], -1)
    return (c64, s64), (cm, sm)


def _even_mixer(xf, b, s, tabs, w_in, g_qa, w_qb, g_kva, w_kvb):
    d = xf.shape[1]
    tab64, tabm = tabs
    na = 3 * 512
    kpe_lo = na + 256 + 128
    wpad = jnp.concatenate([w_in[:, :kpe_lo], jnp.zeros((d, MLA_NOPE), F32), w_in[:, kpe_lo:],
                            jnp.zeros((d, LANES - MLA_NOPE - MLA_ROPE), F32)], 1).astype(BF16)
    nh = w_qb.shape[1] // (MLA_NOPE + MLA_ROPE)
    wqb = jnp.pad(w_qb.reshape(-1, nh, MLA_NOPE + MLA_ROPE), ((0, 0), (0, 0), (0, LANES - MLA_NOPE - MLA_ROPE)))
    wqb = wqb.reshape(-1, nh * LANES).astype(BF16)
    wkv3 = w_kvb.reshape(-1, nh, MLA_NOPE + MLA_V)
    wkk = jnp.pad(wkv3[:, :, :MLA_NOPE], ((0, 0), (0, 0), (0, LANES - MLA_NOPE))).reshape(-1, nh * LANES).astype(BF16)
    wkv = wkv3[:, :, MLA_NOPE:].reshape(-1, nh * MLA_V).astype(BF16)
    segs = [(0, 512, True), (512, 1024, True), (1024, 1536, False)]
    qa, ka, va, qm, km, vm = _projection(
        xf, wpad, segs, tab64,
        mla_args=(tabm[0], tabm[1], g_qa.reshape(1, -1), wqb, g_kva.reshape(1, -1), wkk, wkv),
        mla_out_widths=(nh * LANES, nh * LANES, nh * MLA_V))
    o_a = _moba_attention(qa, ka, va, b, s)
    o_b = _mla_attention(qm, km, vm, b, s)
    return o_a, o_b


def _odd_mixer(xf, b, s, tabs, w_in, sinks, lam_p, g_subln, lambda_init):
    segs = [(0, 512, True), (512, 640, True), (640, 768, False),
            (768, 1280, True), (1280, 1792, True), (1792, 2304, False)]
    qc, kc, vc, qd, kd, vd = _projection(xf, w_in.astype(BF16), segs, tabs[0])
    o_c = _swa_attention(qc, kc, vc, sinks.astype(F32), b, s)
    o_d = _diff_attention(qd, kd, vd, lam_p.astype(F32), g_subln.reshape(1, -1), lambda_init, b, s)
    return o_c, o_d


def kernel(x, positions, ev_w_in, ev_w_out, mla_g_qa, mla_w_qb, mla_g_kva, mla_w_kvb, od_w_in, od_w_out, swa_sinks, diff_lambda, diff_g_subln, ln_g, ln_b, moe_w_router, moe_bias, moe_w_gate, moe_w_up, moe_w_down, sh_w_gate, sh_w_up, sh_w_down):
    b, s, d = x.shape
    depth = ln_g.shape[0]
    alpha = float((2 * depth) ** 0.25)
    xf = x.reshape(b * s, d)
    tabs = _rope_tables(positions)
    for l in range(depth):
        i = l // 2
        if l % 2 == 0:
            o1, o2 = _even_mixer(xf, b, s, tabs, ev_w_in[i], mla_g_qa[i], mla_w_qb[i], mla_g_kva[i], mla_w_kvb[i])
            w_out = ev_w_out[i]
        else:
            lambda_init = 0.8 - 0.6 * float(np.exp(-0.3 * l))
            o1, o2 = _odd_mixer(xf, b, s, tabs, od_w_in[i], swa_sinks[i], diff_lambda[i], diff_g_subln[i], lambda_init)
            w_out = od_w_out[i]
        xf = _out_ln(xf, o1, o2, w_out.astype(BF16), ln_g[l, 0].reshape(1, d), ln_b[l, 0].reshape(1, d), alpha)
        eidx, wts = _router(xf, moe_w_router[l], moe_bias[l])
        y = _moe(xf, eidx, wts, moe_w_gate[l].astype(BF16), moe_w_up[l].astype(BF16), moe_w_down[l].astype(BF16),
                 sh_w_gate[l].astype(BF16), sh_w_up[l].astype(BF16), sh_w_down[l].astype(BF16))
        xf = _res_ln(xf, y, ln_g[l, 1].reshape(1, d), ln_b[l, 1].reshape(1, d), alpha)
    return xf.reshape(b, s, d)
```

```python
import functools

import numpy as np
import jax
import jax.numpy as jnp
from jax import lax
from jax.experimental import pallas as pl
from jax.experimental.pallas import tpu as pltpu

F32 = jnp.float32
BF16 = jnp.bfloat16
I32 = jnp.int32

LANES = 128
NEG = -1e30

HEAD_DIM = 64
MOBA_BLOCK = 256
MOBA_TOPK = 3
MLA_NOPE = 64
MLA_ROPE = 32
MLA_V = 64
SWA_WINDOW = 128
N_EXPERTS = 64
N_GROUPS = 8
TOPK_GROUPS = 4
TOP_K = 8
ROUTED_SCALE = 2.5
ROPE_THETA = 10000.0
QK_SCALE = HEAD_DIM ** -0.5
MLA_SCALE = (MLA_NOPE + MLA_ROPE) ** -0.5

PROJ_ROWS = 512
ATTN_TILE = 1024
SWA_ROWS = 512
ROUTER_ROWS = 512
MOE_CHUNK = 2048
MOE_ROWS = 256
VMEM_LIMIT = 56 * 1024 * 1024

_NT = (((1,), (1,)), ((), ()))


def _cparams(*sem):
    return pltpu.CompilerParams(dimension_semantics=sem, vmem_limit_bytes=VMEM_LIMIT)


def _lane_iota(rows):
    return lax.broadcasted_iota(I32, (rows, LANES), 1)


def _rope(x, c, s, first_half, half):
    rot = jnp.where(first_half, pltpu.roll(x, LANES - half, 1), pltpu.roll(x, half, 1))
    return x * c + rot * s


def _rms(x, g, eps):
    return x * lax.rsqrt(jnp.mean(x * x, -1, keepdims=True) + eps) * g


def _proj_kernel(segs, mla, x_ref, w_ref, c64_ref, s64_ref, *refs):
    xb = x_ref[...].astype(BF16)
    rows = xb.shape[0]
    lane = _lane_iota(rows)
    fh64 = (lane % HEAD_DIM) < HEAD_DIM // 2
    c64 = c64_ref[...]
    s64 = s64_ref[...]

    def seg(lo, hi):
        return jnp.dot(xb, w_ref[:, lo:hi], preferred_element_type=F32)

    if mla:
        cm_ref, sm_ref, gq_ref, wqb_ref, gkv_ref, wkk_ref, wkv_ref = refs[:7]
        refs = refs[7:]
    for (lo, hi, rope, scale), o_ref in zip(segs, refs):
        h = seg(lo, hi)
        if rope:
            for g in range((hi - lo) // LANES):
                sl = slice(g * LANES, (g + 1) * LANES)
                o_ref[:, sl] = (_rope(h[:, sl], c64, s64, fh64, HEAD_DIM // 2) * scale).astype(BF16)
        else:
            o_ref[...] = h.astype(BF16)
    if mla:
        qm_ref, km_ref, vm_ref = refs[len(segs):]
        cm = cm_ref[...]
        sm = sm_ref[...]
        fhm = (lane >= MLA_NOPE) & (lane < MLA_NOPE + MLA_ROPE // 2)
        lo = segs[-1][1]
        qn = _rms(seg(lo, lo + 256), gq_ref[...], 1e-6).astype(BF16)
        qm = jnp.dot(qn, wqb_ref[...], preferred_element_type=F32)
        kvn = _rms(seg(lo + 256, lo + 384), gkv_ref[...], 1e-6).astype(BF16)
        kk = jnp.dot(kvn, wkk_ref[...], preferred_element_type=F32)
        kpe = _rope(seg(lo + 384, lo + 512), cm, sm, fhm, MLA_ROPE // 2)
        for g in range(qm.shape[1] // LANES):
            sl = slice(g * LANES, (g + 1) * LANES)
            qm_ref[:, sl] = (_rope(qm[:, sl], cm, sm, fhm, MLA_ROPE // 2) * MLA_SCALE).astype(BF16)
            km_ref[:, sl] = (kk[:, sl] + kpe).astype(BF16)
        vm_ref[...] = jnp.dot(kvn, wkv_ref[...], preferred_element_type=F32).astype(BF16)


def _projection(xf, w, segs, tabs, mla_args=None, mla_out_widths=()):
    n, d = xf.shape
    tm = min(PROJ_ROWS, n)
    row = lambda width: pl.BlockSpec((tm, width), lambda i: (i, 0))
    full = lambda a: pl.BlockSpec(a.shape, lambda i: (0,) * a.ndim)
    ins = [xf, w, tabs[0], tabs[1]]
    in_specs = [row(d), full(w), row(LANES), row(LANES)]
    if mla_args is not None:
        cm, sm, gq, wqb, gkv, wkk, wkv = mla_args
        ins += [cm, sm, gq, wqb, gkv, wkk, wkv]
        in_specs += [row(LANES), row(LANES), full(gq), full(wqb), full(gkv), full(wkk), full(wkv)]
    widths = [hi - lo for lo, hi, _, _ in segs] + list(mla_out_widths)
    return pl.pallas_call(
        functools.partial(_proj_kernel, tuple(segs), mla_args is not None),
        grid=(n // tm,),
        in_specs=in_specs,
        out_specs=[row(wd) for wd in widths],
        out_shape=[jax.ShapeDtypeStruct((n, wd), BF16) for wd in widths],
        compiler_params=_cparams("parallel"),
        name="projection",
    )(*ins)


def _flash_update(c, q, k, v, bias, state, first):
    m_ref, l_ref, acc_ref = state
    s = lax.dot_general(q, k, _NT, preferred_element_type=F32)
    if bias is not None:
        s = s + bias
    mx = jnp.max(s, -1, keepdims=True)
    if first:
        p = jnp.exp(s - mx)
        l_ref[c] = jnp.sum(p, -1, keepdims=True)
        acc_ref[c] = jnp.dot(p.astype(BF16), v, preferred_element_type=F32)
        m_ref[c] = mx
    else:
        m_prev = m_ref[c]
        m_new = jnp.maximum(m_prev, mx)
        alpha = jnp.exp(m_prev - m_new)
        p = jnp.exp(s - m_new)
        l_ref[c] = alpha * l_ref[c] + jnp.sum(p, -1, keepdims=True)
        acc_ref[c] = alpha * acc_ref[c] + jnp.dot(p.astype(BF16), v, preferred_element_type=F32)
        m_ref[c] = m_new


def _flash_state(chains, t):
    return [pltpu.VMEM((chains, t, 1), F32), pltpu.VMEM((chains, t, 1), F32), pltpu.VMEM((chains, t, LANES), F32)]


def _causal_sweep(qi, step):
    step(qi, True)

    def body(j, carry):
        step(j, False)
        return carry

    lax.fori_loop(0, qi, body, 0)


def _causal_bias(t, block=None):
    r = lax.broadcasted_iota(I32, (t, t), 0)
    c = lax.broadcasted_iota(I32, (t, t), 1)
    hide = c > r
    if block is not None:
        hide = hide & ((r // block) == (c // block))
    return jnp.where(hide, NEG, 0.0).astype(F32)


def _mla_kernel(q_ref, k_ref, v_ref, o_ref, m_ref, l_ref, acc_ref, *, t):
    qi = pl.program_id(2)
    dbias = _causal_bias(t)
    state = (m_ref, l_ref, acc_ref)

    def step(j, first):
        ks = pl.multiple_of(j * t, t)
        v = v_ref[pl.ds(ks, t), :]
        for h in range(2):
            hs = slice(h * LANES, (h + 1) * LANES)
            _flash_update(h, q_ref[:, hs], k_ref[pl.ds(ks, t), hs], v, dbias if first else None, state, first)

    _causal_sweep(qi, step)
    lane = _lane_iota(t)
    o_ref[...] = jnp.where(lane < MLA_V, acc_ref[0] / l_ref[0], acc_ref[1] / l_ref[1]).astype(BF16)


def _mla_attention(qm, km, vm, b, s):
    t = min(ATTN_TILE, s)
    nq = s // t
    hp = qm.shape[1] // (2 * LANES)
    return pl.pallas_call(
        functools.partial(_mla_kernel, t=t),
        grid=(b, hp, nq),
        in_specs=[pl.BlockSpec((t, 2 * LANES), lambda bi, h, qi: (bi * nq + qi, h)),
                  pl.BlockSpec((s, 2 * LANES), lambda bi, h, qi: (bi, h)),
                  pl.BlockSpec((s, LANES), lambda bi, h, qi: (bi, h))],
        out_specs=pl.BlockSpec((t, LANES), lambda bi, h, qi: (bi * nq + qi, h)),
        out_shape=jax.ShapeDtypeStruct((b * s, hp * LANES), BF16),
        scratch_shapes=_flash_state(2, t),
        compiler_params=_cparams("parallel", "parallel", "arbitrary"),
        name="mla_attention",
    )(qm, km, vm)


def _moba_kernel(q_ref, k_ref, v_ref, o_ref, kmean_ref, kaug_ref, qaug_ref, m_ref, l_ref, acc_ref, *, t, nb):
    qi = pl.program_id(2)
    bs = MOBA_BLOCK
    s_len = k_ref.shape[0]
    half = LANES // 2

    @pl.when(qi == 0)
    def _():
        kmean_ref[...] = jnp.zeros_like(kmean_ref)
        for n in range(nb):
            blk = k_ref[n * bs:(n + 1) * bs, :].astype(F32)
            kmean_ref[n:n + 1, :] = jnp.sum(blk, 0, keepdims=True) * (1.0 / bs)
        lane_s = _lane_iota(s_len)
        row_blk = lax.broadcasted_iota(I32, (s_len, LANES), 0) // bs
        kf = k_ref[...].astype(F32)
        kaug_ref[0] = jnp.where(lane_s < half, kf, (lane_s - half == row_blk).astype(F32)).astype(BF16)
        kaug_ref[1] = jnp.where(lane_s >= half, kf, (lane_s == row_blk).astype(F32)).astype(BF16)

    lane = _lane_iota(t)
    own = qi * (t // bs) + lax.broadcasted_iota(I32, (t, LANES), 0) // bs
    qf = q_ref[...].astype(F32)
    kmean = kmean_ref[...]
    for h in range(2):
        head = (lane >= h * half) & (lane < (h + 1) * half)
        qh = jnp.where(head, qf, 0.0)
        gate = lax.dot_general(qh, kmean, _NT, preferred_element_type=F32, precision=lax.Precision.HIGHEST)
        g = jnp.where(lane < own, gate, -jnp.inf)
        sel = lane == own
        for _ in range(MOBA_TOPK):
            mx = jnp.max(g, -1, keepdims=True)
            idx = jnp.min(jnp.where(g == mx, lane, LANES), -1, keepdims=True)
            hit = lane == idx
            sel = sel | (hit & (lane < own))
            g = jnp.where(hit, -jnp.inf, g)
        selb = jnp.where(sel | (lane >= nb), 0.0, NEG).astype(F32)
        if h == 0:
            selb = pltpu.roll(selb, half, 1)
        qaug_ref[h] = jnp.where(head, qf, selb).astype(BF16)

    dbias = _causal_bias(t, bs)
    state = (m_ref, l_ref, acc_ref)

    def step(j, first):
        ks = pl.multiple_of(j * t, t)
        v = v_ref[pl.ds(ks, t), :]
        for h in range(2):
            _flash_update(h, qaug_ref[h], kaug_ref[h, pl.ds(ks, t), :], v, dbias if first else None, state, first)

    _causal_sweep(qi, step)
    o_ref[...] = jnp.where(lane < half, acc_ref[0] / l_ref[0], acc_ref[1] / l_ref[1]).astype(BF16)


def _moba_attention(q, k, v, b, s):
    bs = MOBA_BLOCK
    nb = s // bs
    assert nb <= LANES // 2, "block gate must fit the spare half of the lanes"
    t = min(ATTN_TILE, s)
    nq = s // t
    hp = q.shape[1] // LANES
    return pl.pallas_call(
        functools.partial(_moba_kernel, t=t, nb=nb),
        grid=(b, hp, nq),
        in_specs=[pl.BlockSpec((t, LANES), lambda bi, h, qi: (bi * nq + qi, h)),
                  pl.BlockSpec((s, LANES), lambda bi, h, qi: (bi, h)),
                  pl.BlockSpec((s, LANES), lambda bi, h, qi: (bi, h))],
        out_specs=pl.BlockSpec((t, LANES), lambda bi, h, qi: (bi * nq + qi, h)),
        out_shape=jax.ShapeDtypeStruct((b * s, hp * LANES), BF16),
        scratch_shapes=[pltpu.VMEM((LANES, LANES), F32), pltpu.VMEM((2, s, LANES), BF16),
                        pltpu.VMEM((2, t, LANES), BF16)] + _flash_state(2, t),
        compiler_params=_cparams("parallel", "parallel", "arbitrary"),
        name="moba_attention",
    )(q, k, v)


def _diff_kernel(q_ref, k_ref, v_ref, lam_ref, g_ref, o_ref, qs_ref, m_ref, l_ref, acc_ref, *, t, lambda_init):
    qi = pl.program_id(2)
    lane = _lane_iota(t)
    qf = q_ref[...].astype(F32)
    qs_ref[0] = jnp.where(lane < HEAD_DIM, qf, 0.0).astype(BF16)
    qs_ref[1] = jnp.where(lane >= HEAD_DIM, qf, 0.0).astype(BF16)
    dbias = _causal_bias(t)
    state = (m_ref, l_ref, acc_ref)

    def step(j, first):
        ks = pl.multiple_of(j * t, t)
        k = k_ref[pl.ds(ks, t), :]
        v = v_ref[pl.ds(ks, t), :]
        for c in range(2):
            _flash_update(c, qs_ref[c], k, v, dbias if first else None, state, first)

    _causal_sweep(qi, step)
    lp = lam_ref[...]
    lam = (jnp.exp(jnp.sum(lp[0:1] * lp[1:2], -1, keepdims=True))
           - jnp.exp(jnp.sum(lp[2:3] * lp[3:4], -1, keepdims=True)) + lambda_init)
    o = acc_ref[0] / l_ref[0] - lam * (acc_ref[1] / l_ref[1])
    o_ref[...] = (_rms(o, g_ref[...], 1e-5) * (1.0 - lambda_init)).astype(BF16)


def _diff_attention(q, k, v, lam_p, g_subln, lambda_init, b, s):
    t = min(ATTN_TILE, s)
    nq = s // t
    nh = q.shape[1] // LANES
    return pl.pallas_call(
        functools.partial(_diff_kernel, t=t, lambda_init=lambda_init),
        grid=(b, nh, nq),
        in_specs=[pl.BlockSpec((t, LANES), lambda bi, h, qi: (bi * nq + qi, h)),
                  pl.BlockSpec((s, LANES), lambda bi, h, qi: (bi, h)),
                  pl.BlockSpec((s, LANES), lambda bi, h, qi: (bi, h)),
                  pl.BlockSpec(lam_p.shape, lambda bi, h, qi: (0, 0)),
                  pl.BlockSpec(g_subln.shape, lambda bi, h, qi: (0, 0))],
        out_specs=pl.BlockSpec((t, LANES), lambda bi, h, qi: (bi * nq + qi, h)),
        out_shape=jax.ShapeDtypeStruct((b * s, nh * LANES), BF16),
        scratch_shapes=[pltpu.VMEM((2, t, LANES), BF16)] + _flash_state(2, t),
        compiler_params=_cparams("parallel", "parallel", "arbitrary"),
        name="diff_attention",
    )(q, k, v, lam_p, g_subln)


def _swa_kernel(sink_ref, q_ref, k_ref, v_ref, o_ref, *, rows):
    ti = pl.program_id(1)
    w = SWA_WINDOW
    nsub = rows // w
    n_q = q_ref.shape[1] // HEAD_DIM
    n_kv = k_ref.shape[1] // HEAD_DIM
    grp = n_q // n_kv
    lane_k = lax.broadcasted_iota(I32, (2 * w, LANES), 1)
    lane_o = _lane_iota(w)
    qrow = lax.broadcasted_iota(I32, (grp * w, 2 * w), 0) % w
    kcol = lax.broadcasted_iota(I32, (grp * w, 2 * w), 1)
    for i in range(nsub):
        sb = ti * nsub + i
        start = pl.multiple_of(jnp.maximum(sb - 1, 0) * w, w)
        kb = k_ref[pl.ds(start, 2 * w), :]
        vb = v_ref[pl.ds(start, 2 * w), :]
        rel = qrow + (sb * w - start) - kcol
        bias = jnp.where((rel >= 0) & (rel < w), 0.0, NEG).astype(F32)
        qf = q_ref[i * w:(i + 1) * w, :].astype(F32)
        head_out = [None] * n_q
        for kvh in range(n_kv):
            km = jnp.where(lane_k // HEAD_DIM == kvh, kb.astype(F32), 0.0).astype(BF16)
            qs, sinks = [], []
            for a in range(grp):
                h = kvh * grp + a
                qp = qf[:, (h // 2) * LANES:(h // 2 + 1) * LANES]
                if h % 2 != kvh:
                    qp = pltpu.roll(qp, HEAD_DIM, 1)
                qs.append(qp.astype(BF16))
                sinks.append(jnp.full((w, 1), sink_ref[h], F32))
            sink = jnp.concatenate(sinks, 0)
            s = lax.dot_general(jnp.concatenate(qs, 0), km, _NT, preferred_element_type=F32) + bias
            m = jnp.maximum(jnp.max(s, -1, keepdims=True), sink)
            p = jnp.exp(s - m)
            den = jnp.sum(p, -1, keepdims=True) + jnp.exp(sink - m)
            o = jnp.dot(p.astype(BF16), vb, preferred_element_type=F32) / den
            for a in range(grp):
                h = kvh * grp + a
                oh = o[a * w:(a + 1) * w, :]
                if h % 2 != kvh:
                    oh = pltpu.roll(oh, HEAD_DIM, 1)
                head_out[h] = oh
        for hp in range(n_q // 2):
            pair = jnp.where(lane_o < HEAD_DIM, head_out[2 * hp], head_out[2 * hp + 1])
            o_ref[i * w:(i + 1) * w, hp * LANES:(hp + 1) * LANES] = pair.astype(BF16)


def _swa_attention(q, k, v, sinks, b, s):
    rows = min(SWA_ROWS, s)
    nt = s // rows
    return pl.pallas_call(
        functools.partial(_swa_kernel, rows=rows),
        grid=(b, nt),
        in_specs=[pl.BlockSpec(memory_space=pltpu.SMEM),
                  pl.BlockSpec((rows, q.shape[1]), lambda bi, ti: (bi * nt + ti, 0)),
                  pl.BlockSpec((s, k.shape[1]), lambda bi, ti: (bi, 0)),
                  pl.BlockSpec((s, v.shape[1]), lambda bi, ti: (bi, 0))],
        out_specs=pl.BlockSpec((rows, q.shape[1]), lambda bi, ti: (bi * nt + ti, 0)),
        out_shape=jax.ShapeDtypeStruct(q.shape, BF16),
        compiler_params=_cparams("parallel", "arbitrary"),
        name="swa_attention",
    )(sinks, q, k, v)


def _layer_norm(y, g, b):
    mu = jnp.mean(y, -1, keepdims=True)
    yc = y - mu
    var = jnp.mean(yc * yc, -1, keepdims=True)
    return yc * lax.rsqrt(var + 1e-5) * g + b


def _out_ln_kernel(x_ref, a_ref, b_ref, w_ref, g_ref, beta_ref, o_ref, *, alpha):
    ka = a_ref.shape[1]
    mix = (jnp.dot(a_ref[...], w_ref[:ka, :], preferred_element_type=F32)
           + jnp.dot(b_ref[...], w_ref[ka:, :], preferred_element_type=F32))
    o_ref[...] = _layer_norm(alpha * x_ref[...] + mix, g_ref[...], beta_ref[...])


def _out_ln(xf, a, b, w, g, beta, alpha):
    n, d = xf.shape
    tm = min(PROJ_ROWS, n)
    row = lambda width: pl.BlockSpec((tm, width), lambda i: (i, 0))
    full = lambda arr: pl.BlockSpec(arr.shape, lambda i: (0, 0))
    return pl.pallas_call(
        functools.partial(_out_ln_kernel, alpha=alpha),
        grid=(n // tm,),
        in_specs=[row(d), row(a.shape[1]), row(b.shape[1]), full(w), full(g), full(beta)],
        out_specs=row(d),
        out_shape=jax.ShapeDtypeStruct((n, d), F32),
        compiler_params=_cparams("parallel"),
        name="out_proj_ln",
    )(xf, a, b, w, g, beta)


def _res_ln_kernel(x_ref, y_ref, g_ref, beta_ref, o_ref, *, alpha):
    o_ref[...] = _layer_norm(alpha * x_ref[...] + y_ref[...], g_ref[...], beta_ref[...])


def _res_ln(xf, y, g, beta, alpha):
    n, d = xf.shape
    tm = min(2 * PROJ_ROWS, n)
    row = pl.BlockSpec((tm, d), lambda i: (i, 0))
    full = lambda arr: pl.BlockSpec(arr.shape, lambda i: (0, 0))
    return pl.pallas_call(
        functools.partial(_res_ln_kernel, alpha=alpha),
        grid=(n // tm,),
        in_specs=[row, row, full(g), full(beta)],
        out_specs=row,
        out_shape=jax.ShapeDtypeStruct((n, d), F32),
        compiler_params=_cparams("parallel"),
        name="residual_ln",
    )(xf, y, g, beta)


def _lane_argmax(v, lane):
    mx = jnp.max(v, -1, keepdims=True)
    idx = jnp.min(jnp.where(v == mx, lane, LANES), -1, keepdims=True)
    return mx, idx


def _router_kernel(x_ref, w_ref, bias_ref, eidx_ref, wts_ref):
    logits = jnp.dot(x_ref[...], w_ref[...], preferred_element_type=F32, precision=lax.Precision.HIGHEST)
    rows = logits.shape[0]
    lane = _lane_iota(rows)
    scores = jax.nn.sigmoid(logits)
    biased = jnp.where(lane < N_EXPERTS, scores + bias_ref[...], -jnp.inf)
    per_group = N_EXPERTS // N_GROUPS
    grp = lane // per_group
    gscore = jnp.full((rows, LANES), -jnp.inf, F32)
    for g in range(N_GROUPS):
        v = jnp.where(grp == g, biased, -jnp.inf)
        m1, i1 = _lane_argmax(v, lane)
        m2 = jnp.max(jnp.where(lane == i1, -jnp.inf, v), -1, keepdims=True)
        gscore = jnp.where(lane == g, m1 + m2, gscore)
    ok = lane < 0
    for _ in range(TOPK_GROUPS):
        _, ig = _lane_argmax(gscore, lane)
        ok = ok | (grp == ig)
        gscore = jnp.where(lane == ig, -jnp.inf, gscore)
    cand = jnp.where(ok & (lane < N_EXPERTS), biased, -jnp.inf)
    eidx = jnp.zeros((rows, LANES), I32)
    w = jnp.zeros((rows, LANES), F32)
    wsum = jnp.zeros((rows, 1), F32)
    for k in range(TOP_K):
        _, ie = _lane_argmax(cand, lane)
        hit = lane == ie
        sc = jnp.sum(jnp.where(hit, scores, 0.0), -1, keepdims=True)
        eidx = jnp.where(lane == k, ie, eidx)
        w = jnp.where(lane == k, sc, w)
        wsum = wsum + sc
        cand = jnp.where(hit, -jnp.inf, cand)
    eidx_ref[...] = eidx
    wts_ref[...] = w / wsum * ROUTED_SCALE


def _router(xf, w_router, e_bias):
    n, d = xf.shape
    tm = min(ROUTER_ROWS, n)
    wpad = jnp.pad(w_router, ((0, 0), (0, LANES - N_EXPERTS)))
    bpad = jnp.pad(e_bias.astype(F32), (0, LANES - N_EXPERTS)).reshape(1, LANES)
    row = lambda width: pl.BlockSpec((tm, width), lambda i: (i, 0))
    eidx, wts = pl.pallas_call(
        _router_kernel,
        grid=(n // tm,),
        in_specs=[row(d), pl.BlockSpec(wpad.shape, lambda i: (0, 0)), pl.BlockSpec(bpad.shape, lambda i: (0, 0))],
        out_specs=[row(LANES), row(LANES)],
        out_shape=[jax.ShapeDtypeStruct((n, LANES), I32), jax.ShapeDtypeStruct((n, LANES), F32)],
        compiler_params=_cparams("parallel"),
        name="router",
    )(xf, wpad, bpad)
    return eidx[:, :TOP_K], wts[:, :TOP_K]


def _dispatch(eidx, wts, chunk, rows):
    n = eidx.shape[0]
    nc = n // chunk
    a = chunk * TOP_K
    nblk = a // rows + N_EXPERTS
    ec = eidx.reshape(nc, a)
    wc = wts.reshape(nc, a)
    ex = jnp.arange(N_EXPERTS, dtype=I32)
    counts = jnp.sum((ec[:, :, None] == ex[None, None, :]).astype(I32), axis=1)
    nb_e = (counts + rows - 1) // rows
    npad = nb_e * rows - counts
    bend = jnp.cumsum(nb_e, axis=1).astype(I32)
    bstart = bend - nb_e
    used = bend[:, -1:]
    barange = jnp.arange(nblk, dtype=I32)[None, :]
    be = jnp.sum((bend[:, None, :] <= barange[:, :, None]).astype(I32), axis=-1)
    live = barange < used
    be = jnp.minimum(be, N_EXPERTS - 1)
    be = jnp.where(live, be, jnp.max(jnp.where(live, be, 0), axis=1, keepdims=True))
    onehot = be[:, :, None] == ex[None, None, :]
    pick = lambda t: jnp.sum(jnp.where(onehot, t[:, None, :], 0), axis=-1)
    within = barange - pick(bstart)
    brows = jnp.where(live, jnp.clip(pick(counts) - within * rows, 0, rows), 0).astype(I32)
    mult = 1 << int(chunk).bit_length()
    fill = jnp.arange(rows, dtype=I32)[None, None, :] < npad[:, :, None]
    fill_key = jnp.where(fill, 2 * ex[None, :, None] + 1, 2 * N_EXPERTS).reshape(nc, N_EXPERTS * rows)
    tok_ids = jnp.broadcast_to(jnp.arange(a, dtype=I32)[None, :] // TOP_K, (nc, a))
    keys = jnp.concatenate([2 * ec * mult + tok_ids, fill_key * mult + chunk], axis=1)
    gates = jnp.concatenate([wc, jnp.zeros((nc, N_EXPERTS * rows), F32)], axis=1)
    keys, gates = lax.sort((keys, gates), dimension=1, num_keys=1)
    tok = keys % mult
    return (be.reshape(-1), brows.reshape(-1), tok.reshape(nc * nblk, 1, rows), gates.reshape(nc * nblk, 1, rows), nblk)


def _swiglu(xb, wg, wu):
    hg = jnp.dot(xb, wg, preferred_element_type=F32)
    hu = jnp.dot(xb, wu, preferred_element_type=F32)
    return hg * jax.nn.sigmoid(hg) * hu


def _moe_kernel(be_ref, br_ref, tok_ref, gate_ref, x_ref, wg_ref, wu_ref, wd_ref, sg_ref, su_ref, sd_ref,
                y_ref, acc_ref, xg_ref, og_ref, *, chunk, rows):
    c = pl.program_id(0)
    b = pl.program_id(1)
    nblk = pl.num_programs(1)
    group = 8

    @pl.when((c == 0) & (b == 0))
    def _():
        xg_ref[...] = jnp.zeros_like(xg_ref)

    @pl.when(b == 0)
    def _():
        sub = min(512, chunk)
        for s0 in range(0, chunk, sub):
            xb = x_ref[s0:s0 + sub, :].astype(BF16)
            h = _swiglu(xb, sg_ref[...], su_ref[...]).astype(BF16)
            acc_ref[s0:s0 + sub, :] = jnp.dot(h, sd_ref[...], preferred_element_type=F32)
        acc_ref[chunk:chunk + group, :] = jnp.zeros((group, acc_ref.shape[1]), F32)

    nrow = br_ref[c * nblk + b]

    @pl.when(nrow > 0)
    def _():
        ngrp = (nrow + group - 1) // group

        def gather(gi, carry):
            base = pl.multiple_of(gi * group, group)
            for i in range(group):
                t = jnp.minimum(tok_ref[0, 0, base + i], chunk - 1)
                xg_ref[pl.ds(base + i, 1), :] = x_ref[pl.ds(t, 1), :]
            return carry

        lax.fori_loop(0, ngrp, gather, 0)
        xb = xg_ref[...].astype(BF16)
        h = _swiglu(xb, wg_ref[0], wu_ref[0])
        eye = lax.broadcasted_iota(I32, (rows, rows), 0) == lax.broadcasted_iota(I32, (rows, rows), 1)
        gcol = jnp.sum(jnp.where(eye, gate_ref[0], 0.0), -1, keepdims=True)
        og_ref[...] = jnp.dot((h * gcol).astype(BF16), wd_ref[0], preferred_element_type=F32)

        def scatter(gi, carry):
            base = pl.multiple_of(gi * group, group)
            toks = [tok_ref[0, 0, base + i] for i in range(group)]
            vals = [acc_ref[pl.ds(toks[i], 1), :] + og_ref[pl.ds(base + i, 1), :] for i in range(group)]
            for i in range(group):
                acc_ref[pl.ds(toks[i], 1), :] = vals[i]
            return carry

        lax.fori_loop(0, ngrp, scatter, 0)

    @pl.when(b == nblk - 1)
    def _():
        y_ref[...] = acc_ref[0:chunk, :]


def _moe(xf, eidx, wts, wg, wu, wd, sg, su, sd):
    n, d = xf.shape
    chunk = min(MOE_CHUNK, n)
    rows = MOE_ROWS
    be, brows, tok, gate, nblk = _dispatch(eidx, wts, chunk, rows)
    nc = n // chunk
    de = wg.shape[2]
    full = lambda arr: pl.BlockSpec(arr.shape, lambda c, b, be_r, br_r: (0,) * arr.ndim)
    grid_spec = pltpu.PrefetchScalarGridSpec(
        num_scalar_prefetch=2,
        grid=(nc, nblk),
        in_specs=[
            pl.BlockSpec((1, 1, rows), lambda c, b, be_r, br_r: (c * nblk + b, 0, 0), memory_space=pltpu.SMEM),
            pl.BlockSpec((1, 1, rows), lambda c, b, be_r, br_r: (c * nblk + b, 0, 0)),
            pl.BlockSpec((chunk, d), lambda c, b, be_r, br_r: (c, 0)),
            pl.BlockSpec((1, d, de), lambda c, b, be_r, br_r: (be_r[c * nblk + b], 0, 0)),
            pl.BlockSpec((1, d, de), lambda c, b, be_r, br_r: (be_r[c * nblk + b], 0, 0)),
            pl.BlockSpec((1, de, d), lambda c, b, be_r, br_r: (be_r[c * nblk + b], 0, 0)),
            full(sg), full(su), full(sd),
        ],
        out_specs=pl.BlockSpec((chunk, d), lambda c, b, be_r, br_r: (c, 0)),
        scratch_shapes=[pltpu.VMEM((chunk + 8, d), F32), pltpu.VMEM((rows, d), F32), pltpu.VMEM((rows, d), F32)],
    )
    return pl.pallas_call(
        functools.partial(_moe_kernel, chunk=chunk, rows=rows),
        grid_spec=grid_spec,
        out_shape=jax.ShapeDtypeStruct((n, d), F32),
        compiler_params=_cparams("arbitrary", "arbitrary"),
        name="moe_experts",
    )(be, brows, tok, gate, xf, wg, wu, wd, sg, su, sd)


def _rope_tables(positions):
    pos = positions.reshape(-1).astype(F32)[:, None]
    n = pos.shape[0]

    def cs(dim):
        inv = ROPE_THETA ** (-jnp.arange(0, dim, 2, dtype=F32) / dim)
        ang = pos * inv
        return jnp.cos(ang), jnp.sin(ang)

    c, s = cs(HEAD_DIM)
    c64 = jnp.concatenate([c, c, c, c], -1)
    s64 = jnp.concatenate([-s, s, -s, s], -1)
    c, s = cs(MLA_ROPE)
    one = jnp.ones((n, MLA_NOPE), F32)
    pad = LANES - MLA_NOPE - MLA_ROPE
    cm = jnp.concatenate([one, c, c, jnp.ones((n, pad), F32)], -1)
    sm = jnp.concatenate([0 * one, -s, s, jnp.zeros((n, pad), F32)], -1)
    return (c64, s64), (cm, sm)


def _even_mixer(xf, b, s, tabs, w_in, g_qa, w_qb, g_kva, w_kvb):
    d = xf.shape[1]
    tab64, tabm = tabs
    na = 3 * 512
    kpe_lo = na + 256 + 128
    wpad = jnp.concatenate([w_in[:, :kpe_lo], jnp.zeros((d, MLA_NOPE), F32), w_in[:, kpe_lo:],
                            jnp.zeros((d, LANES - MLA_NOPE - MLA_ROPE), F32)], 1).astype(BF16)
    nh = w_qb.shape[1] // (MLA_NOPE + MLA_ROPE)
    wqb = jnp.pad(w_qb.reshape(-1, nh, MLA_NOPE + MLA_ROPE), ((0, 0), (0, 0), (0, LANES - MLA_NOPE - MLA_ROPE)))
    wqb = wqb.reshape(-1, nh * LANES).astype(BF16)
    wkv3 = w_kvb.reshape(-1, nh, MLA_NOPE + MLA_V)
    wkk = jnp.pad(wkv3[:, :, :MLA_NOPE], ((0, 0), (0, 0), (0, LANES - MLA_NOPE))).reshape(-1, nh * LANES).astype(BF16)
    wkv = wkv3[:, :, MLA_NOPE:].reshape(-1, nh * MLA_V).astype(BF16)
    segs = [(0, 512, True, QK_SCALE), (512, 1024, True, 1.0), (1024, 1536, False, 1.0)]
    qa, ka, va, qm, km, vm = _projection(
        xf, wpad, segs, tab64,
        mla_args=(tabm[0], tabm[1], g_qa.reshape(1, -1), wqb, g_kva.reshape(1, -1), wkk, wkv),
        mla_out_widths=(nh * LANES, nh * LANES, nh * MLA_V))
    o_a = _moba_attention(qa, ka, va, b, s)
    o_b = _mla_attention(qm, km, vm, b, s)
    return o_a, o_b


def _odd_mixer(xf, b, s, tabs, w_in, sinks, lam_p, g_subln, lambda_init):
    segs = [(0, 512, True, QK_SCALE), (512, 640, True, 1.0), (640, 768, False, 1.0),
            (768, 1280, True, QK_SCALE), (1280, 1792, True, 1.0), (1792, 2304, False, 1.0)]
    qc, kc, vc, qd, kd, vd = _projection(xf, w_in.astype(BF16), segs, tabs[0])
    o_c = _swa_attention(qc, kc, vc, sinks.astype(F32), b, s)
    o_d = _diff_attention(qd, kd, vd, lam_p.astype(F32), g_subln.reshape(1, -1), lambda_init, b, s)
    return o_c, o_d


def kernel(x, positions, ev_w_in, ev_w_out, mla_g_qa, mla_w_qb, mla_g_kva, mla_w_kvb, od_w_in, od_w_out, swa_sinks, diff_lambda, diff_g_subln, ln_g, ln_b, moe_w_router, moe_bias, moe_w_gate, moe_w_up, moe_w_down, sh_w_gate, sh_w_up, sh_w_down):
    b, s, d = x.shape
    depth = ln_g.shape[0]
    alpha = float((2 * depth) ** 0.25)
    xf = x.reshape(b * s, d)
    tabs = _rope_tables(positions)
    for l in range(depth):
        i = l // 2
        if l % 2 == 0:
            o1, o2 = _even_mixer(xf, b, s, tabs, ev_w_in[i], mla_g_qa[i], mla_w_qb[i], mla_g_kva[i], mla_w_kvb[i])
            w_out = ev_w_out[i]
        else:
            lambda_init = 0.8 - 0.6 * float(np.exp(-0.3 * l))
            o1, o2 = _odd_mixer(xf, b, s, tabs, od_w_in[i], swa_sinks[i], diff_lambda[i], diff_g_subln[i], lambda_init)
            w_out = od_w_out[i]
        xf = _out_ln(xf, o1, o2, w_out.astype(BF16), ln_g[l, 0].reshape(1, d), ln_b[l, 0].reshape(1, d), alpha)
        eidx, wts = _router(xf, moe_w_router[l], moe_bias[l])
        y = _moe(xf, eidx, wts, moe_w_gate[l].astype(BF16), moe_w_up[l].astype(BF16), moe_w_down[l].astype(BF16),
                 sh_w_gate[l].astype(BF16), sh_w_up[l].astype(BF16), sh_w_down[l].astype(BF16))
        xf = _res_ln(xf, y, ln_g[l, 1].reshape(1, d), ln_b[l, 1].reshape(1, d), alpha)
    return xf.reshape(b, s, d)
```

```python
import functools

import numpy as np
import jax
import jax.numpy as jnp
from jax import lax
from jax.experimental import pallas as pl
from jax.experimental.pallas import tpu as pltpu

F32 = jnp.float32
BF16 = jnp.bfloat16
I32 = jnp.int32

LANES = 128
NEG = -1e30

HEAD_DIM = 64
MOBA_BLOCK = 256
MOBA_TOPK = 3
MLA_NOPE = 64
MLA_ROPE = 32
MLA_V = 64
SWA_WINDOW = 128
N_EXPERTS = 64
N_GROUPS = 8
TOPK_GROUPS = 4
TOP_K = 8
ROUTED_SCALE = 2.5
ROPE_THETA = 10000.0
QK_SCALE = HEAD_DIM ** -0.5
MLA_SCALE = (MLA_NOPE + MLA_ROPE) ** -0.5

PROJ_ROWS = 512
ATTN_TILE = 1024
SWA_ROWS = 512
ROUTER_ROWS = 512
MOE_CHUNK = 2048
MOE_ROWS = 256
VMEM_LIMIT = 56 * 1024 * 1024

_NT = (((1,), (1,)), ((), ()))


def _cparams(*sem):
    return pltpu.CompilerParams(dimension_semantics=sem, vmem_limit_bytes=VMEM_LIMIT)


def _lane_iota(rows):
    return lax.broadcasted_iota(I32, (rows, LANES), 1)


def _rope(x, c, s, first_half, half):
    rot = jnp.where(first_half, pltpu.roll(x, LANES - half, 1), pltpu.roll(x, half, 1))
    return x * c + rot * s


def _rms(x, g, eps):
    return x * lax.rsqrt(jnp.mean(x * x, -1, keepdims=True) + eps) * g


def _proj_kernel(segs, mla, x_ref, w_ref, c64_ref, s64_ref, *refs):
    xb = x_ref[...].astype(BF16)
    rows = xb.shape[0]
    lane = _lane_iota(rows)
    fh64 = (lane % HEAD_DIM) < HEAD_DIM // 2
    c64 = c64_ref[...]
    s64 = s64_ref[...]

    def seg(lo, hi):
        return jnp.dot(xb, w_ref[:, lo:hi], preferred_element_type=F32)

    if mla:
        cm_ref, sm_ref, gq_ref, wqb_ref, gkv_ref, wkk_ref, wkv_ref = refs[:7]
        refs = refs[7:]
    for (lo, hi, rope, scale), o_ref in zip(segs, refs):
        h = seg(lo, hi)
        if rope:
            for g in range((hi - lo) // LANES):
                sl = slice(g * LANES, (g + 1) * LANES)
                o_ref[:, sl] = (_rope(h[:, sl], c64, s64, fh64, HEAD_DIM // 2) * scale).astype(BF16)
        else:
            o_ref[...] = h.astype(BF16)
    if mla:
        qm_ref, km_ref, vm_ref = refs[len(segs):]
        cm = cm_ref[...]
        sm = sm_ref[...]
        fhm = (lane >= MLA_NOPE) & (lane < MLA_NOPE + MLA_ROPE // 2)
        lo = segs[-1][1]
        qn = _rms(seg(lo, lo + 256), gq_ref[...], 1e-6).astype(BF16)
        qm = jnp.dot(qn, wqb_ref[...], preferred_element_type=F32)
        kvn = _rms(seg(lo + 256, lo + 384), gkv_ref[...], 1e-6).astype(BF16)
        kk = jnp.dot(kvn, wkk_ref[...], preferred_element_type=F32)
        kpe = _rope(seg(lo + 384, lo + 512), cm, sm, fhm, MLA_ROPE // 2)
        for g in range(qm.shape[1] // LANES):
            sl = slice(g * LANES, (g + 1) * LANES)
            qm_ref[:, sl] = (_rope(qm[:, sl], cm, sm, fhm, MLA_ROPE // 2) * MLA_SCALE).astype(BF16)
            km_ref[:, sl] = (kk[:, sl] + kpe).astype(BF16)
        vm_ref[...] = jnp.dot(kvn, wkv_ref[...], preferred_element_type=F32).astype(BF16)


def _projection(xf, w, segs, tabs, mla_args=None, mla_out_widths=()):
    n, d = xf.shape
    tm = min(PROJ_ROWS, n)
    row = lambda width: pl.BlockSpec((tm, width), lambda i: (i, 0))
    full = lambda a: pl.BlockSpec(a.shape, lambda i: (0,) * a.ndim)
    ins = [xf, w, tabs[0], tabs[1]]
    in_specs = [row(d), full(w), row(LANES), row(LANES)]
    if mla_args is not None:
        cm, sm, gq, wqb, gkv, wkk, wkv = mla_args
        ins += [cm, sm, gq, wqb, gkv, wkk, wkv]
        in_specs += [row(LANES), row(LANES), full(gq), full(wqb), full(gkv), full(wkk), full(wkv)]
    widths = [hi - lo for lo, hi, _, _ in segs] + list(mla_out_widths)
    return pl.pallas_call(
        functools.partial(_proj_kernel, tuple(segs), mla_args is not None),
        grid=(n // tm,),
        in_specs=in_specs,
        out_specs=[row(wd) for wd in widths],
        out_shape=[jax.ShapeDtypeStruct((n, wd), BF16) for wd in widths],
        compiler_params=_cparams("parallel"),
        name="projection",
    )(*ins)


def _flash_update(c, q, k, v, bias, state, first):
    m_ref, l_ref, acc_ref = state
    s = lax.dot_general(q, k, _NT, preferred_element_type=F32)
    if bias is not None:
        s = s + bias
    mx = jnp.max(s, -1, keepdims=True)
    if first:
        p = jnp.exp(s - mx)
        l_ref[c] = jnp.sum(p, -1, keepdims=True)
        acc_ref[c] = jnp.dot(p.astype(BF16), v, preferred_element_type=F32)
        m_ref[c] = mx
    else:
        m_prev = m_ref[c]
        m_new = jnp.maximum(m_prev, mx)
        alpha = jnp.exp(m_prev - m_new)
        p = jnp.exp(s - m_new)
        l_ref[c] = alpha * l_ref[c] + jnp.sum(p, -1, keepdims=True)
        acc_ref[c] = alpha * acc_ref[c] + jnp.dot(p.astype(BF16), v, preferred_element_type=F32)
        m_ref[c] = m_new


def _flash_state(chains, t):
    return [pltpu.VMEM((chains, t, 1), F32), pltpu.VMEM((chains, t, 1), F32), pltpu.VMEM((chains, t, LANES), F32)]


def _causal_sweep(qi, step):
    step(qi, True)

    def body(j, carry):
        step(j, False)
        return carry

    lax.fori_loop(0, qi, body, 0)


def _causal_bias(t, block=None):
    r = lax.broadcasted_iota(I32, (t, t), 0)
    c = lax.broadcasted_iota(I32, (t, t), 1)
    hide = c > r
    if block is not None:
        hide = hide & ((r // block) == (c // block))
    return jnp.where(hide, NEG, 0.0).astype(F32)


def _mla_kernel(q_ref, k_ref, v_ref, o_ref, m_ref, l_ref, acc_ref, *, t):
    qi = pl.program_id(2)
    dbias = _causal_bias(t)
    state = (m_ref, l_ref, acc_ref)

    def step(j, first):
        ks = pl.multiple_of(j * t, t)
        v = v_ref[pl.ds(ks, t), :]
        for h in range(2):
            hs = slice(h * LANES, (h + 1) * LANES)
            _flash_update(h, q_ref[:, hs], k_ref[pl.ds(ks, t), hs], v, dbias if first else None, state, first)

    _causal_sweep(qi, step)
    lane = _lane_iota(t)
    o_ref[...] = jnp.where(lane < MLA_V, acc_ref[0] / l_ref[0], acc_ref[1] / l_ref[1]).astype(BF16)


def _mla_attention(qm, km, vm, b, s):
    t = min(ATTN_TILE, s)
    nq = s // t
    hp = qm.shape[1] // (2 * LANES)
    return pl.pallas_call(
        functools.partial(_mla_kernel, t=t),
        grid=(b, hp, nq),
        in_specs=[pl.BlockSpec((t, 2 * LANES), lambda bi, h, qi: (bi * nq + qi, h)),
                  pl.BlockSpec((s, 2 * LANES), lambda bi, h, qi: (bi, h)),
                  pl.BlockSpec((s, LANES), lambda bi, h, qi: (bi, h))],
        out_specs=pl.BlockSpec((t, LANES), lambda bi, h, qi: (bi * nq + qi, h)),
        out_shape=jax.ShapeDtypeStruct((b * s, hp * LANES), BF16),
        scratch_shapes=_flash_state(2, t),
        compiler_params=_cparams("parallel", "parallel", "arbitrary"),
        name="mla_attention",
    )(qm, km, vm)


def _moba_kernel(q_ref, k_ref, v_ref, o_ref, kmean_ref, kaug_ref, qaug_ref, m_ref, l_ref, acc_ref, *, t, nb):
    qi = pl.program_id(2)
    bs = MOBA_BLOCK
    s_len = k_ref.shape[0]
    half = LANES // 2

    @pl.when(qi == 0)
    def _():
        kmean_ref[...] = jnp.zeros_like(kmean_ref)
        for n in range(nb):
            blk = k_ref[n * bs:(n + 1) * bs, :].astype(F32)
            kmean_ref[n:n + 1, :] = jnp.sum(blk, 0, keepdims=True) * (1.0 / bs)
        lane_s = _lane_iota(s_len)
        row_blk = lax.broadcasted_iota(I32, (s_len, LANES), 0) // bs
        kf = k_ref[...].astype(F32)
        kaug_ref[0] = jnp.where(lane_s < half, kf, (lane_s - half == row_blk).astype(F32)).astype(BF16)
        kaug_ref[1] = jnp.where(lane_s >= half, kf, (lane_s == row_blk).astype(F32)).astype(BF16)

    lane = _lane_iota(t)
    own = qi * (t // bs) + lax.broadcasted_iota(I32, (t, LANES), 0) // bs
    qf = q_ref[...].astype(F32)
    kmean = kmean_ref[...]
    for h in range(2):
        head = (lane >= h * half) & (lane < (h + 1) * half)
        qh = jnp.where(head, qf, 0.0)
        gate = lax.dot_general(qh, kmean, _NT, preferred_element_type=F32, precision=lax.Precision.HIGHEST)
        g = jnp.where(lane < own, gate, -jnp.inf)
        sel = lane == own
        for _ in range(MOBA_TOPK):
            mx = jnp.max(g, -1, keepdims=True)
            idx = jnp.min(jnp.where(g == mx, lane, LANES), -1, keepdims=True)
            hit = lane == idx
            sel = sel | (hit & (lane < own))
            g = jnp.where(hit, -jnp.inf, g)
        selb = jnp.where(sel | (lane >= nb), 0.0, NEG).astype(F32)
        if h == 0:
            selb = pltpu.roll(selb, half, 1)
        qaug_ref[h] = jnp.where(head, qf, selb).astype(BF16)

    dbias = _causal_bias(t, bs)
    state = (m_ref, l_ref, acc_ref)

    def step(j, first):
        ks = pl.multiple_of(j * t, t)
        v = v_ref[pl.ds(ks, t), :]
        for h in range(2):
            _flash_update(h, qaug_ref[h], kaug_ref[h, pl.ds(ks, t), :], v, dbias if first else None, state, first)

    _causal_sweep(qi, step)
    o_ref[...] = jnp.where(lane < half, acc_ref[0] / l_ref[0], acc_ref[1] / l_ref[1]).astype(BF16)


def _moba_attention(q, k, v, b, s):
    bs = MOBA_BLOCK
    nb = s // bs
    assert nb <= LANES // 2, "block gate must fit the spare half of the lanes"
    t = min(ATTN_TILE, s)
    nq = s // t
    hp = q.shape[1] // LANES
    return pl.pallas_call(
        functools.partial(_moba_kernel, t=t, nb=nb),
        grid=(b, hp, nq),
        in_specs=[pl.BlockSpec((t, LANES), lambda bi, h, qi: (bi * nq + qi, h)),
                  pl.BlockSpec((s, LANES), lambda bi, h, qi: (bi, h)),
                  pl.BlockSpec((s, LANES), lambda bi, h, qi: (bi, h))],
        out_specs=pl.BlockSpec((t, LANES), lambda bi, h, qi: (bi * nq + qi, h)),
        out_shape=jax.ShapeDtypeStruct((b * s, hp * LANES), BF16),
        scratch_shapes=[pltpu.VMEM((LANES, LANES), F32), pltpu.VMEM((2, s, LANES), BF16),
                        pltpu.VMEM((2, t, LANES), BF16)] + _flash_state(2, t),
        compiler_params=_cparams("parallel", "parallel", "arbitrary"),
        name="moba_attention",
    )(q, k, v)


def _diff_kernel(q_ref, k_ref, v_ref, lam_ref, g_ref, o_ref, qs_ref, m_ref, l_ref, acc_ref, *, t, lambda_init):
    qi = pl.program_id(2)
    lane = _lane_iota(t)
    qf = q_ref[...].astype(F32)
    qs_ref[0] = jnp.where(lane < HEAD_DIM, qf, 0.0).astype(BF16)
    qs_ref[1] = jnp.where(lane >= HEAD_DIM, qf, 0.0).astype(BF16)
    dbias = _causal_bias(t)
    state = (m_ref, l_ref, acc_ref)

    def step(j, first):
        ks = pl.multiple_of(j * t, t)
        k = k_ref[pl.ds(ks, t), :]
        v = v_ref[pl.ds(ks, t), :]
        for c in range(2):
            _flash_update(c, qs_ref[c], k, v, dbias if first else None, state, first)

    _causal_sweep(qi, step)
    lp = lam_ref[...]
    lam = (jnp.exp(jnp.sum(lp[0:1] * lp[1:2], -1, keepdims=True))
           - jnp.exp(jnp.sum(lp[2:3] * lp[3:4], -1, keepdims=True)) + lambda_init)
    o = acc_ref[0] / l_ref[0] - lam * (acc_ref[1] / l_ref[1])
    o_ref[...] = (_rms(o, g_ref[...], 1e-5) * (1.0 - lambda_init)).astype(BF16)


def _diff_attention(q, k, v, lam_p, g_subln, lambda_init, b, s):
    t = min(ATTN_TILE, s)
    nq = s // t
    nh = q.shape[1] // LANES
    return pl.pallas_call(
        functools.partial(_diff_kernel, t=t, lambda_init=lambda_init),
        grid=(b, nh, nq),
        in_specs=[pl.BlockSpec((t, LANES), lambda bi, h, qi: (bi * nq + qi, h)),
                  pl.BlockSpec((s, LANES), lambda bi, h, qi: (bi, h)),
                  pl.BlockSpec((s, LANES), lambda bi, h, qi: (bi, h)),
                  pl.BlockSpec(lam_p.shape, lambda bi, h, qi: (0, 0)),
                  pl.BlockSpec(g_subln.shape, lambda bi, h, qi: (0, 0))],
        out_specs=pl.BlockSpec((t, LANES), lambda bi, h, qi: (bi * nq + qi, h)),
        out_shape=jax.ShapeDtypeStruct((b * s, nh * LANES), BF16),
        scratch_shapes=[pltpu.VMEM((2, t, LANES), BF16)] + _flash_state(2, t),
        compiler_params=_cparams("parallel", "parallel", "arbitrary"),
        name="diff_attention",
    )(q, k, v, lam_p, g_subln)


def _swa_kernel(sink_ref, q_ref, k_ref, v_ref, o_ref, *, rows):
    ti = pl.program_id(1)
    w = SWA_WINDOW
    nsub = rows // w
    n_q = q_ref.shape[1] // HEAD_DIM
    n_kv = k_ref.shape[1] // HEAD_DIM
    grp = n_q // n_kv
    lane_k = lax.broadcasted_iota(I32, (2 * w, LANES), 1)
    lane_o = _lane_iota(w)
    qrow = lax.broadcasted_iota(I32, (grp * w, 2 * w), 0) % w
    kcol = lax.broadcasted_iota(I32, (grp * w, 2 * w), 1)
    for i in range(nsub):
        sb = ti * nsub + i
        start = pl.multiple_of(jnp.maximum(sb - 1, 0) * w, w)
        kb = k_ref[pl.ds(start, 2 * w), :]
        vb = v_ref[pl.ds(start, 2 * w), :]
        rel = qrow + (sb * w - start) - kcol
        bias = jnp.where((rel >= 0) & (rel < w), 0.0, NEG).astype(F32)
        qf = q_ref[i * w:(i + 1) * w, :].astype(F32)
        head_out = [None] * n_q
        for kvh in range(n_kv):
            km = jnp.where(lane_k // HEAD_DIM == kvh, kb.astype(F32), 0.0).astype(BF16)
            qs, sinks = [], []
            for a in range(grp):
                h = kvh * grp + a
                qp = qf[:, (h // 2) * LANES:(h // 2 + 1) * LANES]
                if h % 2 != kvh:
                    qp = pltpu.roll(qp, HEAD_DIM, 1)
                qs.append(qp.astype(BF16))
                sinks.append(jnp.full((w, 1), sink_ref[h], F32))
            sink = jnp.concatenate(sinks, 0)
            s = lax.dot_general(jnp.concatenate(qs, 0), km, _NT, preferred_element_type=F32) + bias
            m = jnp.maximum(jnp.max(s, -1, keepdims=True), sink)
            p = jnp.exp(s - m)
            den = jnp.sum(p, -1, keepdims=True) + jnp.exp(sink - m)
            o = jnp.dot(p.astype(BF16), vb, preferred_element_type=F32) / den
            for a in range(grp):
                h = kvh * grp + a
                oh = o[a * w:(a + 1) * w, :]
                if h % 2 != kvh:
                    oh = pltpu.roll(oh, HEAD_DIM, 1)
                head_out[h] = oh
        for hp in range(n_q // 2):
            pair = jnp.where(lane_o < HEAD_DIM, head_out[2 * hp], head_out[2 * hp + 1])
            o_ref[i * w:(i + 1) * w, hp * LANES:(hp + 1) * LANES] = pair.astype(BF16)


def _swa_attention(q, k, v, sinks, b, s):
    rows = min(SWA_ROWS, s)
    nt = s // rows
    return pl.pallas_call(
        functools.partial(_swa_kernel, rows=rows),
        grid=(b, nt),
        in_specs=[pl.BlockSpec(memory_space=pltpu.SMEM),
                  pl.BlockSpec((rows, q.shape[1]), lambda bi, ti: (bi * nt + ti, 0)),
                  pl.BlockSpec((s, k.shape[1]), lambda bi, ti: (bi, 0)),
                  pl.BlockSpec((s, v.shape[1]), lambda bi, ti: (bi, 0))],
        out_specs=pl.BlockSpec((rows, q.shape[1]), lambda bi, ti: (bi * nt + ti, 0)),
        out_shape=jax.ShapeDtypeStruct(q.shape, BF16),
        compiler_params=_cparams("parallel", "arbitrary"),
        name="swa_attention",
    )(sinks, q, k, v)


def _layer_norm(y, g, b):
    mu = jnp.mean(y, -1, keepdims=True)
    yc = y - mu
    var = jnp.mean(yc * yc, -1, keepdims=True)
    return yc * lax.rsqrt(var + 1e-5) * g + b


PIECES = 8


def _tm_load(ref, r0, nrows):
    return jnp.concatenate([ref[pl.ds(r0 * PIECES + p, nrows, stride=PIECES), :] for p in range(PIECES)], axis=1)


def _tm_store(ref, r0, val):
    for p in range(PIECES):
        ref[pl.ds(r0 * PIECES + p, val.shape[0], stride=PIECES), :] = val[:, p * LANES:(p + 1) * LANES]


def _out_ln_kernel(x_ref, a_ref, b_ref, w_ref, g_ref, beta_ref, o_ref, otm_ref, *, alpha):
    ka = a_ref.shape[1]
    mix = (jnp.dot(a_ref[...], w_ref[:ka, :], preferred_element_type=F32)
           + jnp.dot(b_ref[...], w_ref[ka:, :], preferred_element_type=F32))
    y = _layer_norm(alpha * x_ref[...] + mix, g_ref[...], beta_ref[...])
    o_ref[...] = y
    _tm_store(otm_ref, 0, y)


def _out_ln(xf, a, b, w, g, beta, alpha):
    n, d = xf.shape
    assert d == PIECES * LANES
    tm = min(PROJ_ROWS, n)
    row = lambda width: pl.BlockSpec((tm, width), lambda i: (i, 0))
    full = lambda arr: pl.BlockSpec(arr.shape, lambda i: (0, 0))
    return pl.pallas_call(
        functools.partial(_out_ln_kernel, alpha=alpha),
        grid=(n // tm,),
        in_specs=[row(d), row(a.shape[1]), row(b.shape[1]), full(w), full(g), full(beta)],
        out_specs=[row(d), pl.BlockSpec((tm * PIECES, LANES), lambda i: (i, 0))],
        out_shape=[jax.ShapeDtypeStruct((n, d), F32), jax.ShapeDtypeStruct((n * PIECES, LANES), F32)],
        compiler_params=_cparams("parallel"),
        name="out_proj_ln",
    )(xf, a, b, w, g, beta)


def _lane_argmax(v, lane):
    mx = jnp.max(v, -1, keepdims=True)
    idx = jnp.min(jnp.where(v == mx, lane, LANES), -1, keepdims=True)
    return mx, idx


def _router_kernel(x_ref, w_ref, bias_ref, eidx_ref, wts_ref):
    logits = jnp.dot(x_ref[...], w_ref[...], preferred_element_type=F32, precision=lax.Precision.HIGHEST)
    rows = logits.shape[0]
    lane = _lane_iota(rows)
    scores = jax.nn.sigmoid(logits)
    biased = jnp.where(lane < N_EXPERTS, scores + bias_ref[...], -jnp.inf)
    per_group = N_EXPERTS // N_GROUPS
    grp = lane // per_group
    gscore = jnp.full((rows, LANES), -jnp.inf, F32)
    for g in range(N_GROUPS):
        v = jnp.where(grp == g, biased, -jnp.inf)
        m1, i1 = _lane_argmax(v, lane)
        m2 = jnp.max(jnp.where(lane == i1, -jnp.inf, v), -1, keepdims=True)
        gscore = jnp.where(lane == g, m1 + m2, gscore)
    ok = lane < 0
    for _ in range(TOPK_GROUPS):
        _, ig = _lane_argmax(gscore, lane)
        ok = ok | (grp == ig)
        gscore = jnp.where(lane == ig, -jnp.inf, gscore)
    cand = jnp.where(ok & (lane < N_EXPERTS), biased, -jnp.inf)
    eidx = jnp.zeros((rows, LANES), I32)
    w = jnp.zeros((rows, LANES), F32)
    wsum = jnp.zeros((rows, 1), F32)
    for k in range(TOP_K):
        _, ie = _lane_argmax(cand, lane)
        hit = lane == ie
        sc = jnp.sum(jnp.where(hit, scores, 0.0), -1, keepdims=True)
        eidx = jnp.where(lane == k, ie, eidx)
        w = jnp.where(lane == k, sc, w)
        wsum = wsum + sc
        cand = jnp.where(hit, -jnp.inf, cand)
    eidx_ref[...] = eidx
    wts_ref[...] = w / wsum * ROUTED_SCALE


def _router(xf, w_router, e_bias):
    n, d = xf.shape
    tm = min(ROUTER_ROWS, n)
    wpad = jnp.pad(w_router, ((0, 0), (0, LANES - N_EXPERTS)))
    bpad = jnp.pad(e_bias.astype(F32), (0, LANES - N_EXPERTS)).reshape(1, LANES)
    row = lambda width: pl.BlockSpec((tm, width), lambda i: (i, 0))
    eidx, wts = pl.pallas_call(
        _router_kernel,
        grid=(n // tm,),
        in_specs=[row(d), pl.BlockSpec(wpad.shape, lambda i: (0, 0)), pl.BlockSpec(bpad.shape, lambda i: (0, 0))],
        out_specs=[row(LANES), row(LANES)],
        out_shape=[jax.ShapeDtypeStruct((n, LANES), I32), jax.ShapeDtypeStruct((n, LANES), F32)],
        compiler_params=_cparams("parallel"),
        name="router",
    )(xf, wpad, bpad)
    return eidx[:, :TOP_K], wts[:, :TOP_K]


def _dispatch(eidx, wts, chunk, rows):
    n = eidx.shape[0]
    nc = n // chunk
    a = chunk * TOP_K
    nblk = a // rows + N_EXPERTS
    ec = eidx.reshape(nc, a)
    wc = wts.reshape(nc, a)
    ex = jnp.arange(N_EXPERTS, dtype=I32)
    counts = jnp.sum((ec[:, :, None] == ex[None, None, :]).astype(I32), axis=1)
    nb_e = (counts + rows - 1) // rows
    npad = nb_e * rows - counts
    bend = jnp.cumsum(nb_e, axis=1).astype(I32)
    bstart = bend - nb_e
    used = bend[:, -1:]
    barange = jnp.arange(nblk, dtype=I32)[None, :]
    be = jnp.sum((bend[:, None, :] <= barange[:, :, None]).astype(I32), axis=-1)
    live = barange < used
    be = jnp.minimum(be, N_EXPERTS - 1)
    be = jnp.where(live, be, jnp.max(jnp.where(live, be, 0), axis=1, keepdims=True))
    onehot = be[:, :, None] == ex[None, None, :]
    pick = lambda t: jnp.sum(jnp.where(onehot, t[:, None, :], 0), axis=-1)
    within = barange - pick(bstart)
    brows = jnp.where(live, jnp.clip(pick(counts) - within * rows, 0, rows), 0).astype(I32)
    mult = 1 << int(chunk).bit_length()
    fill = jnp.arange(rows, dtype=I32)[None, None, :] < npad[:, :, None]
    fill_key = jnp.where(fill, 2 * ex[None, :, None] + 1, 2 * N_EXPERTS).reshape(nc, N_EXPERTS * rows)
    tok_ids = jnp.broadcast_to(jnp.arange(a, dtype=I32)[None, :] // TOP_K, (nc, a))
    keys = jnp.concatenate([2 * ec * mult + tok_ids, fill_key * mult + chunk], axis=1)
    gates = jnp.concatenate([wc, jnp.zeros((nc, N_EXPERTS * rows), F32)], axis=1)
    keys, gates = lax.sort((keys, gates), dimension=1, num_keys=1)
    tok = keys % mult
    return (be.reshape(-1), brows.reshape(-1), tok.reshape(nc * nblk, 1, rows), gates.reshape(nc * nblk, 1, rows), nblk)


def _swiglu(xb, wg, wu):
    hg = jnp.dot(xb, wg, preferred_element_type=F32)
    hu = jnp.dot(xb, wu, preferred_element_type=F32)
    return hg * jax.nn.sigmoid(hg) * hu


def _moe_kernel(be_ref, br_ref, tok_ref, gate_ref, x_ref, wg_ref, wu_ref, wd_ref, sg_ref, su_ref, sd_ref,
                g_ref, beta_ref, o_ref, acc_ref, xg_ref, og_ref, *, chunk, rows, alpha):
    c = pl.program_id(0)
    b = pl.program_id(1)
    nblk = pl.num_programs(1)
    group = 8
    sub = min(512, chunk)

    @pl.when((c == 0) & (b == 0))
    def _():
        xg_ref[...] = jnp.zeros_like(xg_ref)

    @pl.when(b == 0)
    def _():
        for s0 in range(0, chunk, sub):
            xb = _tm_load(x_ref, s0, sub).astype(BF16)
            h = _swiglu(xb, sg_ref[...], su_ref[...]).astype(BF16)
            _tm_store(acc_ref, s0, jnp.dot(h, sd_ref[...], preferred_element_type=F32))
        acc_ref[chunk * PIECES:(chunk + 1) * PIECES, :] = jnp.zeros((PIECES, LANES), F32)

    nrow = br_ref[c * nblk + b]

    @pl.when(nrow > 0)
    def _():
        ngrp = (nrow + group - 1) // group

        def gather(gi, carry):
            base = pl.multiple_of(gi * group, group)
            for i in range(group):
                t = jnp.minimum(tok_ref[0, 0, base + i], chunk - 1)
                src = pl.multiple_of(t * PIECES, PIECES)
                dst = pl.multiple_of((base + i) * PIECES, PIECES)
                xg_ref[pl.ds(dst, PIECES), :] = x_ref[pl.ds(src, PIECES), :]
            return carry

        lax.fori_loop(0, ngrp, gather, 0)
        xb = _tm_load(xg_ref, 0, rows).astype(BF16)
        h = _swiglu(xb, wg_ref[0], wu_ref[0])
        eye = lax.broadcasted_iota(I32, (rows, rows), 0) == lax.broadcasted_iota(I32, (rows, rows), 1)
        gcol = jnp.sum(jnp.where(eye, gate_ref[0], 0.0), -1, keepdims=True)
        _tm_store(og_ref, 0, jnp.dot((h * gcol).astype(BF16), wd_ref[0], preferred_element_type=F32))

        def scatter(gi, carry):
            base = pl.multiple_of(gi * group, group)
            dsts = [pl.multiple_of(tok_ref[0, 0, base + i] * PIECES, PIECES) for i in range(group)]
            vals = [acc_ref[pl.ds(dsts[i], PIECES), :]
                    + og_ref[pl.ds(pl.multiple_of((base + i) * PIECES, PIECES), PIECES), :] for i in range(group)]
            for i in range(group):
                acc_ref[pl.ds(dsts[i], PIECES), :] = vals[i]
            return carry

        lax.fori_loop(0, ngrp, scatter, 0)

    @pl.when(b == nblk - 1)
    def _():
        for s0 in range(0, chunk, sub):
            z = alpha * _tm_load(x_ref, s0, sub) + _tm_load(acc_ref, s0, sub)
            o_ref[s0:s0 + sub, :] = _layer_norm(z, g_ref[...], beta_ref[...])


def _moe_ln(x_tm, eidx, wts, wg, wu, wd, sg, su, sd, g, beta, alpha):
    n = x_tm.shape[0] // PIECES
    d = PIECES * LANES
    chunk = min(MOE_CHUNK, n)
    rows = MOE_ROWS
    be, brows, tok, gate, nblk = _dispatch(eidx, wts, chunk, rows)
    nc = n // chunk
    de = wg.shape[2]
    full = lambda arr: pl.BlockSpec(arr.shape, lambda c, b, be_r, br_r: (0,) * arr.ndim)
    grid_spec = pltpu.PrefetchScalarGridSpec(
        num_scalar_prefetch=2,
        grid=(nc, nblk),
        in_specs=[
            pl.BlockSpec((1, 1, rows), lambda c, b, be_r, br_r: (c * nblk + b, 0, 0), memory_space=pltpu.SMEM),
            pl.BlockSpec((1, 1, rows), lambda c, b, be_r, br_r: (c * nblk + b, 0, 0)),
            pl.BlockSpec((chunk * PIECES, LANES), lambda c, b, be_r, br_r: (c, 0)),
            pl.BlockSpec((1, d, de), lambda c, b, be_r, br_r: (be_r[c * nblk + b], 0, 0)),
            pl.BlockSpec((1, d, de), lambda c, b, be_r, br_r: (be_r[c * nblk + b], 0, 0)),
            pl.BlockSpec((1, de, d), lambda c, b, be_r, br_r: (be_r[c * nblk + b], 0, 0)),
            full(sg), full(su), full(sd), full(g), full(beta),
        ],
        out_specs=pl.BlockSpec((chunk, d), lambda c, b, be_r, br_r: (c, 0)),
        scratch_shapes=[pltpu.VMEM(((chunk + 1) * PIECES, LANES), F32), pltpu.VMEM((rows * PIECES, LANES), F32),
                        pltpu.VMEM((rows * PIECES, LANES), F32)],
    )
    return pl.pallas_call(
        functools.partial(_moe_kernel, chunk=chunk, rows=rows, alpha=alpha),
        grid_spec=grid_spec,
        out_shape=jax.ShapeDtypeStruct((n, d), F32),
        compiler_params=_cparams("arbitrary", "arbitrary"),
        name="moe_experts",
    )(be, brows, tok, gate, x_tm, wg, wu, wd, sg, su, sd, g, beta)


def _rope_tables(positions):
    pos = positions.reshape(-1).astype(F32)[:, None]
    n = pos.shape[0]

    def cs(dim):
        inv = ROPE_THETA ** (-jnp.arange(0, dim, 2, dtype=F32) / dim)
        ang = pos * inv
        return jnp.cos(ang), jnp.sin(ang)

    c, s = cs(HEAD_DIM)
    c64 = jnp.concatenate([c, c, c, c], -1)
    s64 = jnp.concatenate([-s, s, -s, s], -1)
    c, s = cs(MLA_ROPE)
    one = jnp.ones((n, MLA_NOPE), F32)
    pad = LANES - MLA_NOPE - MLA_ROPE
    cm = jnp.concatenate([one, c, c, jnp.ones((n, pad), F32)], -1)
    sm = jnp.concatenate([0 * one, -s, s, jnp.zeros((n, pad), F32)], -1)
    return (c64, s64), (cm, sm)


def _even_mixer(xf, b, s, tabs, w_in, g_qa, w_qb, g_kva, w_kvb):
    d = xf.shape[1]
    tab64, tabm = tabs
    na = 3 * 512
    kpe_lo = na + 256 + 128
    wpad = jnp.concatenate([w_in[:, :kpe_lo], jnp.zeros((d, MLA_NOPE), F32), w_in[:, kpe_lo:],
                            jnp.zeros((d, LANES - MLA_NOPE - MLA_ROPE), F32)], 1).astype(BF16)
    nh = w_qb.shape[1] // (MLA_NOPE + MLA_ROPE)
    wqb = jnp.pad(w_qb.reshape(-1, nh, MLA_NOPE + MLA_ROPE), ((0, 0), (0, 0), (0, LANES - MLA_NOPE - MLA_ROPE)))
    wqb = wqb.reshape(-1, nh * LANES).astype(BF16)
    wkv3 = w_kvb.reshape(-1, nh, MLA_NOPE + MLA_V)
    wkk = jnp.pad(wkv3[:, :, :MLA_NOPE], ((0, 0), (0, 0), (0, LANES - MLA_NOPE))).reshape(-1, nh * LANES).astype(BF16)
    wkv = wkv3[:, :, MLA_NOPE:].reshape(-1, nh * MLA_V).astype(BF16)
    segs = [(0, 512, True, QK_SCALE), (512, 1024, True, 1.0), (1024, 1536, False, 1.0)]
    qa, ka, va, qm, km, vm = _projection(
        xf, wpad, segs, tab64,
        mla_args=(tabm[0], tabm[1], g_qa.reshape(1, -1), wqb, g_kva.reshape(1, -1), wkk, wkv),
        mla_out_widths=(nh * LANES, nh * LANES, nh * MLA_V))
    o_a = _moba_attention(qa, ka, va, b, s)
    o_b = _mla_attention(qm, km, vm, b, s)
    return o_a, o_b


def _odd_mixer(xf, b, s, tabs, w_in, sinks, lam_p, g_subln, lambda_init):
    segs = [(0, 512, True, QK_SCALE), (512, 640, True, 1.0), (640, 768, False, 1.0),
            (768, 1280, True, QK_SCALE), (1280, 1792, True, 1.0), (1792, 2304, False, 1.0)]
    qc, kc, vc, qd, kd, vd = _projection(xf, w_in.astype(BF16), segs, tabs[0])
    o_c = _swa_attention(qc, kc, vc, sinks.astype(F32), b, s)
    o_d = _diff_attention(qd, kd, vd, lam_p.astype(F32), g_subln.reshape(1, -1), lambda_init, b, s)
    return o_c, o_d


def kernel(x, positions, ev_w_in, ev_w_out, mla_g_qa, mla_w_qb, mla_g_kva, mla_w_kvb, od_w_in, od_w_out, swa_sinks, diff_lambda, diff_g_subln, ln_g, ln_b, moe_w_router, moe_bias, moe_w_gate, moe_w_up, moe_w_down, sh_w_gate, sh_w_up, sh_w_down):
    b, s, d = x.shape
    depth = ln_g.shape[0]
    alpha = float((2 * depth) ** 0.25)
    xf = x.reshape(b * s, d)
    tabs = _rope_tables(positions)
    for l in range(depth):
        i = l // 2
        if l % 2 == 0:
            o1, o2 = _even_mixer(xf, b, s, tabs, ev_w_in[i], mla_g_qa[i], mla_w_qb[i], mla_g_kva[i], mla_w_kvb[i])
            w_out = ev_w_out[i]
        else:
            lambda_init = 0.8 - 0.6 * float(np.exp(-0.3 * l))
            o1, o2 = _odd_mixer(xf, b, s, tabs, od_w_in[i], swa_sinks[i], diff_lambda[i], diff_g_subln[i], lambda_init)
            w_out = od_w_out[i]
        xf, x_tm = _out_ln(xf, o1, o2, w_out.astype(BF16), ln_g[l, 0].reshape(1, d), ln_b[l, 0].reshape(1, d), alpha)
        eidx, wts = _router(xf, moe_w_router[l], moe_bias[l])
        xf = _moe_ln(x_tm, eidx, wts, moe_w_gate[l].astype(BF16), moe_w_up[l].astype(BF16),
                     moe_w_down[l].astype(BF16), sh_w_gate[l].astype(BF16), sh_w_up[l].astype(BF16),
                     sh_w_down[l].astype(BF16), ln_g[l, 1].reshape(1, d), ln_b[l, 1].reshape(1, d), alpha)
    return xf.reshape(b, s, d)
```

```python
import functools

import numpy as np
import jax
import jax.numpy as jnp
from jax import lax
from jax.experimental import pallas as pl
from jax.experimental.pallas import tpu as pltpu

F32 = jnp.float32
BF16 = jnp.bfloat16
I32 = jnp.int32

LANES = 128
NEG = -1e30

HEAD_DIM = 64
MOBA_BLOCK = 256
MOBA_TOPK = 3
MLA_NOPE = 64
MLA_ROPE = 32
MLA_V = 64
SWA_WINDOW = 128
N_EXPERTS = 64
N_GROUPS = 8
TOPK_GROUPS = 4
TOP_K = 8
ROUTED_SCALE = 2.5
ROPE_THETA = 10000.0
QK_SCALE = HEAD_DIM ** -0.5
MLA_SCALE = (MLA_NOPE + MLA_ROPE) ** -0.5

PROJ_ROWS = 512
ATTN_TILE = 1024
SWA_ROWS = 512
ROUTER_ROWS = 512
MOE_CHUNK = 2048
MOE_ROWS = 320
VMEM_LIMIT = 56 * 1024 * 1024

_NT = (((1,), (1,)), ((), ()))


def _cparams(*sem):
    return pltpu.CompilerParams(dimension_semantics=sem, vmem_limit_bytes=VMEM_LIMIT)


def _lane_iota(rows):
    return lax.broadcasted_iota(I32, (rows, LANES), 1)


def _rope(x, c, s, first_half, half):
    rot = jnp.where(first_half, pltpu.roll(x, LANES - half, 1), pltpu.roll(x, half, 1))
    return x * c + rot * s


def _rms(x, g, eps):
    return x * lax.rsqrt(jnp.mean(x * x, -1, keepdims=True) + eps) * g


def _proj_kernel(segs, mla, x_ref, w_ref, c64_ref, s64_ref, *refs):
    xb = x_ref[...].astype(BF16)
    rows = xb.shape[0]
    lane = _lane_iota(rows)
    fh64 = (lane % HEAD_DIM) < HEAD_DIM // 2
    c64 = c64_ref[...]
    s64 = s64_ref[...]

    def seg(lo, hi):
        return jnp.dot(xb, w_ref[:, lo:hi], preferred_element_type=F32)

    if mla:
        cm_ref, sm_ref, gq_ref, wqb_ref, gkv_ref, wkk_ref, wkv_ref = refs[:7]
        refs = refs[7:]
    for (lo, hi, rope, scale), o_ref in zip(segs, refs):
        h = seg(lo, hi)
        if rope:
            for g in range((hi - lo) // LANES):
                sl = slice(g * LANES, (g + 1) * LANES)
                o_ref[:, sl] = (_rope(h[:, sl], c64, s64, fh64, HEAD_DIM // 2) * scale).astype(BF16)
        else:
            o_ref[...] = h.astype(BF16)
    if mla:
        qm_ref, km_ref, vm_ref = refs[len(segs):]
        cm = cm_ref[...]
        sm = sm_ref[...]
        fhm = (lane >= MLA_NOPE) & (lane < MLA_NOPE + MLA_ROPE // 2)
        lo = segs[-1][1]
        qn = _rms(seg(lo, lo + 256), gq_ref[...], 1e-6).astype(BF16)
        qm = jnp.dot(qn, wqb_ref[...], preferred_element_type=F32)
        kvn = _rms(seg(lo + 256, lo + 384), gkv_ref[...], 1e-6).astype(BF16)
        kk = jnp.dot(kvn, wkk_ref[...], preferred_element_type=F32)
        kpe = _rope(seg(lo + 384, lo + 512), cm, sm, fhm, MLA_ROPE // 2)
        for g in range(qm.shape[1] // LANES):
            sl = slice(g * LANES, (g + 1) * LANES)
            qm_ref[:, sl] = (_rope(qm[:, sl], cm, sm, fhm, MLA_ROPE // 2) * MLA_SCALE).astype(BF16)
            km_ref[:, sl] = (kk[:, sl] + kpe).astype(BF16)
        vm_ref[...] = jnp.dot(kvn, wkv_ref[...], preferred_element_type=F32).astype(BF16)


def _projection(xf, w, segs, tabs, mla_args=None, mla_out_widths=()):
    n, d = xf.shape
    tm = min(PROJ_ROWS, n)
    row = lambda width: pl.BlockSpec((tm, width), lambda i: (i, 0))
    full = lambda a: pl.BlockSpec(a.shape, lambda i: (0,) * a.ndim)
    ins = [xf, w, tabs[0], tabs[1]]
    in_specs = [row(d), full(w), row(LANES), row(LANES)]
    if mla_args is not None:
        cm, sm, gq, wqb, gkv, wkk, wkv = mla_args
        ins += [cm, sm, gq, wqb, gkv, wkk, wkv]
        in_specs += [row(LANES), row(LANES), full(gq), full(wqb), full(gkv), full(wkk), full(wkv)]
    widths = [hi - lo for lo, hi, _, _ in segs] + list(mla_out_widths)
    return pl.pallas_call(
        functools.partial(_proj_kernel, tuple(segs), mla_args is not None),
        grid=(n // tm,),
        in_specs=in_specs,
        out_specs=[row(wd) for wd in widths],
        out_shape=[jax.ShapeDtypeStruct((n, wd), BF16) for wd in widths],
        compiler_params=_cparams("parallel"),
        name="projection",
    )(*ins)


def _flash_update(c, q, k, v, bias, state, first):
    m_ref, l_ref, acc_ref = state
    s = lax.dot_general(q, k, _NT, preferred_element_type=F32)
    if bias is not None:
        s = s + bias
    mx = jnp.max(s, -1, keepdims=True)
    if first:
        p = jnp.exp(s - mx)
        l_ref[c] = jnp.sum(p, -1, keepdims=True)
        acc_ref[c] = jnp.dot(p.astype(BF16), v, preferred_element_type=F32)
        m_ref[c] = mx
    else:
        m_prev = m_ref[c]
        m_new = jnp.maximum(m_prev, mx)
        alpha = jnp.exp(m_prev - m_new)
        p = jnp.exp(s - m_new)
        l_ref[c] = alpha * l_ref[c] + jnp.sum(p, -1, keepdims=True)
        acc_ref[c] = alpha * acc_ref[c] + jnp.dot(p.astype(BF16), v, preferred_element_type=F32)
        m_ref[c] = m_new


def _flash_state(chains, t):
    return [pltpu.VMEM((chains, t, 1), F32), pltpu.VMEM((chains, t, 1), F32), pltpu.VMEM((chains, t, LANES), F32)]


def _causal_sweep(qi, step):
    step(qi, True)

    def body(j, carry):
        step(j, False)
        return carry

    lax.fori_loop(0, qi, body, 0)


def _causal_bias(t, block=None):
    r = lax.broadcasted_iota(I32, (t, t), 0)
    c = lax.broadcasted_iota(I32, (t, t), 1)
    hide = c > r
    if block is not None:
        hide = hide & ((r // block) == (c // block))
    return jnp.where(hide, NEG, 0.0).astype(F32)


def _mla_kernel(q_ref, k_ref, v_ref, o_ref, m_ref, l_ref, acc_ref, *, t):
    qi = pl.program_id(2)
    dbias = _causal_bias(t)
    state = (m_ref, l_ref, acc_ref)

    def step(j, first):
        ks = pl.multiple_of(j * t, t)
        v = v_ref[pl.ds(ks, t), :]
        for h in range(2):
            hs = slice(h * LANES, (h + 1) * LANES)
            _flash_update(h, q_ref[:, hs], k_ref[pl.ds(ks, t), hs], v, dbias if first else None, state, first)

    _causal_sweep(qi, step)
    lane = _lane_iota(t)
    o_ref[...] = jnp.where(lane < MLA_V, acc_ref[0] / l_ref[0], acc_ref[1] / l_ref[1]).astype(BF16)


def _mla_attention(qm, km, vm, b, s):
    t = min(ATTN_TILE, s)
    nq = s // t
    hp = qm.shape[1] // (2 * LANES)
    return pl.pallas_call(
        functools.partial(_mla_kernel, t=t),
        grid=(b, hp, nq),
        in_specs=[pl.BlockSpec((t, 2 * LANES), lambda bi, h, qi: (bi * nq + qi, h)),
                  pl.BlockSpec((s, 2 * LANES), lambda bi, h, qi: (bi, h)),
                  pl.BlockSpec((s, LANES), lambda bi, h, qi: (bi, h))],
        out_specs=pl.BlockSpec((t, LANES), lambda bi, h, qi: (bi * nq + qi, h)),
        out_shape=jax.ShapeDtypeStruct((b * s, hp * LANES), BF16),
        scratch_shapes=_flash_state(2, t),
        compiler_params=_cparams("parallel", "parallel", "arbitrary"),
        name="mla_attention",
    )(qm, km, vm)


def _moba_kernel(q_ref, k_ref, v_ref, o_ref, kmean_ref, kaug_ref, qaug_ref, m_ref, l_ref, acc_ref, *, t, nb):
    qi = pl.program_id(2)
    bs = MOBA_BLOCK
    s_len = k_ref.shape[0]
    half = LANES // 2

    @pl.when(qi == 0)
    def _():
        kmean_ref[...] = jnp.zeros_like(kmean_ref)
        for n in range(nb):
            blk = k_ref[n * bs:(n + 1) * bs, :].astype(F32)
            kmean_ref[n:n + 1, :] = jnp.sum(blk, 0, keepdims=True) * (1.0 / bs)
        lane_s = _lane_iota(s_len)
        row_blk = lax.broadcasted_iota(I32, (s_len, LANES), 0) // bs
        kf = k_ref[...].astype(F32)
        kaug_ref[0] = jnp.where(lane_s < half, kf, (lane_s - half == row_blk).astype(F32)).astype(BF16)
        kaug_ref[1] = jnp.where(lane_s >= half, kf, (lane_s == row_blk).astype(F32)).astype(BF16)

    lane = _lane_iota(t)
    own = qi * (t // bs) + lax.broadcasted_iota(I32, (t, LANES), 0) // bs
    qf = q_ref[...].astype(F32)
    kmean = kmean_ref[...]
    for h in range(2):
        head = (lane >= h * half) & (lane < (h + 1) * half)
        qh = jnp.where(head, qf, 0.0)
        gate = lax.dot_general(qh, kmean, _NT, preferred_element_type=F32, precision=lax.Precision.HIGHEST)
        g = jnp.where(lane < own, gate, -jnp.inf)
        sel = lane == own
        for _ in range(MOBA_TOPK):
            mx = jnp.max(g, -1, keepdims=True)
            idx = jnp.min(jnp.where(g == mx, lane, LANES), -1, keepdims=True)
            hit = lane == idx
            sel = sel | (hit & (lane < own))
            g = jnp.where(hit, -jnp.inf, g)
        selb = jnp.where(sel | (lane >= nb), 0.0, NEG).astype(F32)
        if h == 0:
            selb = pltpu.roll(selb, half, 1)
        qaug_ref[h] = jnp.where(head, qf, selb).astype(BF16)

    dbias = _causal_bias(t, bs)
    state = (m_ref, l_ref, acc_ref)

    def step(j, first):
        ks = pl.multiple_of(j * t, t)
        v = v_ref[pl.ds(ks, t), :]
        for h in range(2):
            _flash_update(h, qaug_ref[h], kaug_ref[h, pl.ds(ks, t), :], v, dbias if first else None, state, first)

    _causal_sweep(qi, step)
    o_ref[...] = jnp.where(lane < half, acc_ref[0] / l_ref[0], acc_ref[1] / l_ref[1]).astype(BF16)


def _moba_attention(q, k, v, b, s):
    bs = MOBA_BLOCK
    nb = s // bs
    assert nb <= LANES // 2, "block gate must fit the spare half of the lanes"
    t = min(ATTN_TILE, s)
    nq = s // t
    hp = q.shape[1] // LANES
    return pl.pallas_call(
        functools.partial(_moba_kernel, t=t, nb=nb),
        grid=(b, hp, nq),
        in_specs=[pl.BlockSpec((t, LANES), lambda bi, h, qi: (bi * nq + qi, h)),
                  pl.BlockSpec((s, LANES), lambda bi, h, qi: (bi, h)),
                  pl.BlockSpec((s, LANES), lambda bi, h, qi: (bi, h))],
        out_specs=pl.BlockSpec((t, LANES), lambda bi, h, qi: (bi * nq + qi, h)),
        out_shape=jax.ShapeDtypeStruct((b * s, hp * LANES), BF16),
        scratch_shapes=[pltpu.VMEM((LANES, LANES), F32), pltpu.VMEM((2, s, LANES), BF16),
                        pltpu.VMEM((2, t, LANES), BF16)] + _flash_state(2, t),
        compiler_params=_cparams("parallel", "parallel", "arbitrary"),
        name="moba_attention",
    )(q, k, v)


def _diff_kernel(q_ref, k_ref, v_ref, lam_ref, g_ref, o_ref, qs_ref, m_ref, l_ref, acc_ref, *, t, lambda_init):
    qi = pl.program_id(2)
    lane = _lane_iota(t)
    qf = q_ref[...].astype(F32)
    qs_ref[0] = jnp.where(lane < HEAD_DIM, qf, 0.0).astype(BF16)
    qs_ref[1] = jnp.where(lane >= HEAD_DIM, qf, 0.0).astype(BF16)
    dbias = _causal_bias(t)
    state = (m_ref, l_ref, acc_ref)

    def step(j, first):
        ks = pl.multiple_of(j * t, t)
        k = k_ref[pl.ds(ks, t), :]
        v = v_ref[pl.ds(ks, t), :]
        for c in range(2):
            _flash_update(c, qs_ref[c], k, v, dbias if first else None, state, first)

    _causal_sweep(qi, step)
    lp = lam_ref[...]
    lam = (jnp.exp(jnp.sum(lp[0:1] * lp[1:2], -1, keepdims=True))
           - jnp.exp(jnp.sum(lp[2:3] * lp[3:4], -1, keepdims=True)) + lambda_init)
    o = acc_ref[0] / l_ref[0] - lam * (acc_ref[1] / l_ref[1])
    o_ref[...] = (_rms(o, g_ref[...], 1e-5) * (1.0 - lambda_init)).astype(BF16)


def _diff_attention(q, k, v, lam_p, g_subln, lambda_init, b, s):
    t = min(ATTN_TILE, s)
    nq = s // t
    nh = q.shape[1] // LANES
    return pl.pallas_call(
        functools.partial(_diff_kernel, t=t, lambda_init=lambda_init),
        grid=(b, nh, nq),
        in_specs=[pl.BlockSpec((t, LANES), lambda bi, h, qi: (bi * nq + qi, h)),
                  pl.BlockSpec((s, LANES), lambda bi, h, qi: (bi, h)),
                  pl.BlockSpec((s, LANES), lambda bi, h, qi: (bi, h)),
                  pl.BlockSpec(lam_p.shape, lambda bi, h, qi: (0, 0)),
                  pl.BlockSpec(g_subln.shape, lambda bi, h, qi: (0, 0))],
        out_specs=pl.BlockSpec((t, LANES), lambda bi, h, qi: (bi * nq + qi, h)),
        out_shape=jax.ShapeDtypeStruct((b * s, nh * LANES), BF16),
        scratch_shapes=[pltpu.VMEM((2, t, LANES), BF16)] + _flash_state(2, t),
        compiler_params=_cparams("parallel", "parallel", "arbitrary"),
        name="diff_attention",
    )(q, k, v, lam_p, g_subln)


def _swa_kernel(sink_ref, q_ref, k_ref, v_ref, o_ref, *, rows):
    ti = pl.program_id(1)
    w = SWA_WINDOW
    nsub = rows // w
    n_q = q_ref.shape[1] // HEAD_DIM
    n_kv = k_ref.shape[1] // HEAD_DIM
    grp = n_q // n_kv
    lane_k = lax.broadcasted_iota(I32, (2 * w, LANES), 1)
    lane_o = _lane_iota(w)
    qrow = lax.broadcasted_iota(I32, (grp * w, 2 * w), 0) % w
    kcol = lax.broadcasted_iota(I32, (grp * w, 2 * w), 1)
    for i in range(nsub):
        sb = ti * nsub + i
        start = pl.multiple_of(jnp.maximum(sb - 1, 0) * w, w)
        kb = k_ref[pl.ds(start, 2 * w), :]
        vb = v_ref[pl.ds(start, 2 * w), :]
        rel = qrow + (sb * w - start) - kcol
        bias = jnp.where((rel >= 0) & (rel < w), 0.0, NEG).astype(F32)
        qf = q_ref[i * w:(i + 1) * w, :].astype(F32)
        head_out = [None] * n_q
        for kvh in range(n_kv):
            km = jnp.where(lane_k // HEAD_DIM == kvh, kb.astype(F32), 0.0).astype(BF16)
            qs, sinks = [], []
            for a in range(grp):
                h = kvh * grp + a
                qp = qf[:, (h // 2) * LANES:(h // 2 + 1) * LANES]
                if h % 2 != kvh:
                    qp = pltpu.roll(qp, HEAD_DIM, 1)
                qs.append(qp.astype(BF16))
                sinks.append(jnp.full((w, 1), sink_ref[h], F32))
            sink = jnp.concatenate(sinks, 0)
            s = lax.dot_general(jnp.concatenate(qs, 0), km, _NT, preferred_element_type=F32) + bias
            m = jnp.maximum(jnp.max(s, -1, keepdims=True), sink)
            p = jnp.exp(s - m)
            den = jnp.sum(p, -1, keepdims=True) + jnp.exp(sink - m)
            o = jnp.dot(p.astype(BF16), vb, preferred_element_type=F32) / den
            for a in range(grp):
                h = kvh * grp + a
                oh = o[a * w:(a + 1) * w, :]
                if h % 2 != kvh:
                    oh = pltpu.roll(oh, HEAD_DIM, 1)
                head_out[h] = oh
        for hp in range(n_q // 2):
            pair = jnp.where(lane_o < HEAD_DIM, head_out[2 * hp], head_out[2 * hp + 1])
            o_ref[i * w:(i + 1) * w, hp * LANES:(hp + 1) * LANES] = pair.astype(BF16)


def _swa_attention(q, k, v, sinks, b, s):
    rows = min(SWA_ROWS, s)
    nt = s // rows
    return pl.pallas_call(
        functools.partial(_swa_kernel, rows=rows),
        grid=(b, nt),
        in_specs=[pl.BlockSpec(memory_space=pltpu.SMEM),
                  pl.BlockSpec((rows, q.shape[1]), lambda bi, ti: (bi * nt + ti, 0)),
                  pl.BlockSpec((s, k.shape[1]), lambda bi, ti: (bi, 0)),
                  pl.BlockSpec((s, v.shape[1]), lambda bi, ti: (bi, 0))],
        out_specs=pl.BlockSpec((rows, q.shape[1]), lambda bi, ti: (bi * nt + ti, 0)),
        out_shape=jax.ShapeDtypeStruct(q.shape, BF16),
        compiler_params=_cparams("parallel", "arbitrary"),
        name="swa_attention",
    )(sinks, q, k, v)


def _layer_norm(y, g, b):
    mu = jnp.mean(y, -1, keepdims=True)
    yc = y - mu
    var = jnp.mean(yc * yc, -1, keepdims=True)
    return yc * lax.rsqrt(var + 1e-5) * g + b


PIECES = 8


def _tm_load(ref, r0, nrows):
    return jnp.concatenate([ref[pl.ds(r0 * PIECES + p, nrows, stride=PIECES), :] for p in range(PIECES)], axis=1)


def _tm_store(ref, r0, val):
    for p in range(PIECES):
        ref[pl.ds(r0 * PIECES + p, val.shape[0], stride=PIECES), :] = val[:, p * LANES:(p + 1) * LANES]


def _out_ln_kernel(x_ref, a_ref, b_ref, w_ref, g_ref, beta_ref, o_ref, otm_ref, *, alpha):
    ka = a_ref.shape[1]
    mix = (jnp.dot(a_ref[...], w_ref[:ka, :], preferred_element_type=F32)
           + jnp.dot(b_ref[...], w_ref[ka:, :], preferred_element_type=F32))
    y = _layer_norm(alpha * x_ref[...] + mix, g_ref[...], beta_ref[...])
    o_ref[...] = y
    _tm_store(otm_ref, 0, y)


def _out_ln(xf, a, b, w, g, beta, alpha):
    n, d = xf.shape
    assert d == PIECES * LANES
    tm = min(PROJ_ROWS, n)
    row = lambda width: pl.BlockSpec((tm, width), lambda i: (i, 0))
    full = lambda arr: pl.BlockSpec(arr.shape, lambda i: (0, 0))
    return pl.pallas_call(
        functools.partial(_out_ln_kernel, alpha=alpha),
        grid=(n // tm,),
        in_specs=[row(d), row(a.shape[1]), row(b.shape[1]), full(w), full(g), full(beta)],
        out_specs=[row(d), pl.BlockSpec((tm * PIECES, LANES), lambda i: (i, 0))],
        out_shape=[jax.ShapeDtypeStruct((n, d), F32), jax.ShapeDtypeStruct((n * PIECES, LANES), F32)],
        compiler_params=_cparams("parallel"),
        name="out_proj_ln",
    )(xf, a, b, w, g, beta)


def _row_argmax(v, row, nrow):
    mx = jnp.max(v, 0, keepdims=True)
    idx = jnp.min(jnp.where(v == mx, row, nrow), 0, keepdims=True)
    return mx, idx


def _router_kernel(x_ref, w_ref, bias_ref, eidx_ref, wts_ref):
    logits = jnp.dot(x_ref[...], w_ref[...], preferred_element_type=F32, precision=lax.Precision.HIGHEST)
    tm = logits.shape[0]
    scores = jax.nn.sigmoid(logits.T[:N_EXPERTS, :])
    biased = scores + bias_ref[...]
    per = N_EXPERTS // N_GROUPS
    row_g = lax.broadcasted_iota(I32, (per, tm), 0)
    gsc = []
    for g in range(N_GROUPS):
        v = biased[g * per:(g + 1) * per, :]
        m1, i1 = _row_argmax(v, row_g, per)
        m2 = jnp.max(jnp.where(row_g == i1, -jnp.inf, v), 0, keepdims=True)
        gsc.append(m1 + m2)
    gscore = jnp.concatenate(gsc, 0)
    row_n = lax.broadcasted_iota(I32, (N_GROUPS, tm), 0)
    gok = jnp.zeros((N_GROUPS, tm), F32)
    for _ in range(TOPK_GROUPS):
        _, ig = _row_argmax(gscore, row_n, N_GROUPS)
        hit = row_n == ig
        gok = jnp.where(hit, 1.0, gok)
        gscore = jnp.where(hit, -jnp.inf, gscore)
    cand = jnp.concatenate([jnp.where(gok[g:g + 1, :] > 0.0, biased[g * per:(g + 1) * per, :], -jnp.inf)
                            for g in range(N_GROUPS)], 0)
    row_e = lax.broadcasted_iota(I32, (N_EXPERTS, tm), 0)
    ids, ws = [], []
    for _ in range(TOP_K):
        _, ie = _row_argmax(cand, row_e, N_EXPERTS)
        hit = row_e == ie
        ids.append(ie)
        ws.append(jnp.sum(jnp.where(hit, scores, 0.0), 0, keepdims=True))
        cand = jnp.where(hit, -jnp.inf, cand)
    w = jnp.concatenate(ws, 0)
    eidx_ref[...] = jnp.concatenate(ids, 0)
    wts_ref[...] = w / jnp.sum(w, 0, keepdims=True) * ROUTED_SCALE


def _router(xf, w_router, e_bias):
    n, d = xf.shape
    tm = min(ROUTER_ROWS, n)
    wpad = jnp.pad(w_router, ((0, 0), (0, LANES - N_EXPERTS)))
    bcol = e_bias.astype(F32).reshape(N_EXPERTS, 1)
    out = pl.BlockSpec((TOP_K, tm), lambda i: (0, i))
    return pl.pallas_call(
        _router_kernel,
        grid=(n // tm,),
        in_specs=[pl.BlockSpec((tm, d), lambda i: (i, 0)), pl.BlockSpec(wpad.shape, lambda i: (0, 0)),
                  pl.BlockSpec(bcol.shape, lambda i: (0, 0))],
        out_specs=[out, out],
        out_shape=[jax.ShapeDtypeStruct((TOP_K, n), I32), jax.ShapeDtypeStruct((TOP_K, n), F32)],
        compiler_params=_cparams("parallel"),
        name="router",
    )(xf, wpad, bcol)


def _dispatch(eidx_t, wts_t, chunk, rows):
    n = eidx_t.shape[1]
    nc = n // chunk
    a = chunk * TOP_K
    nblk = -(-a // rows) + N_EXPERTS
    ec = eidx_t.reshape(TOP_K, nc, chunk).transpose(1, 0, 2).reshape(nc, a)
    wc = wts_t.reshape(TOP_K, nc, chunk).transpose(1, 0, 2).reshape(nc, a)
    ex = jnp.arange(N_EXPERTS, dtype=I32)
    counts = jnp.sum((ec[:, :, None] == ex[None, None, :]).astype(I32), axis=1)
    nb_e = (counts + rows - 1) // rows
    npad = nb_e * rows - counts
    bend = jnp.cumsum(nb_e, axis=1).astype(I32)
    bstart = bend - nb_e
    used = bend[:, -1:]
    barange = jnp.arange(nblk, dtype=I32)[None, :]
    be = jnp.sum((bend[:, None, :] <= barange[:, :, None]).astype(I32), axis=-1)
    live = barange < used
    be = jnp.minimum(be, N_EXPERTS - 1)
    be = jnp.where(live, be, jnp.max(jnp.where(live, be, 0), axis=1, keepdims=True))
    onehot = be[:, :, None] == ex[None, None, :]
    pick = lambda t: jnp.sum(jnp.where(onehot, t[:, None, :], 0), axis=-1)
    within = barange - pick(bstart)
    brows = jnp.where(live, jnp.clip(pick(counts) - within * rows, 0, rows), 0).astype(I32)
    fetch = jnp.arange(nc, dtype=I32)[:, None] * nblk + jnp.minimum(barange, used - 1)
    mult = 1 << int(chunk).bit_length()
    nfill = nblk * rows - a
    fill = jnp.arange(rows, dtype=I32)[None, None, :] < npad[:, :, None]
    fill_key = jnp.where(fill, 2 * ex[None, :, None] + 1, 2 * N_EXPERTS).reshape(nc, N_EXPERTS * rows)
    fill_key = jnp.pad(fill_key, ((0, 0), (0, nfill - N_EXPERTS * rows)), constant_values=2 * N_EXPERTS)
    tok_ids = jnp.broadcast_to(jnp.arange(a, dtype=I32)[None, :] % chunk, (nc, a))
    keys = jnp.concatenate([2 * ec * mult + tok_ids, fill_key * mult + chunk], axis=1)
    gates = jnp.concatenate([wc, jnp.zeros((nc, nfill), F32)], axis=1)
    keys, gates = lax.sort((keys, gates), dimension=1, num_keys=1)
    tok = keys % mult
    return (be.reshape(-1), brows.reshape(-1), fetch.reshape(-1), tok.reshape(nc * nblk, 1, rows),
            gates.reshape(nc * nblk, 1, rows), nblk)


def _swiglu(xb, wg, wu):
    hg = jnp.dot(xb, wg, preferred_element_type=F32)
    hu = jnp.dot(xb, wu, preferred_element_type=F32)
    return hg * jax.nn.sigmoid(hg) * hu


def _moe_kernel(be_ref, br_ref, fetch_ref, tok_ref, gate_ref, x_ref, wg_ref, wu_ref, wd_ref, sg_ref, su_ref, sd_ref,
                g_ref, beta_ref, o_ref, acc_ref, xg_ref, og_ref, *, chunk, rows, alpha):
    c = pl.program_id(0)
    b = pl.program_id(1)
    nblk = pl.num_programs(1)
    group = 8
    sub = min(512, chunk)

    @pl.when((c == 0) & (b == 0))
    def _():
        xg_ref[...] = jnp.zeros_like(xg_ref)

    @pl.when(b == 0)
    def _():
        for s0 in range(0, chunk, sub):
            xb = _tm_load(x_ref, s0, sub).astype(BF16)
            h = _swiglu(xb, sg_ref[...], su_ref[...]).astype(BF16)
            _tm_store(acc_ref, s0, jnp.dot(h, sd_ref[...], preferred_element_type=F32))
        acc_ref[chunk * PIECES:(chunk + 1) * PIECES, :] = jnp.zeros((PIECES, LANES), F32)

    nrow = br_ref[c * nblk + b]

    @pl.when(nrow > 0)
    def _():
        ngrp = (nrow + group - 1) // group

        def gather(gi, carry):
            base = pl.multiple_of(gi * group, group)
            for i in range(group):
                t = jnp.minimum(tok_ref[0, 0, base + i], chunk - 1)
                src = pl.multiple_of(t * PIECES, PIECES)
                dst = pl.multiple_of((base + i) * PIECES, PIECES)
                xg_ref[pl.ds(dst, PIECES), :] = x_ref[pl.ds(src, PIECES), :]
            return carry

        lax.fori_loop(0, ngrp, gather, 0)
        xb = _tm_load(xg_ref, 0, rows).astype(BF16)
        h = _swiglu(xb, wg_ref[0], wu_ref[0])
        eye = lax.broadcasted_iota(I32, (rows, rows), 0) == lax.broadcasted_iota(I32, (rows, rows), 1)
        gcol = jnp.sum(jnp.where(eye, gate_ref[0], 0.0), -1, keepdims=True)
        _tm_store(og_ref, 0, jnp.dot((h * gcol).astype(BF16), wd_ref[0], preferred_element_type=F32))

        def scatter(gi, carry):
            base = pl.multiple_of(gi * group, group)
            dsts = [pl.multiple_of(tok_ref[0, 0, base + i] * PIECES, PIECES) for i in range(group)]
            vals = [acc_ref[pl.ds(dsts[i], PIECES), :]
                    + og_ref[pl.ds(pl.multiple_of((base + i) * PIECES, PIECES), PIECES), :] for i in range(group)]
            for i in range(group):
                acc_ref[pl.ds(dsts[i], PIECES), :] = vals[i]
            return carry

        lax.fori_loop(0, ngrp, scatter, 0)

    @pl.when(b == nblk - 1)
    def _():
        for s0 in range(0, chunk, sub):
            z = alpha * _tm_load(x_ref, s0, sub) + _tm_load(acc_ref, s0, sub)
            o_ref[s0:s0 + sub, :] = _layer_norm(z, g_ref[...], beta_ref[...])


def _moe_ln(x_tm, eidx, wts, wg, wu, wd, sg, su, sd, g, beta, alpha):
    n = x_tm.shape[0] // PIECES
    d = PIECES * LANES
    chunk = min(MOE_CHUNK, n)
    rows = MOE_ROWS
    be, brows, fetch, tok, gate, nblk = _dispatch(eidx, wts, chunk, rows)
    nc = n // chunk
    de = wg.shape[2]
    full = lambda arr: pl.BlockSpec(arr.shape, lambda c, b, be_r, br_r, f_r: (0,) * arr.ndim)
    lists = lambda c, b, be_r, br_r, f_r: (f_r[c * nblk + b], 0, 0)
    expert = lambda c, b, be_r, br_r, f_r: (be_r[c * nblk + b], 0, 0)
    grid_spec = pltpu.PrefetchScalarGridSpec(
        num_scalar_prefetch=3,
        grid=(nc, nblk),
        in_specs=[
            pl.BlockSpec((1, 1, rows), lists, memory_space=pltpu.SMEM),
            pl.BlockSpec((1, 1, rows), lists),
            pl.BlockSpec((chunk * PIECES, LANES), lambda c, b, be_r, br_r, f_r: (c, 0)),
            pl.BlockSpec((1, d, de), expert),
            pl.BlockSpec((1, d, de), expert),
            pl.BlockSpec((1, de, d), expert),
            full(sg), full(su), full(sd), full(g), full(beta),
        ],
        out_specs=pl.BlockSpec((chunk, d), lambda c, b, be_r, br_r, f_r: (c, 0)),
        scratch_shapes=[pltpu.VMEM(((chunk + 1) * PIECES, LANES), F32), pltpu.VMEM((rows * PIECES, LANES), F32),
                        pltpu.VMEM((rows * PIECES, LANES), F32)],
    )
    return pl.pallas_call(
        functools.partial(_moe_kernel, chunk=chunk, rows=rows, alpha=alpha),
        grid_spec=grid_spec,
        out_shape=jax.ShapeDtypeStruct((n, d), F32),
        compiler_params=_cparams("arbitrary", "arbitrary"),
        name="moe_experts",
    )(be, brows, fetch, tok, gate, x_tm, wg, wu, wd, sg, su, sd, g, beta)


def _rope_tables(positions):
    pos = positions.reshape(-1).astype(F32)[:, None]
    n = pos.shape[0]

    def cs(dim):
        inv = ROPE_THETA ** (-jnp.arange(0, dim, 2, dtype=F32) / dim)
        ang = pos * inv
        return jnp.cos(ang), jnp.sin(ang)

    c, s = cs(HEAD_DIM)
    c64 = jnp.concatenate([c, c, c, c], -1)
    s64 = jnp.concatenate([-s, s, -s, s], -1)
    c, s = cs(MLA_ROPE)
    one = jnp.ones((n, MLA_NOPE), F32)
    pad = LANES - MLA_NOPE - MLA_ROPE
    cm = jnp.concatenate([one, c, c, jnp.ones((n, pad), F32)], -1)
    sm = jnp.concatenate([0 * one, -s, s, jnp.zeros((n, pad), F32)], -1)
    return (c64, s64), (cm, sm)


def _even_mixer(xf, b, s, tabs, w_in, g_qa, w_qb, g_kva, w_kvb):
    d = xf.shape[1]
    tab64, tabm = tabs
    na = 3 * 512
    kpe_lo = na + 256 + 128
    wpad = jnp.concatenate([w_in[:, :kpe_lo], jnp.zeros((d, MLA_NOPE), F32), w_in[:, kpe_lo:],
                            jnp.zeros((d, LANES - MLA_NOPE - MLA_ROPE), F32)], 1).astype(BF16)
    nh = w_qb.shape[1] // (MLA_NOPE + MLA_ROPE)
    wqb = jnp.pad(w_qb.reshape(-1, nh, MLA_NOPE + MLA_ROPE), ((0, 0), (0, 0), (0, LANES - MLA_NOPE - MLA_ROPE)))
    wqb = wqb.reshape(-1, nh * LANES).astype(BF16)
    wkv3 = w_kvb.reshape(-1, nh, MLA_NOPE + MLA_V)
    wkk = jnp.pad(wkv3[:, :, :MLA_NOPE], ((0, 0), (0, 0), (0, LANES - MLA_NOPE))).reshape(-1, nh * LANES).astype(BF16)
    wkv = wkv3[:, :, MLA_NOPE:].reshape(-1, nh * MLA_V).astype(BF16)
    segs = [(0, 512, True, QK_SCALE), (512, 1024, True, 1.0), (1024, 1536, False, 1.0)]
    qa, ka, va, qm, km, vm = _projection(
        xf, wpad, segs, tab64,
        mla_args=(tabm[0], tabm[1], g_qa.reshape(1, -1), wqb, g_kva.reshape(1, -1), wkk, wkv),
        mla_out_widths=(nh * LANES, nh * LANES, nh * MLA_V))
    o_a = _moba_attention(qa, ka, va, b, s)
    o_b = _mla_attention(qm, km, vm, b, s)
    return o_a, o_b


def _odd_mixer(xf, b, s, tabs, w_in, sinks, lam_p, g_subln, lambda_init):
    segs = [(0, 512, True, QK_SCALE), (512, 640, True, 1.0), (640, 768, False, 1.0),
            (768, 1280, True, QK_SCALE), (1280, 1792, True, 1.0), (1792, 2304, False, 1.0)]
    qc, kc, vc, qd, kd, vd = _projection(xf, w_in.astype(BF16), segs, tabs[0])
    o_c = _swa_attention(qc, kc, vc, sinks.astype(F32), b, s)
    o_d = _diff_attention(qd, kd, vd, lam_p.astype(F32), g_subln.reshape(1, -1), lambda_init, b, s)
    return o_c, o_d


def kernel(x, positions, ev_w_in, ev_w_out, mla_g_qa, mla_w_qb, mla_g_kva, mla_w_kvb, od_w_in, od_w_out, swa_sinks, diff_lambda, diff_g_subln, ln_g, ln_b, moe_w_router, moe_bias, moe_w_gate, moe_w_up, moe_w_down, sh_w_gate, sh_w_up, sh_w_down):
    b, s, d = x.shape
    depth = ln_g.shape[0]
    alpha = float((2 * depth) ** 0.25)
    xf = x.reshape(b * s, d)
    tabs = _rope_tables(positions)
    for l in range(depth):
        i = l // 2
        if l % 2 == 0:
            o1, o2 = _even_mixer(xf, b, s, tabs, ev_w_in[i], mla_g_qa[i], mla_w_qb[i], mla_g_kva[i], mla_w_kvb[i])
            w_out = ev_w_out[i]
        else:
            lambda_init = 0.8 - 0.6 * float(np.exp(-0.3 * l))
            o1, o2 = _odd_mixer(xf, b, s, tabs, od_w_in[i], swa_sinks[i], diff_lambda[i], diff_g_subln[i], lambda_init)
            w_out = od_w_out[i]
        xf, x_tm = _out_ln(xf, o1, o2, w_out.astype(BF16), ln_g[l, 0].reshape(1, d), ln_b[l, 0].reshape(1, d), alpha)
        eidx, wts = _router(xf, moe_w_router[l], moe_bias[l])
        xf = _moe_ln(x_tm, eidx, wts, moe_w_gate[l].astype(BF16), moe_w_up[l].astype(BF16),
                     moe_w_down[l].astype(BF16), sh_w_gate[l].astype(BF16), sh_w_up[l].astype(BF16),
                     sh_w_down[l].astype(BF16), ln_g[l, 1].reshape(1, d), ln_b[l, 1].reshape(1, d), alpha)
    return xf.reshape(b, s, d)
```

```python
import functools

import numpy as np
import jax
import jax.numpy as jnp
from jax import lax
from jax.experimental import pallas as pl
from jax.experimental.pallas import tpu as pltpu

F32 = jnp.float32
BF16 = jnp.bfloat16
I32 = jnp.int32

LANES = 128
NEG = -1e30

HEAD_DIM = 64
MOBA_BLOCK = 256
MOBA_TOPK = 3
MLA_NOPE = 64
MLA_ROPE = 32
MLA_V = 64
SWA_WINDOW = 128
N_EXPERTS = 64
N_GROUPS = 8
TOPK_GROUPS = 4
TOP_K = 8
ROUTED_SCALE = 2.5
ROPE_THETA = 10000.0
QK_SCALE = HEAD_DIM ** -0.5
MLA_SCALE = (MLA_NOPE + MLA_ROPE) ** -0.5

PROJ_ROWS = 512
ATTN_TILE = 1024
SWA_ROWS = 512
ROUTER_ROWS = 512
MOE_CHUNK = 2048
MOE_ROWS = 320
VMEM_LIMIT = 56 * 1024 * 1024

_NT = (((1,), (1,)), ((), ()))


def _cparams(*sem):
    return pltpu.CompilerParams(dimension_semantics=sem, vmem_limit_bytes=VMEM_LIMIT)


def _lane_iota(rows):
    return lax.broadcasted_iota(I32, (rows, LANES), 1)


def _rope(x, c, s, first_half, half):
    rot = jnp.where(first_half, pltpu.roll(x, LANES - half, 1), pltpu.roll(x, half, 1))
    return x * c + rot * s


def _rms(x, g, eps):
    return x * lax.rsqrt(jnp.mean(x * x, -1, keepdims=True) + eps) * g


def _proj_kernel(segs, mla, x_ref, w_ref, c64_ref, s64_ref, *refs):
    xb = x_ref[...].astype(BF16)
    rows = xb.shape[0]
    lane = _lane_iota(rows)
    fh64 = (lane % HEAD_DIM) < HEAD_DIM // 2
    c64 = c64_ref[...]
    s64 = s64_ref[...]

    def seg(lo, hi):
        return jnp.dot(xb, w_ref[:, lo:hi], preferred_element_type=F32)

    if mla:
        cm_ref, sm_ref, gq_ref, wqb_ref, gkv_ref, wkk_ref, wkv_ref = refs[:7]
        refs = refs[7:]
    for (lo, hi, rope, scale), o_ref in zip(segs, refs):
        h = seg(lo, hi)
        if rope:
            for g in range((hi - lo) // LANES):
                sl = slice(g * LANES, (g + 1) * LANES)
                o_ref[:, sl] = (_rope(h[:, sl], c64, s64, fh64, HEAD_DIM // 2) * scale).astype(BF16)
        else:
            o_ref[...] = h.astype(BF16)
    if mla:
        qm_ref, km_ref, vm_ref = refs[len(segs):]
        cm = cm_ref[...]
        sm = sm_ref[...]
        fhm = (lane >= MLA_NOPE) & (lane < MLA_NOPE + MLA_ROPE // 2)
        lo = segs[-1][1]
        qn = _rms(seg(lo, lo + 256), gq_ref[...], 1e-6).astype(BF16)
        qm = jnp.dot(qn, wqb_ref[...], preferred_element_type=F32)
        kvn = _rms(seg(lo + 256, lo + 384), gkv_ref[...], 1e-6).astype(BF16)
        kk = jnp.dot(kvn, wkk_ref[...], preferred_element_type=F32)
        kpe = _rope(seg(lo + 384, lo + 512), cm, sm, fhm, MLA_ROPE // 2)
        for g in range(qm.shape[1] // LANES):
            sl = slice(g * LANES, (g + 1) * LANES)
            qm_ref[:, sl] = (_rope(qm[:, sl], cm, sm, fhm, MLA_ROPE // 2) * MLA_SCALE).astype(BF16)
            km_ref[:, sl] = (kk[:, sl] + kpe).astype(BF16)
        vm_ref[...] = jnp.dot(kvn, wkv_ref[...], preferred_element_type=F32).astype(BF16)


def _projection(xf, w, segs, tabs, mla_args=None, mla_out_widths=()):
    n, d = xf.shape
    tm = min(PROJ_ROWS, n)
    row = lambda width: pl.BlockSpec((tm, width), lambda i: (i, 0))
    full = lambda a: pl.BlockSpec(a.shape, lambda i: (0,) * a.ndim)
    ins = [xf, w, tabs[0], tabs[1]]
    in_specs = [row(d), full(w), row(LANES), row(LANES)]
    if mla_args is not None:
        cm, sm, gq, wqb, gkv, wkk, wkv = mla_args
        ins += [cm, sm, gq, wqb, gkv, wkk, wkv]
        in_specs += [row(LANES), row(LANES), full(gq), full(wqb), full(gkv), full(wkk), full(wkv)]
    widths = [hi - lo for lo, hi, _, _ in segs] + list(mla_out_widths)
    return pl.pallas_call(
        functools.partial(_proj_kernel, tuple(segs), mla_args is not None),
        grid=(n // tm,),
        in_specs=in_specs,
        out_specs=[row(wd) for wd in widths],
        out_shape=[jax.ShapeDtypeStruct((n, wd), BF16) for wd in widths],
        compiler_params=_cparams("parallel"),
        name="projection",
    )(*ins)


def _flash_update(c, q, k, v, bias, state, first):
    m_ref, l_ref, acc_ref = state
    s = lax.dot_general(q, k, _NT, preferred_element_type=F32)
    if bias is not None:
        s = s + bias
    mx = jnp.max(s, -1, keepdims=True)
    if first:
        p = jnp.exp(s - mx)
        l_ref[c] = jnp.sum(p, -1, keepdims=True)
        acc_ref[c] = jnp.dot(p.astype(BF16), v, preferred_element_type=F32)
        m_ref[c] = mx
    else:
        m_prev = m_ref[c]
        m_new = jnp.maximum(m_prev, mx)
        alpha = jnp.exp(m_prev - m_new)
        p = jnp.exp(s - m_new)
        l_ref[c] = alpha * l_ref[c] + jnp.sum(p, -1, keepdims=True)
        acc_ref[c] = alpha * acc_ref[c] + jnp.dot(p.astype(BF16), v, preferred_element_type=F32)
        m_ref[c] = m_new


def _flash_state(chains, t):
    return [pltpu.VMEM((chains, t, 1), F32), pltpu.VMEM((chains, t, 1), F32), pltpu.VMEM((chains, t, LANES), F32)]


def _causal_sweep(qi, step):
    step(qi, True)

    def body(j, carry):
        step(j, False)
        return carry

    lax.fori_loop(0, qi, body, 0)


def _causal_bias(t, block=None):
    r = lax.broadcasted_iota(I32, (t, t), 0)
    c = lax.broadcasted_iota(I32, (t, t), 1)
    hide = c > r
    if block is not None:
        hide = hide & ((r // block) == (c // block))
    return jnp.where(hide, NEG, 0.0).astype(F32)


def _mla_kernel(q_ref, k_ref, v_ref, o_ref, m_ref, l_ref, acc_ref, *, t):
    qi = pl.program_id(2)
    dbias = _causal_bias(t)
    state = (m_ref, l_ref, acc_ref)

    def step(j, first):
        ks = pl.multiple_of(j * t, t)
        v = v_ref[pl.ds(ks, t), :]
        for h in range(2):
            hs = slice(h * LANES, (h + 1) * LANES)
            _flash_update(h, q_ref[:, hs], k_ref[pl.ds(ks, t), hs], v, dbias if first else None, state, first)

    _causal_sweep(qi, step)
    lane = _lane_iota(t)
    o_ref[...] = jnp.where(lane < MLA_V, acc_ref[0] / l_ref[0], acc_ref[1] / l_ref[1]).astype(BF16)


def _mla_attention(qm, km, vm, b, s):
    t = min(ATTN_TILE, s)
    nq = s // t
    hp = qm.shape[1] // (2 * LANES)
    return pl.pallas_call(
        functools.partial(_mla_kernel, t=t),
        grid=(b, hp, nq),
        in_specs=[pl.BlockSpec((t, 2 * LANES), lambda bi, h, qi: (bi * nq + qi, h)),
                  pl.BlockSpec((s, 2 * LANES), lambda bi, h, qi: (bi, h)),
                  pl.BlockSpec((s, LANES), lambda bi, h, qi: (bi, h))],
        out_specs=pl.BlockSpec((t, LANES), lambda bi, h, qi: (bi * nq + qi, h)),
        out_shape=jax.ShapeDtypeStruct((b * s, hp * LANES), BF16),
        scratch_shapes=_flash_state(2, t),
        compiler_params=_cparams("parallel", "parallel", "arbitrary"),
        name="mla_attention",
    )(qm, km, vm)


def _moba_kernel(q_ref, k_ref, v_ref, o_ref, kmean_ref, kaug_ref, qaug_ref, m_ref, l_ref, acc_ref, *, t, nb):
    qi = pl.program_id(2)
    bs = MOBA_BLOCK
    s_len = k_ref.shape[0]
    half = LANES // 2

    @pl.when(qi == 0)
    def _():
        kmean_ref[...] = jnp.zeros_like(kmean_ref)
        for n in range(nb):
            blk = k_ref[n * bs:(n + 1) * bs, :].astype(F32)
            kmean_ref[n:n + 1, :] = jnp.sum(blk, 0, keepdims=True) * (1.0 / bs)
        lane_s = _lane_iota(s_len)
        row_blk = lax.broadcasted_iota(I32, (s_len, LANES), 0) // bs
        kf = k_ref[...].astype(F32)
        kaug_ref[0] = jnp.where(lane_s < half, kf, (lane_s - half == row_blk).astype(F32)).astype(BF16)
        kaug_ref[1] = jnp.where(lane_s >= half, kf, (lane_s == row_blk).astype(F32)).astype(BF16)

    lane = _lane_iota(t)
    own = qi * (t // bs) + lax.broadcasted_iota(I32, (t, LANES), 0) // bs
    qf = q_ref[...].astype(F32)
    kmean = kmean_ref[...]
    for h in range(2):
        head = (lane >= h * half) & (lane < (h + 1) * half)
        qh = jnp.where(head, qf, 0.0)
        gate = lax.dot_general(qh, kmean, _NT, preferred_element_type=F32, precision=lax.Precision.HIGHEST)
        g = jnp.where(lane < own, gate, -jnp.inf)
        sel = lane == own
        for _ in range(MOBA_TOPK):
            mx = jnp.max(g, -1, keepdims=True)
            idx = jnp.min(jnp.where(g == mx, lane, LANES), -1, keepdims=True)
            hit = lane == idx
            sel = sel | (hit & (lane < own))
            g = jnp.where(hit, -jnp.inf, g)
        selb = jnp.where(sel | (lane >= nb), 0.0, NEG).astype(F32)
        if h == 0:
            selb = pltpu.roll(selb, half, 1)
        qaug_ref[h] = jnp.where(head, qf, selb).astype(BF16)

    dbias = _causal_bias(t, bs)
    state = (m_ref, l_ref, acc_ref)

    def step(j, first):
        ks = pl.multiple_of(j * t, t)
        v = v_ref[pl.ds(ks, t), :]
        for h in range(2):
            _flash_update(h, qaug_ref[h], kaug_ref[h, pl.ds(ks, t), :], v, dbias if first else None, state, first)

    _causal_sweep(qi, step)
    o_ref[...] = jnp.where(lane < half, acc_ref[0] / l_ref[0], acc_ref[1] / l_ref[1]).astype(BF16)


def _moba_attention(q, k, v, b, s):
    bs = MOBA_BLOCK
    nb = s // bs
    assert nb <= LANES // 2, "block gate must fit the spare half of the lanes"
    t = min(ATTN_TILE, s)
    nq = s // t
    hp = q.shape[1] // LANES
    return pl.pallas_call(
        functools.partial(_moba_kernel, t=t, nb=nb),
        grid=(b, hp, nq),
        in_specs=[pl.BlockSpec((t, LANES), lambda bi, h, qi: (bi * nq + qi, h)),
                  pl.BlockSpec((s, LANES), lambda bi, h, qi: (bi, h)),
                  pl.BlockSpec((s, LANES), lambda bi, h, qi: (bi, h))],
        out_specs=pl.BlockSpec((t, LANES), lambda bi, h, qi: (bi * nq + qi, h)),
        out_shape=jax.ShapeDtypeStruct((b * s, hp * LANES), BF16),
        scratch_shapes=[pltpu.VMEM((LANES, LANES), F32), pltpu.VMEM((2, s, LANES), BF16),
                        pltpu.VMEM((2, t, LANES), BF16)] + _flash_state(2, t),
        compiler_params=_cparams("parallel", "parallel", "arbitrary"),
        name="moba_attention",
    )(q, k, v)


def _diff_kernel(q_ref, k_ref, v_ref, lam_ref, g_ref, o_ref, qs_ref, m_ref, l_ref, acc_ref, *, t, lambda_init):
    qi = pl.program_id(2)
    lane = _lane_iota(t)
    qf = q_ref[...].astype(F32)
    qs_ref[0] = jnp.where(lane < HEAD_DIM, qf, 0.0).astype(BF16)
    qs_ref[1] = jnp.where(lane >= HEAD_DIM, qf, 0.0).astype(BF16)
    dbias = _causal_bias(t)
    state = (m_ref, l_ref, acc_ref)

    def step(j, first):
        ks = pl.multiple_of(j * t, t)
        k = k_ref[pl.ds(ks, t), :]
        v = v_ref[pl.ds(ks, t), :]
        for c in range(2):
            _flash_update(c, qs_ref[c], k, v, dbias if first else None, state, first)

    _causal_sweep(qi, step)
    lp = lam_ref[...]
    lam = (jnp.exp(jnp.sum(lp[0:1] * lp[1:2], -1, keepdims=True))
           - jnp.exp(jnp.sum(lp[2:3] * lp[3:4], -1, keepdims=True)) + lambda_init)
    o = acc_ref[0] / l_ref[0] - lam * (acc_ref[1] / l_ref[1])
    o_ref[...] = (_rms(o, g_ref[...], 1e-5) * (1.0 - lambda_init)).astype(BF16)


def _diff_attention(q, k, v, lam_p, g_subln, lambda_init, b, s):
    t = min(ATTN_TILE, s)
    nq = s // t
    nh = q.shape[1] // LANES
    return pl.pallas_call(
        functools.partial(_diff_kernel, t=t, lambda_init=lambda_init),
        grid=(b, nh, nq),
        in_specs=[pl.BlockSpec((t, LANES), lambda bi, h, qi: (bi * nq + qi, h)),
                  pl.BlockSpec((s, LANES), lambda bi, h, qi: (bi, h)),
                  pl.BlockSpec((s, LANES), lambda bi, h, qi: (bi, h)),
                  pl.BlockSpec(lam_p.shape, lambda bi, h, qi: (0, 0)),
                  pl.BlockSpec(g_subln.shape, lambda bi, h, qi: (0, 0))],
        out_specs=pl.BlockSpec((t, LANES), lambda bi, h, qi: (bi * nq + qi, h)),
        out_shape=jax.ShapeDtypeStruct((b * s, nh * LANES), BF16),
        scratch_shapes=[pltpu.VMEM((2, t, LANES), BF16)] + _flash_state(2, t),
        compiler_params=_cparams("parallel", "parallel", "arbitrary"),
        name="diff_attention",
    )(q, k, v, lam_p, g_subln)


def _swa_kernel(sink_ref, q_ref, k_ref, v_ref, o_ref, *, rows):
    ti = pl.program_id(1)
    w = SWA_WINDOW
    nsub = rows // w
    n_q = q_ref.shape[1] // HEAD_DIM
    n_kv = k_ref.shape[1] // HEAD_DIM
    grp = n_q // n_kv
    lane_k = lax.broadcasted_iota(I32, (2 * w, LANES), 1)
    lane_o = _lane_iota(w)
    qrow = lax.broadcasted_iota(I32, (grp * w, 2 * w), 0) % w
    kcol = lax.broadcasted_iota(I32, (grp * w, 2 * w), 1)
    for i in range(nsub):
        sb = ti * nsub + i
        start = pl.multiple_of(jnp.maximum(sb - 1, 0) * w, w)
        kb = k_ref[pl.ds(start, 2 * w), :]
        vb = v_ref[pl.ds(start, 2 * w), :]
        rel = qrow + (sb * w - start) - kcol
        bias = jnp.where((rel >= 0) & (rel < w), 0.0, NEG).astype(F32)
        qf = q_ref[i * w:(i + 1) * w, :].astype(F32)
        head_out = [None] * n_q
        for kvh in range(n_kv):
            km = jnp.where(lane_k // HEAD_DIM == kvh, kb.astype(F32), 0.0).astype(BF16)
            qs, sinks = [], []
            for a in range(grp):
                h = kvh * grp + a
                qp = qf[:, (h // 2) * LANES:(h // 2 + 1) * LANES]
                if h % 2 != kvh:
                    qp = pltpu.roll(qp, HEAD_DIM, 1)
                qs.append(qp.astype(BF16))
                sinks.append(jnp.full((w, 1), sink_ref[h], F32))
            sink = jnp.concatenate(sinks, 0)
            s = lax.dot_general(jnp.concatenate(qs, 0), km, _NT, preferred_element_type=F32) + bias
            m = jnp.maximum(jnp.max(s, -1, keepdims=True), sink)
            p = jnp.exp(s - m)
            den = jnp.sum(p, -1, keepdims=True) + jnp.exp(sink - m)
            o = jnp.dot(p.astype(BF16), vb, preferred_element_type=F32) / den
            for a in range(grp):
                h = kvh * grp + a
                oh = o[a * w:(a + 1) * w, :]
                if h % 2 != kvh:
                    oh = pltpu.roll(oh, HEAD_DIM, 1)
                head_out[h] = oh
        for hp in range(n_q // 2):
            pair = jnp.where(lane_o < HEAD_DIM, head_out[2 * hp], head_out[2 * hp + 1])
            o_ref[i * w:(i + 1) * w, hp * LANES:(hp + 1) * LANES] = pair.astype(BF16)


def _swa_attention(q, k, v, sinks, b, s):
    rows = min(SWA_ROWS, s)
    nt = s // rows
    return pl.pallas_call(
        functools.partial(_swa_kernel, rows=rows),
        grid=(b, nt),
        in_specs=[pl.BlockSpec(memory_space=pltpu.SMEM),
                  pl.BlockSpec((rows, q.shape[1]), lambda bi, ti: (bi * nt + ti, 0)),
                  pl.BlockSpec((s, k.shape[1]), lambda bi, ti: (bi, 0)),
                  pl.BlockSpec((s, v.shape[1]), lambda bi, ti: (bi, 0))],
        out_specs=pl.BlockSpec((rows, q.shape[1]), lambda bi, ti: (bi * nt + ti, 0)),
        out_shape=jax.ShapeDtypeStruct(q.shape, BF16),
        compiler_params=_cparams("parallel", "arbitrary"),
        name="swa_attention",
    )(sinks, q, k, v)


def _layer_norm(y, g, b):
    mu = jnp.mean(y, -1, keepdims=True)
    yc = y - mu
    var = jnp.mean(yc * yc, -1, keepdims=True)
    return yc * lax.rsqrt(var + 1e-5) * g + b


PIECES = 8


def _tm_load(ref, r0, nrows):
    return jnp.concatenate([ref[pl.ds(r0 * PIECES + p, nrows, stride=PIECES), :] for p in range(PIECES)], axis=1)


def _tm_store(ref, r0, val):
    for p in range(PIECES):
        ref[pl.ds(r0 * PIECES + p, val.shape[0], stride=PIECES), :] = val[:, p * LANES:(p + 1) * LANES]


def _out_ln_kernel(x_ref, a_ref, b_ref, w_ref, g_ref, beta_ref, o_ref, otm_ref, *, alpha):
    ka = a_ref.shape[1]
    mix = (jnp.dot(a_ref[...], w_ref[:ka, :], preferred_element_type=F32)
           + jnp.dot(b_ref[...], w_ref[ka:, :], preferred_element_type=F32))
    y = _layer_norm(alpha * x_ref[...] + mix, g_ref[...], beta_ref[...])
    o_ref[...] = y
    _tm_store(otm_ref, 0, y)


def _out_ln(xf, a, b, w, g, beta, alpha):
    n, d = xf.shape
    assert d == PIECES * LANES
    tm = min(PROJ_ROWS, n)
    row = lambda width: pl.BlockSpec((tm, width), lambda i: (i, 0))
    full = lambda arr: pl.BlockSpec(arr.shape, lambda i: (0, 0))
    return pl.pallas_call(
        functools.partial(_out_ln_kernel, alpha=alpha),
        grid=(n // tm,),
        in_specs=[row(d), row(a.shape[1]), row(b.shape[1]), full(w), full(g), full(beta)],
        out_specs=[row(d), pl.BlockSpec((tm * PIECES, LANES), lambda i: (i, 0))],
        out_shape=[jax.ShapeDtypeStruct((n, d), F32), jax.ShapeDtypeStruct((n * PIECES, LANES), F32)],
        compiler_params=_cparams("parallel"),
        name="out_proj_ln",
    )(xf, a, b, w, g, beta)


def _row_argmax(v, row, nrow):
    mx = jnp.max(v, 0, keepdims=True)
    idx = jnp.min(jnp.where(v == mx, row, nrow), 0, keepdims=True)
    return mx, idx


def _router_kernel(x_ref, w_ref, bias_ref, eidx_ref, wts_ref):
    logits = jnp.dot(x_ref[...], w_ref[...], preferred_element_type=F32, precision=lax.Precision.HIGHEST)
    tm = logits.shape[0]
    scores = jax.nn.sigmoid(logits.T[:N_EXPERTS, :])
    biased = scores + bias_ref[...]
    per = N_EXPERTS // N_GROUPS
    row_g = lax.broadcasted_iota(I32, (per, tm), 0)
    gsc = []
    for g in range(N_GROUPS):
        v = biased[g * per:(g + 1) * per, :]
        m1, i1 = _row_argmax(v, row_g, per)
        m2 = jnp.max(jnp.where(row_g == i1, -jnp.inf, v), 0, keepdims=True)
        gsc.append(m1 + m2)
    gscore = jnp.concatenate(gsc, 0)
    row_n = lax.broadcasted_iota(I32, (N_GROUPS, tm), 0)
    gok = jnp.zeros((N_GROUPS, tm), F32)
    for _ in range(TOPK_GROUPS):
        _, ig = _row_argmax(gscore, row_n, N_GROUPS)
        hit = row_n == ig
        gok = jnp.where(hit, 1.0, gok)
        gscore = jnp.where(hit, -jnp.inf, gscore)
    cand = jnp.concatenate([jnp.where(gok[g:g + 1, :] > 0.0, biased[g * per:(g + 1) * per, :], -jnp.inf)
                            for g in range(N_GROUPS)], 0)
    row_e = lax.broadcasted_iota(I32, (N_EXPERTS, tm), 0)
    ids, ws = [], []
    for _ in range(TOP_K):
        _, ie = _row_argmax(cand, row_e, N_EXPERTS)
        hit = row_e == ie
        ids.append(ie)
        ws.append(jnp.sum(jnp.where(hit, scores, 0.0), 0, keepdims=True))
        cand = jnp.where(hit, -jnp.inf, cand)
    w = jnp.concatenate(ws, 0)
    eidx_ref[...] = jnp.concatenate(ids, 0)
    wts_ref[...] = w / jnp.sum(w, 0, keepdims=True) * ROUTED_SCALE


def _router(xf, w_router, e_bias):
    n, d = xf.shape
    tm = min(ROUTER_ROWS, n)
    wpad = jnp.pad(w_router, ((0, 0), (0, LANES - N_EXPERTS)))
    bcol = e_bias.astype(F32).reshape(N_EXPERTS, 1)
    out = pl.BlockSpec((TOP_K, tm), lambda i: (0, i))
    return pl.pallas_call(
        _router_kernel,
        grid=(n // tm,),
        in_specs=[pl.BlockSpec((tm, d), lambda i: (i, 0)), pl.BlockSpec(wpad.shape, lambda i: (0, 0)),
                  pl.BlockSpec(bcol.shape, lambda i: (0, 0))],
        out_specs=[out, out],
        out_shape=[jax.ShapeDtypeStruct((TOP_K, n), I32), jax.ShapeDtypeStruct((TOP_K, n), F32)],
        compiler_params=_cparams("parallel"),
        name="router",
    )(xf, wpad, bcol)


def _dispatch(eidx_t, wts_t, chunk):
    n = eidx_t.shape[1]
    nc = n // chunk
    a = chunk * TOP_K
    ec = eidx_t.reshape(TOP_K, nc, chunk).transpose(1, 0, 2).reshape(nc, a)
    wc = wts_t.reshape(TOP_K, nc, chunk).transpose(1, 0, 2).reshape(nc, a)
    ex = jnp.arange(N_EXPERTS, dtype=I32)
    counts = jnp.sum((ec[:, :, None] == ex[None, None, :]).astype(I32), axis=1)
    ends = jnp.cumsum(counts, axis=1).astype(I32)
    starts = jnp.concatenate([jnp.zeros((nc, 1), I32), ends], axis=1)
    mult = 1 << int(chunk - 1).bit_length()
    tok_ids = jnp.broadcast_to(jnp.arange(a, dtype=I32)[None, :] % chunk, (nc, a))
    keys, gates = lax.sort((ec * mult + tok_ids, wc), dimension=1, num_keys=1)
    return starts.reshape(-1), (keys % mult).reshape(-1), gates.reshape(-1)


def _swiglu(xb, wg, wu):
    hg = jnp.dot(xb, wg, preferred_element_type=F32)
    hu = jnp.dot(xb, wu, preferred_element_type=F32)
    return hg * jax.nn.sigmoid(hg) * hu


def _moe_kernel(start_ref, tok_ref, gate_ref, x_ref, wg_ref, wu_ref, wd_ref, sg_ref, su_ref, sd_ref,
                g_ref, beta_ref, o_ref, acc_ref, xg_ref, og_ref, *, chunk, rows, alpha):
    c = pl.program_id(0)
    e = pl.program_id(1)
    n_e = pl.num_programs(1)
    group = 8
    sub = min(512, chunk)
    last = tok_ref.shape[0] - 1

    @pl.when((c == 0) & (e == 0))
    def _():
        xg_ref[...] = jnp.zeros_like(xg_ref)

    @pl.when(e == 0)
    def _():
        for s0 in range(0, chunk, sub):
            xb = _tm_load(x_ref, s0, sub).astype(BF16)
            h = _swiglu(xb, sg_ref[...], su_ref[...]).astype(BF16)
            _tm_store(acc_ref, s0, jnp.dot(h, sd_ref[...], preferred_element_type=F32))
        acc_ref[chunk * PIECES:(chunk + 1) * PIECES, :] = jnp.zeros((PIECES, LANES), F32)

    lo = start_ref[c * (n_e + 1) + e]
    hi = start_ref[c * (n_e + 1) + e + 1]

    def block(bi, carry):
        base = lo + bi * rows
        nrow = jnp.minimum(hi - base, rows)
        nfull = nrow // group
        ragged = nrow % group != 0

        def gather(gi, partial):
            r0 = pl.multiple_of(gi * group, group)
            for i in range(group):
                idx = jnp.minimum(base + r0 + i, last) if partial else base + r0 + i
                src = pl.multiple_of(tok_ref[idx] * PIECES, PIECES)
                dst = pl.multiple_of((r0 + i) * PIECES, PIECES)
                xg_ref[pl.ds(dst, PIECES), :] = x_ref[pl.ds(src, PIECES), :]

        def scatter(gi, partial):
            r0 = pl.multiple_of(gi * group, group)
            dsts, vals = [], []
            for i in range(group):
                idx = jnp.minimum(base + r0 + i, last) if partial else base + r0 + i
                t = tok_ref[idx]
                if partial:
                    t = jnp.where(r0 + i < nrow, t, chunk)
                dsts.append(pl.multiple_of(t * PIECES, PIECES))
                row = og_ref[pl.ds(pl.multiple_of((r0 + i) * PIECES, PIECES), PIECES), :]
                vals.append(acc_ref[pl.ds(dsts[i], PIECES), :] + gate_ref[idx] * row)
            for i in range(group):
                acc_ref[pl.ds(dsts[i], PIECES), :] = vals[i]

        def sweep(fn):
            def body(gi, carry):
                fn(gi, False)
                return carry

            lax.fori_loop(0, nfull, body, 0)
            pl.when(ragged)(lambda: fn(nfull, True))

        sweep(gather)
        xb = _tm_load(xg_ref, 0, rows).astype(BF16)
        h = _swiglu(xb, wg_ref[0], wu_ref[0]).astype(BF16)
        _tm_store(og_ref, 0, jnp.dot(h, wd_ref[0], preferred_element_type=F32))
        sweep(scatter)
        return carry

    lax.fori_loop(0, (hi - lo + rows - 1) // rows, block, 0)

    @pl.when(e == n_e - 1)
    def _():
        for s0 in range(0, chunk, sub):
            z = alpha * _tm_load(x_ref, s0, sub) + _tm_load(acc_ref, s0, sub)
            o_ref[s0:s0 + sub, :] = _layer_norm(z, g_ref[...], beta_ref[...])


def _moe_ln(x_tm, eidx, wts, wg, wu, wd, sg, su, sd, g, beta, alpha):
    n = x_tm.shape[0] // PIECES
    d = PIECES * LANES
    chunk = min(MOE_CHUNK, n)
    rows = MOE_ROWS
    starts, tok, gate = _dispatch(eidx, wts, chunk)
    nc = n // chunk
    a = chunk * TOP_K
    n_e, _, de = wg.shape
    full = lambda arr: pl.BlockSpec(arr.shape, lambda c, e, s_r: (0,) * arr.ndim)
    lists = pl.BlockSpec((a,), lambda c, e, s_r: (c,), memory_space=pltpu.SMEM)
    grid_spec = pltpu.PrefetchScalarGridSpec(
        num_scalar_prefetch=1,
        grid=(nc, n_e),
        in_specs=[
            lists, lists,
            pl.BlockSpec((chunk * PIECES, LANES), lambda c, e, s_r: (c, 0)),
            pl.BlockSpec((1, d, de), lambda c, e, s_r: (e, 0, 0)),
            pl.BlockSpec((1, d, de), lambda c, e, s_r: (e, 0, 0)),
            pl.BlockSpec((1, de, d), lambda c, e, s_r: (e, 0, 0)),
            full(sg), full(su), full(sd), full(g), full(beta),
        ],
        out_specs=pl.BlockSpec((chunk, d), lambda c, e, s_r: (c, 0)),
        scratch_shapes=[pltpu.VMEM(((chunk + 1) * PIECES, LANES), F32), pltpu.VMEM((rows * PIECES, LANES), F32),
                        pltpu.VMEM((rows * PIECES, LANES), F32)],
    )
    return pl.pallas_call(
        functools.partial(_moe_kernel, chunk=chunk, rows=rows, alpha=alpha),
        grid_spec=grid_spec,
        out_shape=jax.ShapeDtypeStruct((n, d), F32),
        compiler_params=_cparams("arbitrary", "arbitrary"),
        name="moe_experts",
    )(starts, tok, gate, x_tm, wg, wu, wd, sg, su, sd, g, beta)


def _rope_tables(positions):
    pos = positions.reshape(-1).astype(F32)[:, None]
    n = pos.shape[0]

    def cs(dim):
        inv = ROPE_THETA ** (-jnp.arange(0, dim, 2, dtype=F32) / dim)
        ang = pos * inv
        return jnp.cos(ang), jnp.sin(ang)

    c, s = cs(HEAD_DIM)
    c64 = jnp.concatenate([c, c, c, c], -1)
    s64 = jnp.concatenate([-s, s, -s, s], -1)
    c, s = cs(MLA_ROPE)
    one = jnp.ones((n, MLA_NOPE), F32)
    pad = LANES - MLA_NOPE - MLA_ROPE
    cm = jnp.concatenate([one, c, c, jnp.ones((n, pad), F32)], -1)
    sm = jnp.concatenate([0 * one, -s, s, jnp.zeros((n, pad), F32)], -1)
    return (c64, s64), (cm, sm)


def _even_mixer(xf, b, s, tabs, w_in, g_qa, w_qb, g_kva, w_kvb):
    d = xf.shape[1]
    tab64, tabm = tabs
    na = 3 * 512
    kpe_lo = na + 256 + 128
    wpad = jnp.concatenate([w_in[:, :kpe_lo], jnp.zeros((d, MLA_NOPE), F32), w_in[:, kpe_lo:],
                            jnp.zeros((d, LANES - MLA_NOPE - MLA_ROPE), F32)], 1).astype(BF16)
    nh = w_qb.shape[1] // (MLA_NOPE + MLA_ROPE)
    wqb = jnp.pad(w_qb.reshape(-1, nh, MLA_NOPE + MLA_ROPE), ((0, 0), (0, 0), (0, LANES - MLA_NOPE - MLA_ROPE)))
    wqb = wqb.reshape(-1, nh * LANES).astype(BF16)
    wkv3 = w_kvb.reshape(-1, nh, MLA_NOPE + MLA_V)
    wkk = jnp.pad(wkv3[:, :, :MLA_NOPE], ((0, 0), (0, 0), (0, LANES - MLA_NOPE))).reshape(-1, nh * LANES).astype(BF16)
    wkv = wkv3[:, :, MLA_NOPE:].reshape(-1, nh * MLA_V).astype(BF16)
    segs = [(0, 512, True, QK_SCALE), (512, 1024, True, 1.0), (1024, 1536, False, 1.0)]
    qa, ka, va, qm, km, vm = _projection(
        xf, wpad, segs, tab64,
        mla_args=(tabm[0], tabm[1], g_qa.reshape(1, -1), wqb, g_kva.reshape(1, -1), wkk, wkv),
        mla_out_widths=(nh * LANES, nh * LANES, nh * MLA_V))
    o_a = _moba_attention(qa, ka, va, b, s)
    o_b = _mla_attention(qm, km, vm, b, s)
    return o_a, o_b


def _odd_mixer(xf, b, s, tabs, w_in, sinks, lam_p, g_subln, lambda_init):
    segs = [(0, 512, True, QK_SCALE), (512, 640, True, 1.0), (640, 768, False, 1.0),
            (768, 1280, True, QK_SCALE), (1280, 1792, True, 1.0), (1792, 2304, False, 1.0)]
    qc, kc, vc, qd, kd, vd = _projection(xf, w_in.astype(BF16), segs, tabs[0])
    o_c = _swa_attention(qc, kc, vc, sinks.astype(F32), b, s)
    o_d = _diff_attention(qd, kd, vd, lam_p.astype(F32), g_subln.reshape(1, -1), lambda_init, b, s)
    return o_c, o_d


def kernel(x, positions, ev_w_in, ev_w_out, mla_g_qa, mla_w_qb, mla_g_kva, mla_w_kvb, od_w_in, od_w_out, swa_sinks, diff_lambda, diff_g_subln, ln_g, ln_b, moe_w_router, moe_bias, moe_w_gate, moe_w_up, moe_w_down, sh_w_gate, sh_w_up, sh_w_down):
    b, s, d = x.shape
    depth = ln_g.shape[0]
    alpha = float((2 * depth) ** 0.25)
    xf = x.reshape(b * s, d)
    tabs = _rope_tables(positions)
    for l in range(depth):
        i = l // 2
        if l % 2 == 0:
            o1, o2 = _even_mixer(xf, b, s, tabs, ev_w_in[i], mla_g_qa[i], mla_w_qb[i], mla_g_kva[i], mla_w_kvb[i])
            w_out = ev_w_out[i]
        else:
            lambda_init = 0.8 - 0.6 * float(np.exp(-0.3 * l))
            o1, o2 = _odd_mixer(xf, b, s, tabs, od_w_in[i], swa_sinks[i], diff_lambda[i], diff_g_subln[i], lambda_init)
            w_out = od_w_out[i]
        xf, x_tm = _out_ln(xf, o1, o2, w_out.astype(BF16), ln_g[l, 0].reshape(1, d), ln_b[l, 0].reshape(1, d), alpha)
        eidx, wts = _router(xf, moe_w_router[l], moe_bias[l])
        xf = _moe_ln(x_tm, eidx, wts, moe_w_gate[l].astype(BF16), moe_w_up[l].astype(BF16),
                     moe_w_down[l].astype(BF16), sh_w_gate[l].astype(BF16), sh_w_up[l].astype(BF16),
                     sh_w_down[l].astype(BF16), ln_g[l, 1].reshape(1, d), ln_b[l, 1].reshape(1, d), alpha)
    return xf.reshape(b, s, d)
```

```python
import functools

import numpy as np
import jax
import jax.numpy as jnp
from jax import lax
from jax.experimental import pallas as pl
from jax.experimental.pallas import tpu as pltpu

F32 = jnp.float32
BF16 = jnp.bfloat16
I32 = jnp.int32

LANES = 128
NEG = -1e30

HEAD_DIM = 64
MOBA_BLOCK = 256
MOBA_TOPK = 3
MLA_NOPE = 64
MLA_ROPE = 32
MLA_V = 64
SWA_WINDOW = 128
N_EXPERTS = 64
N_GROUPS = 8
TOPK_GROUPS = 4
TOP_K = 8
ROUTED_SCALE = 2.5
ROPE_THETA = 10000.0
QK_SCALE = HEAD_DIM ** -0.5
MLA_SCALE = (MLA_NOPE + MLA_ROPE) ** -0.5

PROJ_ROWS = 512
ATTN_TILE = 1024
SWA_ROWS = 512
ROUTER_ROWS = 512
MOE_CHUNK = 2048
MOE_ROWS = 320
VMEM_LIMIT = 56 * 1024 * 1024

_NT = (((1,), (1,)), ((), ()))


def _cparams(*sem):
    return pltpu.CompilerParams(dimension_semantics=sem, vmem_limit_bytes=VMEM_LIMIT)


def _lane_iota(rows):
    return lax.broadcasted_iota(I32, (rows, LANES), 1)


def _rope(x, c, s, first_half, half):
    rot = jnp.where(first_half, pltpu.roll(x, LANES - half, 1), pltpu.roll(x, half, 1))
    return x * c + rot * s


def _rms(x, g, eps):
    return x * lax.rsqrt(jnp.mean(x * x, -1, keepdims=True) + eps) * g


def _proj_kernel(segs, mla, x_ref, w_ref, c64_ref, s64_ref, *refs):
    xb = x_ref[...].astype(BF16)
    rows = xb.shape[0]
    lane = _lane_iota(rows)
    fh64 = (lane % HEAD_DIM) < HEAD_DIM // 2
    c64 = c64_ref[...]
    s64 = s64_ref[...]

    def seg(lo, hi):
        return jnp.dot(xb, w_ref[:, lo:hi], preferred_element_type=F32)

    if mla:
        cm_ref, sm_ref, gq_ref, wqb_ref, gkv_ref, wkk_ref, wkv_ref = refs[:7]
        refs = refs[7:]
    for (lo, hi, rope, scale), o_ref in zip(segs, refs):
        h = seg(lo, hi)
        if rope:
            for g in range((hi - lo) // LANES):
                sl = slice(g * LANES, (g + 1) * LANES)
                o_ref[:, sl] = (_rope(h[:, sl], c64, s64, fh64, HEAD_DIM // 2) * scale).astype(BF16)
        else:
            o_ref[...] = h.astype(BF16)
    if mla:
        qm_ref, km_ref, vm_ref = refs[len(segs):]
        cm = cm_ref[...]
        sm = sm_ref[...]
        fhm = (lane >= MLA_NOPE) & (lane < MLA_NOPE + MLA_ROPE // 2)
        lo = segs[-1][1]
        qn = _rms(seg(lo, lo + 256), gq_ref[...], 1e-6).astype(BF16)
        qm = jnp.dot(qn, wqb_ref[...], preferred_element_type=F32)
        kvn = _rms(seg(lo + 256, lo + 384), gkv_ref[...], 1e-6).astype(BF16)
        kk = jnp.dot(kvn, wkk_ref[...], preferred_element_type=F32)
        kpe = _rope(seg(lo + 384, lo + 512), cm, sm, fhm, MLA_ROPE // 2)
        for g in range(qm.shape[1] // LANES):
            sl = slice(g * LANES, (g + 1) * LANES)
            qm_ref[:, sl] = (_rope(qm[:, sl], cm, sm, fhm, MLA_ROPE // 2) * MLA_SCALE).astype(BF16)
            km_ref[:, sl] = (kk[:, sl] + kpe).astype(BF16)
        vm_ref[...] = jnp.dot(kvn, wkv_ref[...], preferred_element_type=F32).astype(BF16)


def _projection(xf, w, segs, tabs, mla_args=None, mla_out_widths=()):
    n, d = xf.shape
    tm = min(PROJ_ROWS, n)
    row = lambda width: pl.BlockSpec((tm, width), lambda i: (i, 0))
    full = lambda a: pl.BlockSpec(a.shape, lambda i: (0,) * a.ndim)
    ins = [xf, w, tabs[0], tabs[1]]
    in_specs = [row(d), full(w), row(LANES), row(LANES)]
    if mla_args is not None:
        cm, sm, gq, wqb, gkv, wkk, wkv = mla_args
        ins += [cm, sm, gq, wqb, gkv, wkk, wkv]
        in_specs += [row(LANES), row(LANES), full(gq), full(wqb), full(gkv), full(wkk), full(wkv)]
    widths = [hi - lo for lo, hi, _, _ in segs] + list(mla_out_widths)
    return pl.pallas_call(
        functools.partial(_proj_kernel, tuple(segs), mla_args is not None),
        grid=(n // tm,),
        in_specs=in_specs,
        out_specs=[row(wd) for wd in widths],
        out_shape=[jax.ShapeDtypeStruct((n, wd), BF16) for wd in widths],
        compiler_params=_cparams("parallel"),
        name="projection",
    )(*ins)


def _flash_update(c, q, k, v, bias, state, first):
    m_ref, l_ref, acc_ref = state
    s = lax.dot_general(q, k, _NT, preferred_element_type=F32)
    if bias is not None:
        s = s + bias
    t, tk = s.shape
    mx = jnp.max(s, -1, keepdims=True)
    if first:
        m_new = jnp.broadcast_to(mx, (t, LANES))
    else:
        m_prev = m_ref[c]
        m_new = jnp.maximum(m_prev, mx)
        alpha = jnp.exp(m_prev - m_new)
    ps, psum = [], None
    for j in range(tk // LANES):
        pj = jnp.exp(s[:, j * LANES:(j + 1) * LANES] - m_new)
        ps.append(pj.astype(BF16))
        psum = pj if psum is None else psum + pj
    pv = jnp.dot(jnp.concatenate(ps, axis=1), v, preferred_element_type=F32)
    if first:
        l_ref[c] = psum
        acc_ref[c] = pv
    else:
        l_ref[c] = alpha * l_ref[c] + psum
        acc_ref[c] = alpha * acc_ref[c] + pv
    m_ref[c] = m_new


def _flash_out(c, state):
    _, l_ref, acc_ref = state
    return acc_ref[c] / jnp.sum(l_ref[c], -1, keepdims=True)


def _flash_state(chains, t):
    return [pltpu.VMEM((chains, t, LANES), F32) for _ in range(3)]


def _causal_sweep(qi, step):
    step(qi, True)

    def body(j, carry):
        step(j, False)
        return carry

    lax.fori_loop(0, qi, body, 0)


def _causal_bias(t, block=None):
    r = lax.broadcasted_iota(I32, (t, t), 0)
    c = lax.broadcasted_iota(I32, (t, t), 1)
    hide = c > r
    if block is not None:
        hide = hide & ((r // block) == (c // block))
    return jnp.where(hide, NEG, 0.0).astype(F32)


def _mla_kernel(q_ref, k_ref, v_ref, o_ref, m_ref, l_ref, acc_ref, *, t):
    qi = pl.program_id(2)
    dbias = _causal_bias(t)
    state = (m_ref, l_ref, acc_ref)

    def step(j, first):
        ks = pl.multiple_of(j * t, t)
        v = v_ref[pl.ds(ks, t), :]
        for h in range(2):
            hs = slice(h * LANES, (h + 1) * LANES)
            _flash_update(h, q_ref[:, hs], k_ref[pl.ds(ks, t), hs], v, dbias if first else None, state, first)

    _causal_sweep(qi, step)
    lane = _lane_iota(t)
    o_ref[...] = jnp.where(lane < MLA_V, _flash_out(0, state), _flash_out(1, state)).astype(BF16)


def _mla_attention(qm, km, vm, b, s):
    t = min(ATTN_TILE, s)
    nq = s // t
    hp = qm.shape[1] // (2 * LANES)
    return pl.pallas_call(
        functools.partial(_mla_kernel, t=t),
        grid=(b, hp, nq),
        in_specs=[pl.BlockSpec((t, 2 * LANES), lambda bi, h, qi: (bi * nq + qi, h)),
                  pl.BlockSpec((s, 2 * LANES), lambda bi, h, qi: (bi, h)),
                  pl.BlockSpec((s, LANES), lambda bi, h, qi: (bi, h))],
        out_specs=pl.BlockSpec((t, LANES), lambda bi, h, qi: (bi * nq + qi, h)),
        out_shape=jax.ShapeDtypeStruct((b * s, hp * LANES), BF16),
        scratch_shapes=_flash_state(2, t),
        compiler_params=_cparams("parallel", "parallel", "arbitrary"),
        name="mla_attention",
    )(qm, km, vm)


def _moba_kernel(q_ref, k_ref, v_ref, o_ref, kmean_ref, kaug_ref, qaug_ref, m_ref, l_ref, acc_ref, *, t, nb):
    qi = pl.program_id(2)
    bs = MOBA_BLOCK
    s_len = k_ref.shape[0]
    half = LANES // 2

    @pl.when(qi == 0)
    def _():
        kmean_ref[...] = jnp.zeros_like(kmean_ref)
        for n in range(nb):
            blk = k_ref[n * bs:(n + 1) * bs, :].astype(F32)
            kmean_ref[n:n + 1, :] = jnp.sum(blk, 0, keepdims=True) * (1.0 / bs)
        lane_s = _lane_iota(s_len)
        row_blk = lax.broadcasted_iota(I32, (s_len, LANES), 0) // bs
        kf = k_ref[...].astype(F32)
        kaug_ref[0] = jnp.where(lane_s < half, kf, (lane_s - half == row_blk).astype(F32)).astype(BF16)
        kaug_ref[1] = jnp.where(lane_s >= half, kf, (lane_s == row_blk).astype(F32)).astype(BF16)

    lane = _lane_iota(t)
    own = qi * (t // bs) + lax.broadcasted_iota(I32, (t, LANES), 0) // bs
    qf = q_ref[...].astype(F32)
    kmean = kmean_ref[...]
    for h in range(2):
        head = (lane >= h * half) & (lane < (h + 1) * half)
        qh = jnp.where(head, qf, 0.0)
        gate = lax.dot_general(qh, kmean, _NT, preferred_element_type=F32, precision=lax.Precision.HIGHEST)
        g = jnp.where(lane < own, gate, -jnp.inf)
        sel = lane == own
        for _ in range(MOBA_TOPK):
            mx = jnp.max(g, -1, keepdims=True)
            idx = jnp.min(jnp.where(g == mx, lane, LANES), -1, keepdims=True)
            hit = lane == idx
            sel = sel | (hit & (lane < own))
            g = jnp.where(hit, -jnp.inf, g)
        selb = jnp.where(sel | (lane >= nb), 0.0, NEG).astype(F32)
        if h == 0:
            selb = pltpu.roll(selb, half, 1)
        qaug_ref[h] = jnp.where(head, qf, selb).astype(BF16)

    dbias = _causal_bias(t, bs)
    state = (m_ref, l_ref, acc_ref)

    def step(j, first):
        ks = pl.multiple_of(j * t, t)
        v = v_ref[pl.ds(ks, t), :]
        for h in range(2):
            _flash_update(h, qaug_ref[h], kaug_ref[h, pl.ds(ks, t), :], v, dbias if first else None, state, first)

    _causal_sweep(qi, step)
    o_ref[...] = jnp.where(lane < half, _flash_out(0, state), _flash_out(1, state)).astype(BF16)


def _moba_attention(q, k, v, b, s):
    bs = MOBA_BLOCK
    nb = s // bs
    assert nb <= LANES // 2, "block gate must fit the spare half of the lanes"
    t = min(ATTN_TILE, s)
    nq = s // t
    hp = q.shape[1] // LANES
    return pl.pallas_call(
        functools.partial(_moba_kernel, t=t, nb=nb),
        grid=(b, hp, nq),
        in_specs=[pl.BlockSpec((t, LANES), lambda bi, h, qi: (bi * nq + qi, h)),
                  pl.BlockSpec((s, LANES), lambda bi, h, qi: (bi, h)),
                  pl.BlockSpec((s, LANES), lambda bi, h, qi: (bi, h))],
        out_specs=pl.BlockSpec((t, LANES), lambda bi, h, qi: (bi * nq + qi, h)),
        out_shape=jax.ShapeDtypeStruct((b * s, hp * LANES), BF16),
        scratch_shapes=[pltpu.VMEM((LANES, LANES), F32), pltpu.VMEM((2, s, LANES), BF16),
                        pltpu.VMEM((2, t, LANES), BF16)] + _flash_state(2, t),
        compiler_params=_cparams("parallel", "parallel", "arbitrary"),
        name="moba_attention",
    )(q, k, v)


def _diff_kernel(q_ref, k_ref, v_ref, lam_ref, g_ref, o_ref, qs_ref, m_ref, l_ref, acc_ref, *, t, lambda_init):
    qi = pl.program_id(2)
    lane = _lane_iota(t)
    qf = q_ref[...].astype(F32)
    qs_ref[0] = jnp.where(lane < HEAD_DIM, qf, 0.0).astype(BF16)
    qs_ref[1] = jnp.where(lane >= HEAD_DIM, qf, 0.0).astype(BF16)
    dbias = _causal_bias(t)
    state = (m_ref, l_ref, acc_ref)

    def step(j, first):
        ks = pl.multiple_of(j * t, t)
        k = k_ref[pl.ds(ks, t), :]
        v = v_ref[pl.ds(ks, t), :]
        for c in range(2):
            _flash_update(c, qs_ref[c], k, v, dbias if first else None, state, first)

    _causal_sweep(qi, step)
    lp = lam_ref[...]
    lam = (jnp.exp(jnp.sum(lp[0:1] * lp[1:2], -1, keepdims=True))
           - jnp.exp(jnp.sum(lp[2:3] * lp[3:4], -1, keepdims=True)) + lambda_init)
    o = _flash_out(0, state) - lam * _flash_out(1, state)
    o_ref[...] = (_rms(o, g_ref[...], 1e-5) * (1.0 - lambda_init)).astype(BF16)


def _diff_attention(q, k, v, lam_p, g_subln, lambda_init, b, s):
    t = min(ATTN_TILE, s)
    nq = s // t
    nh = q.shape[1] // LANES
    return pl.pallas_call(
        functools.partial(_diff_kernel, t=t, lambda_init=lambda_init),
        grid=(b, nh, nq),
        in_specs=[pl.BlockSpec((t, LANES), lambda bi, h, qi: (bi * nq + qi, h)),
                  pl.BlockSpec((s, LANES), lambda bi, h, qi: (bi, h)),
                  pl.BlockSpec((s, LANES), lambda bi, h, qi: (bi, h)),
                  pl.BlockSpec(lam_p.shape, lambda bi, h, qi: (0, 0)),
                  pl.BlockSpec(g_subln.shape, lambda bi, h, qi: (0, 0))],
        out_specs=pl.BlockSpec((t, LANES), lambda bi, h, qi: (bi * nq + qi, h)),
        out_shape=jax.ShapeDtypeStruct((b * s, nh * LANES), BF16),
        scratch_shapes=[pltpu.VMEM((2, t, LANES), BF16)] + _flash_state(2, t),
        compiler_params=_cparams("parallel", "parallel", "arbitrary"),
        name="diff_attention",
    )(q, k, v, lam_p, g_subln)


def _swa_kernel(sink_ref, q_ref, k_ref, v_ref, o_ref, *, rows):
    ti = pl.program_id(1)
    w = SWA_WINDOW
    nsub = rows // w
    n_q = q_ref.shape[1] // HEAD_DIM
    n_kv = k_ref.shape[1] // HEAD_DIM
    grp = n_q // n_kv
    lane_k = lax.broadcasted_iota(I32, (2 * w, LANES), 1)
    lane_o = _lane_iota(w)
    qrow = lax.broadcasted_iota(I32, (grp * w, 2 * w), 0) % w
    kcol = lax.broadcasted_iota(I32, (grp * w, 2 * w), 1)
    for i in range(nsub):
        sb = ti * nsub + i
        start = pl.multiple_of(jnp.maximum(sb - 1, 0) * w, w)
        kb = k_ref[pl.ds(start, 2 * w), :]
        vb = v_ref[pl.ds(start, 2 * w), :]
        rel = qrow + (sb * w - start) - kcol
        bias = jnp.where((rel >= 0) & (rel < w), 0.0, NEG).astype(F32)
        qf = q_ref[i * w:(i + 1) * w, :].astype(F32)
        head_out = [None] * n_q
        for kvh in range(n_kv):
            km = jnp.where(lane_k // HEAD_DIM == kvh, kb.astype(F32), 0.0).astype(BF16)
            qs, sinks = [], []
            for a in range(grp):
                h = kvh * grp + a
                qp = qf[:, (h // 2) * LANES:(h // 2 + 1) * LANES]
                if h % 2 != kvh:
                    qp = pltpu.roll(qp, HEAD_DIM, 1)
                qs.append(qp.astype(BF16))
                sinks.append(jnp.full((w, 1), sink_ref[h], F32))
            sink = jnp.concatenate(sinks, 0)
            s = lax.dot_general(jnp.concatenate(qs, 0), km, _NT, preferred_element_type=F32) + bias
            m = jnp.maximum(jnp.max(s, -1, keepdims=True), sink)
            p = jnp.exp(s - m)
            den = jnp.sum(p, -1, keepdims=True) + jnp.exp(sink - m)
            o = jnp.dot(p.astype(BF16), vb, preferred_element_type=F32) / den
            for a in range(grp):
                h = kvh * grp + a
                oh = o[a * w:(a + 1) * w, :]
                if h % 2 != kvh:
                    oh = pltpu.roll(oh, HEAD_DIM, 1)
                head_out[h] = oh
        for hp in range(n_q // 2):
            pair = jnp.where(lane_o < HEAD_DIM, head_out[2 * hp], head_out[2 * hp + 1])
            o_ref[i * w:(i + 1) * w, hp * LANES:(hp + 1) * LANES] = pair.astype(BF16)


def _swa_attention(q, k, v, sinks, b, s):
    rows = min(SWA_ROWS, s)
    nt = s // rows
    return pl.pallas_call(
        functools.partial(_swa_kernel, rows=rows),
        grid=(b, nt),
        in_specs=[pl.BlockSpec(memory_space=pltpu.SMEM),
                  pl.BlockSpec((rows, q.shape[1]), lambda bi, ti: (bi * nt + ti, 0)),
                  pl.BlockSpec((s, k.shape[1]), lambda bi, ti: (bi, 0)),
                  pl.BlockSpec((s, v.shape[1]), lambda bi, ti: (bi, 0))],
        out_specs=pl.BlockSpec((rows, q.shape[1]), lambda bi, ti: (bi * nt + ti, 0)),
        out_shape=jax.ShapeDtypeStruct(q.shape, BF16),
        compiler_params=_cparams("parallel", "arbitrary"),
        name="swa_attention",
    )(sinks, q, k, v)


def _layer_norm(y, g, b):
    mu = jnp.mean(y, -1, keepdims=True)
    yc = y - mu
    var = jnp.mean(yc * yc, -1, keepdims=True)
    return yc * lax.rsqrt(var + 1e-5) * g + b


PIECES = 8


def _tm_load(ref, r0, nrows):
    return jnp.concatenate([ref[pl.ds(r0 * PIECES + p, nrows, stride=PIECES), :] for p in range(PIECES)], axis=1)


def _tm_store(ref, r0, val):
    for p in range(PIECES):
        ref[pl.ds(r0 * PIECES + p, val.shape[0], stride=PIECES), :] = val[:, p * LANES:(p + 1) * LANES]


def _out_ln_kernel(x_ref, a_ref, b_ref, w_ref, g_ref, beta_ref, o_ref, otm_ref, *, alpha):
    ka = a_ref.shape[1]
    mix = (jnp.dot(a_ref[...], w_ref[:ka, :], preferred_element_type=F32)
           + jnp.dot(b_ref[...], w_ref[ka:, :], preferred_element_type=F32))
    y = _layer_norm(alpha * x_ref[...] + mix, g_ref[...], beta_ref[...])
    o_ref[...] = y
    _tm_store(otm_ref, 0, y)


def _out_ln(xf, a, b, w, g, beta, alpha):
    n, d = xf.shape
    assert d == PIECES * LANES
    tm = min(PROJ_ROWS, n)
    row = lambda width: pl.BlockSpec((tm, width), lambda i: (i, 0))
    full = lambda arr: pl.BlockSpec(arr.shape, lambda i: (0, 0))
    return pl.pallas_call(
        functools.partial(_out_ln_kernel, alpha=alpha),
        grid=(n // tm,),
        in_specs=[row(d), row(a.shape[1]), row(b.shape[1]), full(w), full(g), full(beta)],
        out_specs=[row(d), pl.BlockSpec((tm * PIECES, LANES), lambda i: (i, 0))],
        out_shape=[jax.ShapeDtypeStruct((n, d), F32), jax.ShapeDtypeStruct((n * PIECES, LANES), F32)],
        compiler_params=_cparams("parallel"),
        name="out_proj_ln",
    )(xf, a, b, w, g, beta)


def _row_argmax(v, row, nrow):
    mx = jnp.max(v, 0, keepdims=True)
    idx = jnp.min(jnp.where(v == mx, row, nrow), 0, keepdims=True)
    return mx, idx


def _router_kernel(x_ref, w_ref, bias_ref, eidx_ref, wts_ref):
    logits = jnp.dot(x_ref[...], w_ref[...], preferred_element_type=F32, precision=lax.Precision.HIGHEST)
    tm = logits.shape[0]
    scores = jax.nn.sigmoid(logits.T[:N_EXPERTS, :])
    biased = scores + bias_ref[...]
    per = N_EXPERTS // N_GROUPS
    row_g = lax.broadcasted_iota(I32, (per, tm), 0)
    gsc = []
    for g in range(N_GROUPS):
        v = biased[g * per:(g + 1) * per, :]
        m1, i1 = _row_argmax(v, row_g, per)
        m2 = jnp.max(jnp.where(row_g == i1, -jnp.inf, v), 0, keepdims=True)
        gsc.append(m1 + m2)
    gscore = jnp.concatenate(gsc, 0)
    row_n = lax.broadcasted_iota(I32, (N_GROUPS, tm), 0)
    gok = jnp.zeros((N_GROUPS, tm), F32)
    for _ in range(TOPK_GROUPS):
        _, ig = _row_argmax(gscore, row_n, N_GROUPS)
        hit = row_n == ig
        gok = jnp.where(hit, 1.0, gok)
        gscore = jnp.where(hit, -jnp.inf, gscore)
    cand = jnp.concatenate([jnp.where(gok[g:g + 1, :] > 0.0, biased[g * per:(g + 1) * per, :], -jnp.inf)
                            for g in range(N_GROUPS)], 0)
    row_e = lax.broadcasted_iota(I32, (N_EXPERTS, tm), 0)
    ids, ws = [], []
    for _ in range(TOP_K):
        _, ie = _row_argmax(cand, row_e, N_EXPERTS)
        hit = row_e == ie
        ids.append(ie)
        ws.append(jnp.sum(jnp.where(hit, scores, 0.0), 0, keepdims=True))
        cand = jnp.where(hit, -jnp.inf, cand)
    w = jnp.concatenate(ws, 0)
    eidx_ref[...] = jnp.concatenate(ids, 0)
    wts_ref[...] = w / jnp.sum(w, 0, keepdims=True) * ROUTED_SCALE


def _router(xf, w_router, e_bias):
    n, d = xf.shape
    tm = min(ROUTER_ROWS, n)
    wpad = jnp.pad(w_router, ((0, 0), (0, LANES - N_EXPERTS)))
    bcol = e_bias.astype(F32).reshape(N_EXPERTS, 1)
    out = pl.BlockSpec((TOP_K, tm), lambda i: (0, i))
    return pl.pallas_call(
        _router_kernel,
        grid=(n // tm,),
        in_specs=[pl.BlockSpec((tm, d), lambda i: (i, 0)), pl.BlockSpec(wpad.shape, lambda i: (0, 0)),
                  pl.BlockSpec(bcol.shape, lambda i: (0, 0))],
        out_specs=[out, out],
        out_shape=[jax.ShapeDtypeStruct((TOP_K, n), I32), jax.ShapeDtypeStruct((TOP_K, n), F32)],
        compiler_params=_cparams("parallel"),
        name="router",
    )(xf, wpad, bcol)


def _dispatch(eidx_t, wts_t, chunk):
    n = eidx_t.shape[1]
    nc = n // chunk
    a = chunk * TOP_K
    ec = eidx_t.reshape(TOP_K, nc, chunk).transpose(1, 0, 2).reshape(nc, a)
    wc = wts_t.reshape(TOP_K, nc, chunk).transpose(1, 0, 2).reshape(nc, a)
    ex = jnp.arange(N_EXPERTS, dtype=I32)
    counts = jnp.sum((ec[:, :, None] == ex[None, None, :]).astype(I32), axis=1)
    ends = jnp.cumsum(counts, axis=1).astype(I32)
    starts = jnp.concatenate([jnp.zeros((nc, 1), I32), ends], axis=1)
    mult = 1 << int(chunk - 1).bit_length()
    tok_ids = jnp.broadcast_to(jnp.arange(a, dtype=I32)[None, :] % chunk, (nc, a))
    keys, gates = lax.sort((ec * mult + tok_ids, wc), dimension=1, num_keys=1)
    return starts.reshape(-1), (keys % mult).reshape(-1), gates.reshape(-1)


def _swiglu(xb, wg, wu):
    hg = jnp.dot(xb, wg, preferred_element_type=F32)
    hu = jnp.dot(xb, wu, preferred_element_type=F32)
    return hg * jax.nn.sigmoid(hg) * hu


def _moe_kernel(start_ref, tok_ref, gate_ref, x_ref, wg_ref, wu_ref, wd_ref, sg_ref, su_ref, sd_ref,
                g_ref, beta_ref, o_ref, acc_ref, xg_ref, og_ref, *, chunk, rows, alpha):
    c = pl.program_id(0)
    e = pl.program_id(1)
    n_e = pl.num_programs(1)
    group = 8
    sub = min(512, chunk)
    last = tok_ref.shape[0] - 1

    @pl.when((c == 0) & (e == 0))
    def _():
        xg_ref[...] = jnp.zeros_like(xg_ref)

    @pl.when(e == 0)
    def _():
        for s0 in range(0, chunk, sub):
            xb = _tm_load(x_ref, s0, sub).astype(BF16)
            h = _swiglu(xb, sg_ref[...], su_ref[...]).astype(BF16)
            _tm_store(acc_ref, s0, jnp.dot(h, sd_ref[...], preferred_element_type=F32))
        acc_ref[chunk * PIECES:(chunk + 1) * PIECES, :] = jnp.zeros((PIECES, LANES), F32)

    lo = start_ref[c * (n_e + 1) + e]
    hi = start_ref[c * (n_e + 1) + e + 1]

    def block(bi, carry):
        base = lo + bi * rows
        nrow = jnp.minimum(hi - base, rows)
        nfull = nrow // group
        ragged = nrow % group != 0

        def gather(gi, partial):
            r0 = pl.multiple_of(gi * group, group)
            for i in range(group):
                idx = jnp.minimum(base + r0 + i, last) if partial else base + r0 + i
                src = pl.multiple_of(tok_ref[idx] * PIECES, PIECES)
                dst = pl.multiple_of((r0 + i) * PIECES, PIECES)
                xg_ref[pl.ds(dst, PIECES), :] = x_ref[pl.ds(src, PIECES), :]

        def scatter(gi, partial):
            r0 = pl.multiple_of(gi * group, group)
            dsts, vals = [], []
            for i in range(group):
                idx = jnp.minimum(base + r0 + i, last) if partial else base + r0 + i
                t = tok_ref[idx]
                if partial:
                    t = jnp.where(r0 + i < nrow, t, chunk)
                dsts.append(pl.multiple_of(t * PIECES, PIECES))
                row = og_ref[pl.ds(pl.multiple_of((r0 + i) * PIECES, PIECES), PIECES), :]
                vals.append(acc_ref[pl.ds(dsts[i], PIECES), :] + gate_ref[idx] * row)
            for i in range(group):
                acc_ref[pl.ds(dsts[i], PIECES), :] = vals[i]

        def sweep(fn):
            def body(gi, carry):
                fn(gi, False)
                return carry

            lax.fori_loop(0, nfull, body, 0)
            pl.when(ragged)(lambda: fn(nfull, True))

        sweep(gather)
        xb = _tm_load(xg_ref, 0, rows).astype(BF16)
        h = _swiglu(xb, wg_ref[0], wu_ref[0]).astype(BF16)
        _tm_store(og_ref, 0, jnp.dot(h, wd_ref[0], preferred_element_type=F32))
        sweep(scatter)
        return carry

    lax.fori_loop(0, (hi - lo + rows - 1) // rows, block, 0)

    @pl.when(e == n_e - 1)
    def _():
        for s0 in range(0, chunk, sub):
            z = alpha * _tm_load(x_ref, s0, sub) + _tm_load(acc_ref, s0, sub)
            o_ref[s0:s0 + sub, :] = _layer_norm(z, g_ref[...], beta_ref[...])


def _moe_ln(x_tm, eidx, wts, wg, wu, wd, sg, su, sd, g, beta, alpha):
    n = x_tm.shape[0] // PIECES
    d = PIECES * LANES
    chunk = min(MOE_CHUNK, n)
    rows = MOE_ROWS
    starts, tok, gate = _dispatch(eidx, wts, chunk)
    nc = n // chunk
    a = chunk * TOP_K
    n_e, _, de = wg.shape
    full = lambda arr: pl.BlockSpec(arr.shape, lambda c, e, s_r: (0,) * arr.ndim)
    lists = pl.BlockSpec((a,), lambda c, e, s_r: (c,), memory_space=pltpu.SMEM)
    grid_spec = pltpu.PrefetchScalarGridSpec(
        num_scalar_prefetch=1,
        grid=(nc, n_e),
        in_specs=[
            lists, lists,
            pl.BlockSpec((chunk * PIECES, LANES), lambda c, e, s_r: (c, 0)),
            pl.BlockSpec((1, d, de), lambda c, e, s_r: (e, 0, 0)),
            pl.BlockSpec((1, d, de), lambda c, e, s_r: (e, 0, 0)),
            pl.BlockSpec((1, de, d), lambda c, e, s_r: (e, 0, 0)),
            full(sg), full(su), full(sd), full(g), full(beta),
        ],
        out_specs=pl.BlockSpec((chunk, d), lambda c, e, s_r: (c, 0)),
        scratch_shapes=[pltpu.VMEM(((chunk + 1) * PIECES, LANES), F32), pltpu.VMEM((rows * PIECES, LANES), F32),
                        pltpu.VMEM((rows * PIECES, LANES), F32)],
    )
    return pl.pallas_call(
        functools.partial(_moe_kernel, chunk=chunk, rows=rows, alpha=alpha),
        grid_spec=grid_spec,
        out_shape=jax.ShapeDtypeStruct((n, d), F32),
        compiler_params=_cparams("arbitrary", "arbitrary"),
        name="moe_experts",
    )(starts, tok, gate, x_tm, wg, wu, wd, sg, su, sd, g, beta)


def _rope_tables(positions):
    pos = positions.reshape(-1).astype(F32)[:, None]
    n = pos.shape[0]

    def cs(dim):
        inv = ROPE_THETA ** (-jnp.arange(0, dim, 2, dtype=F32) / dim)
        ang = pos * inv
        return jnp.cos(ang), jnp.sin(ang)

    c, s = cs(HEAD_DIM)
    c64 = jnp.concatenate([c, c, c, c], -1)
    s64 = jnp.concatenate([-s, s, -s, s], -1)
    c, s = cs(MLA_ROPE)
    one = jnp.ones((n, MLA_NOPE), F32)
    pad = LANES - MLA_NOPE - MLA_ROPE
    cm = jnp.concatenate([one, c, c, jnp.ones((n, pad), F32)], -1)
    sm = jnp.concatenate([0 * one, -s, s, jnp.zeros((n, pad), F32)], -1)
    return (c64, s64), (cm, sm)


def _even_mixer(xf, b, s, tabs, w_in, g_qa, w_qb, g_kva, w_kvb):
    d = xf.shape[1]
    tab64, tabm = tabs
    na = 3 * 512
    kpe_lo = na + 256 + 128
    wpad = jnp.concatenate([w_in[:, :kpe_lo], jnp.zeros((d, MLA_NOPE), F32), w_in[:, kpe_lo:],
                            jnp.zeros((d, LANES - MLA_NOPE - MLA_ROPE), F32)], 1).astype(BF16)
    nh = w_qb.shape[1] // (MLA_NOPE + MLA_ROPE)
    wqb = jnp.pad(w_qb.reshape(-1, nh, MLA_NOPE + MLA_ROPE), ((0, 0), (0, 0), (0, LANES - MLA_NOPE - MLA_ROPE)))
    wqb = wqb.reshape(-1, nh * LANES).astype(BF16)
    wkv3 = w_kvb.reshape(-1, nh, MLA_NOPE + MLA_V)
    wkk = jnp.pad(wkv3[:, :, :MLA_NOPE], ((0, 0), (0, 0), (0, LANES - MLA_NOPE))).reshape(-1, nh * LANES).astype(BF16)
    wkv = wkv3[:, :, MLA_NOPE:].reshape(-1, nh * MLA_V).astype(BF16)
    segs = [(0, 512, True, QK_SCALE), (512, 1024, True, 1.0), (1024, 1536, False, 1.0)]
    qa, ka, va, qm, km, vm = _projection(
        xf, wpad, segs, tab64,
        mla_args=(tabm[0], tabm[1], g_qa.reshape(1, -1), wqb, g_kva.reshape(1, -1), wkk, wkv),
        mla_out_widths=(nh * LANES, nh * LANES, nh * MLA_V))
    o_a = _moba_attention(qa, ka, va, b, s)
    o_b = _mla_attention(qm, km, vm, b, s)
    return o_a, o_b


def _odd_mixer(xf, b, s, tabs, w_in, sinks, lam_p, g_subln, lambda_init):
    segs = [(0, 512, True, QK_SCALE), (512, 640, True, 1.0), (640, 768, False, 1.0),
            (768, 1280, True, QK_SCALE), (1280, 1792, True, 1.0), (1792, 2304, False, 1.0)]
    qc, kc, vc, qd, kd, vd = _projection(xf, w_in.astype(BF16), segs, tabs[0])
    o_c = _swa_attention(qc, kc, vc, sinks.astype(F32), b, s)
    o_d = _diff_attention(qd, kd, vd, lam_p.astype(F32), g_subln.reshape(1, -1), lambda_init, b, s)
    return o_c, o_d


def kernel(x, positions, ev_w_in, ev_w_out, mla_g_qa, mla_w_qb, mla_g_kva, mla_w_kvb, od_w_in, od_w_out, swa_sinks, diff_lambda, diff_g_subln, ln_g, ln_b, moe_w_router, moe_bias, moe_w_gate, moe_w_up, moe_w_down, sh_w_gate, sh_w_up, sh_w_down):
    b, s, d = x.shape
    depth = ln_g.shape[0]
    alpha = float((2 * depth) ** 0.25)
    xf = x.reshape(b * s, d)
    tabs = _rope_tables(positions)
    for l in range(depth):
        i = l // 2
        if l % 2 == 0:
            o1, o2 = _even_mixer(xf, b, s, tabs, ev_w_in[i], mla_g_qa[i], mla_w_qb[i], mla_g_kva[i], mla_w_kvb[i])
            w_out = ev_w_out[i]
        else:
            lambda_init = 0.8 - 0.6 * float(np.exp(-0.3 * l))
            o1, o2 = _odd_mixer(xf, b, s, tabs, od_w_in[i], swa_sinks[i], diff_lambda[i], diff_g_subln[i], lambda_init)
            w_out = od_w_out[i]
        xf, x_tm = _out_ln(xf, o1, o2, w_out.astype(BF16), ln_g[l, 0].reshape(1, d), ln_b[l, 0].reshape(1, d), alpha)
        eidx, wts = _router(xf, moe_w_router[l], moe_bias[l])
        xf = _moe_ln(x_tm, eidx, wts, moe_w_gate[l].astype(BF16), moe_w_up[l].astype(BF16),
                     moe_w_down[l].astype(BF16), sh_w_gate[l].astype(BF16), sh_w_up[l].astype(BF16),
                     sh_w_down[l].astype(BF16), ln_g[l, 1].reshape(1, d), ln_b[l, 1].reshape(1, d), alpha)
    return xf.reshape(b, s, d)
```

```python
import functools

import numpy as np
import jax
import jax.numpy as jnp
from jax import lax
from jax.experimental import pallas as pl
from jax.experimental.pallas import tpu as pltpu

F32 = jnp.float32
BF16 = jnp.bfloat16
I32 = jnp.int32

LANES = 128
NEG = -1e30

HEAD_DIM = 64
MOBA_BLOCK = 256
MOBA_TOPK = 3
MLA_NOPE = 64
MLA_ROPE = 32
MLA_V = 64
SWA_WINDOW = 128
N_EXPERTS = 64
N_GROUPS = 8
TOPK_GROUPS = 4
TOP_K = 8
ROUTED_SCALE = 2.5
ROPE_THETA = 10000.0
QK_SCALE = HEAD_DIM ** -0.5
MLA_SCALE = (MLA_NOPE + MLA_ROPE) ** -0.5

PROJ_ROWS = 512
ATTN_TILE = 1024
SWA_ROWS = 512
ROUTER_ROWS = 512
MOE_CHUNK = 2048
MOE_ROWS = 320
MOE_HEAD = 256
VMEM_LIMIT = 56 * 1024 * 1024

_NT = (((1,), (1,)), ((), ()))


def _cparams(*sem):
    return pltpu.CompilerParams(dimension_semantics=sem, vmem_limit_bytes=VMEM_LIMIT)


def _lane_iota(rows):
    return lax.broadcasted_iota(I32, (rows, LANES), 1)


def _rope(x, c, s, first_half, half):
    rot = jnp.where(first_half, pltpu.roll(x, LANES - half, 1), pltpu.roll(x, half, 1))
    return x * c + rot * s


def _rms(x, g, eps):
    return x * lax.rsqrt(jnp.mean(x * x, -1, keepdims=True) + eps) * g


def _proj_kernel(segs, mla, x_ref, w_ref, c64_ref, s64_ref, *refs):
    xb = x_ref[...].astype(BF16)
    rows = xb.shape[0]
    lane = _lane_iota(rows)
    fh64 = (lane % HEAD_DIM) < HEAD_DIM // 2
    c64 = c64_ref[...]
    s64 = s64_ref[...]

    def seg(lo, hi):
        return jnp.dot(xb, w_ref[:, lo:hi], preferred_element_type=F32)

    if mla:
        cm_ref, sm_ref, gq_ref, wqb_ref, gkv_ref, wkk_ref, wkv_ref = refs[:7]
        refs = refs[7:]
    for (lo, hi, rope, scale), o_ref in zip(segs, refs):
        h = seg(lo, hi)
        if rope:
            for g in range((hi - lo) // LANES):
                sl = slice(g * LANES, (g + 1) * LANES)
                o_ref[:, sl] = (_rope(h[:, sl], c64, s64, fh64, HEAD_DIM // 2) * scale).astype(BF16)
        else:
            o_ref[...] = h.astype(BF16)
    if mla:
        qm_ref, km_ref, vm_ref = refs[len(segs):]
        cm = cm_ref[...]
        sm = sm_ref[...]
        fhm = (lane >= MLA_NOPE) & (lane < MLA_NOPE + MLA_ROPE // 2)
        lo = segs[-1][1]
        qn = _rms(seg(lo, lo + 256), gq_ref[...], 1e-6).astype(BF16)
        qm = jnp.dot(qn, wqb_ref[...], preferred_element_type=F32)
        kvn = _rms(seg(lo + 256, lo + 384), gkv_ref[...], 1e-6).astype(BF16)
        kk = jnp.dot(kvn, wkk_ref[...], preferred_element_type=F32)
        kpe = _rope(seg(lo + 384, lo + 512), cm, sm, fhm, MLA_ROPE // 2)
        for g in range(qm.shape[1] // LANES):
            sl = slice(g * LANES, (g + 1) * LANES)
            qm_ref[:, sl] = (_rope(qm[:, sl], cm, sm, fhm, MLA_ROPE // 2) * MLA_SCALE).astype(BF16)
            km_ref[:, sl] = (kk[:, sl] + kpe).astype(BF16)
        vm_ref[...] = jnp.dot(kvn, wkv_ref[...], preferred_element_type=F32).astype(BF16)


def _projection(xf, w, segs, tabs, mla_args=None, mla_out_widths=()):
    n, d = xf.shape
    tm = min(PROJ_ROWS, n)
    row = lambda width: pl.BlockSpec((tm, width), lambda i: (i, 0))
    full = lambda a: pl.BlockSpec(a.shape, lambda i: (0,) * a.ndim)
    ins = [xf, w, tabs[0], tabs[1]]
    in_specs = [row(d), full(w), row(LANES), row(LANES)]
    if mla_args is not None:
        cm, sm, gq, wqb, gkv, wkk, wkv = mla_args
        ins += [cm, sm, gq, wqb, gkv, wkk, wkv]
        in_specs += [row(LANES), row(LANES), full(gq), full(wqb), full(gkv), full(wkk), full(wkv)]
    widths = [hi - lo for lo, hi, _, _ in segs] + list(mla_out_widths)
    return pl.pallas_call(
        functools.partial(_proj_kernel, tuple(segs), mla_args is not None),
        grid=(n // tm,),
        in_specs=in_specs,
        out_specs=[row(wd) for wd in widths],
        out_shape=[jax.ShapeDtypeStruct((n, wd), BF16) for wd in widths],
        compiler_params=_cparams("parallel"),
        name="projection",
    )(*ins)


def _flash_update(c, q, k, v, bias, state, first):
    m_ref, l_ref, acc_ref = state
    s = lax.dot_general(q, k, _NT, preferred_element_type=F32)
    if bias is not None:
        s = s + bias
    t, tk = s.shape
    mx = jnp.max(s, -1, keepdims=True)
    if first:
        m_new = jnp.broadcast_to(mx, (t, LANES))
    else:
        m_prev = m_ref[c]
        m_new = jnp.maximum(m_prev, mx)
        alpha = jnp.exp(m_prev - m_new)
    ps, psum = [], None
    for j in range(tk // LANES):
        pj = jnp.exp(s[:, j * LANES:(j + 1) * LANES] - m_new)
        ps.append(pj.astype(BF16))
        psum = pj if psum is None else psum + pj
    pv = jnp.dot(jnp.concatenate(ps, axis=1), v, preferred_element_type=F32)
    if first:
        l_ref[c] = psum
        acc_ref[c] = pv
    else:
        l_ref[c] = alpha * l_ref[c] + psum
        acc_ref[c] = alpha * acc_ref[c] + pv
    m_ref[c] = m_new


def _flash_out(c, state):
    _, l_ref, acc_ref = state
    return acc_ref[c] / jnp.sum(l_ref[c], -1, keepdims=True)


def _flash_state(chains, t):
    return [pltpu.VMEM((chains, t, LANES), F32) for _ in range(3)]


def _causal_sweep(qi, step):
    step(qi, True)

    def body(j, carry):
        step(j, False)
        return carry

    lax.fori_loop(0, qi, body, 0)


def _causal_bias(t, block=None):
    r = lax.broadcasted_iota(I32, (t, t), 0)
    c = lax.broadcasted_iota(I32, (t, t), 1)
    hide = c > r
    if block is not None:
        hide = hide & ((r // block) == (c // block))
    return jnp.where(hide, NEG, 0.0).astype(F32)


def _mla_kernel(q_ref, k_ref, v_ref, o_ref, m_ref, l_ref, acc_ref, *, t):
    qi = pl.program_id(2)
    dbias = _causal_bias(t)
    state = (m_ref, l_ref, acc_ref)

    def step(j, first):
        ks = pl.multiple_of(j * t, t)
        v = v_ref[pl.ds(ks, t), :]
        for h in range(2):
            hs = slice(h * LANES, (h + 1) * LANES)
            _flash_update(h, q_ref[:, hs], k_ref[pl.ds(ks, t), hs], v, dbias if first else None, state, first)

    _causal_sweep(qi, step)
    lane = _lane_iota(t)
    o_ref[...] = jnp.where(lane < MLA_V, _flash_out(0, state), _flash_out(1, state)).astype(BF16)


def _mla_attention(qm, km, vm, b, s):
    t = min(ATTN_TILE, s)
    nq = s // t
    hp = qm.shape[1] // (2 * LANES)
    return pl.pallas_call(
        functools.partial(_mla_kernel, t=t),
        grid=(b, hp, nq),
        in_specs=[pl.BlockSpec((t, 2 * LANES), lambda bi, h, qi: (bi * nq + qi, h)),
                  pl.BlockSpec((s, 2 * LANES), lambda bi, h, qi: (bi, h)),
                  pl.BlockSpec((s, LANES), lambda bi, h, qi: (bi, h))],
        out_specs=pl.BlockSpec((t, LANES), lambda bi, h, qi: (bi * nq + qi, h)),
        out_shape=jax.ShapeDtypeStruct((b * s, hp * LANES), BF16),
        scratch_shapes=_flash_state(2, t),
        compiler_params=_cparams("parallel", "parallel", "arbitrary"),
        name="mla_attention",
    )(qm, km, vm)


def _moba_kernel(q_ref, k_ref, v_ref, o_ref, kmean_ref, kaug_ref, qaug_ref, m_ref, l_ref, acc_ref, *, t, nb):
    qi = pl.program_id(2)
    bs = MOBA_BLOCK
    s_len = k_ref.shape[0]
    half = LANES // 2

    @pl.when(qi == 0)
    def _():
        kmean_ref[...] = jnp.zeros_like(kmean_ref)
        for n in range(nb):
            blk = k_ref[n * bs:(n + 1) * bs, :].astype(F32)
            kmean_ref[n:n + 1, :] = jnp.sum(blk, 0, keepdims=True) * (1.0 / bs)
        lane_s = _lane_iota(s_len)
        row_blk = lax.broadcasted_iota(I32, (s_len, LANES), 0) // bs
        kf = k_ref[...].astype(F32)
        kaug_ref[0] = jnp.where(lane_s < half, kf, (lane_s - half == row_blk).astype(F32)).astype(BF16)
        kaug_ref[1] = jnp.where(lane_s >= half, kf, (lane_s == row_blk).astype(F32)).astype(BF16)

    lane = _lane_iota(t)
    own = qi * (t // bs) + lax.broadcasted_iota(I32, (t, LANES), 0) // bs
    qf = q_ref[...].astype(F32)
    kmean = kmean_ref[...]
    for h in range(2):
        head = (lane >= h * half) & (lane < (h + 1) * half)
        qh = jnp.where(head, qf, 0.0)
        gate = lax.dot_general(qh, kmean, _NT, preferred_element_type=F32, precision=lax.Precision.HIGHEST)
        g = jnp.where(lane < own, gate, -jnp.inf)
        sel = lane == own
        for _ in range(MOBA_TOPK):
            mx = jnp.max(g, -1, keepdims=True)
            idx = jnp.min(jnp.where(g == mx, lane, LANES), -1, keepdims=True)
            hit = lane == idx
            sel = sel | (hit & (lane < own))
            g = jnp.where(hit, -jnp.inf, g)
        selb = jnp.where(sel | (lane >= nb), 0.0, NEG).astype(F32)
        if h == 0:
            selb = pltpu.roll(selb, half, 1)
        qaug_ref[h] = jnp.where(head, qf, selb).astype(BF16)

    dbias = _causal_bias(t, bs)
    state = (m_ref, l_ref, acc_ref)

    def step(j, first):
        ks = pl.multiple_of(j * t, t)
        v = v_ref[pl.ds(ks, t), :]
        for h in range(2):
            _flash_update(h, qaug_ref[h], kaug_ref[h, pl.ds(ks, t), :], v, dbias if first else None, state, first)

    _causal_sweep(qi, step)
    o_ref[...] = jnp.where(lane < half, _flash_out(0, state), _flash_out(1, state)).astype(BF16)


def _moba_attention(q, k, v, b, s):
    bs = MOBA_BLOCK
    nb = s // bs
    assert nb <= LANES // 2, "block gate must fit the spare half of the lanes"
    t = min(ATTN_TILE, s)
    nq = s // t
    hp = q.shape[1] // LANES
    return pl.pallas_call(
        functools.partial(_moba_kernel, t=t, nb=nb),
        grid=(b, hp, nq),
        in_specs=[pl.BlockSpec((t, LANES), lambda bi, h, qi: (bi * nq + qi, h)),
                  pl.BlockSpec((s, LANES), lambda bi, h, qi: (bi, h)),
                  pl.BlockSpec((s, LANES), lambda bi, h, qi: (bi, h))],
        out_specs=pl.BlockSpec((t, LANES), lambda bi, h, qi: (bi * nq + qi, h)),
        out_shape=jax.ShapeDtypeStruct((b * s, hp * LANES), BF16),
        scratch_shapes=[pltpu.VMEM((LANES, LANES), F32), pltpu.VMEM((2, s, LANES), BF16),
                        pltpu.VMEM((2, t, LANES), BF16)] + _flash_state(2, t),
        compiler_params=_cparams("parallel", "parallel", "arbitrary"),
        name="moba_attention",
    )(q, k, v)


def _diff_kernel(q_ref, k_ref, v_ref, lam_ref, g_ref, o_ref, qs_ref, m_ref, l_ref, acc_ref, *, t, lambda_init):
    qi = pl.program_id(2)
    lane = _lane_iota(t)
    qf = q_ref[...].astype(F32)
    qs_ref[0] = jnp.where(lane < HEAD_DIM, qf, 0.0).astype(BF16)
    qs_ref[1] = jnp.where(lane >= HEAD_DIM, qf, 0.0).astype(BF16)
    dbias = _causal_bias(t)
    state = (m_ref, l_ref, acc_ref)

    def step(j, first):
        ks = pl.multiple_of(j * t, t)
        k = k_ref[pl.ds(ks, t), :]
        v = v_ref[pl.ds(ks, t), :]
        for c in range(2):
            _flash_update(c, qs_ref[c], k, v, dbias if first else None, state, first)

    _causal_sweep(qi, step)
    lp = lam_ref[...]
    lam = (jnp.exp(jnp.sum(lp[0:1] * lp[1:2], -1, keepdims=True))
           - jnp.exp(jnp.sum(lp[2:3] * lp[3:4], -1, keepdims=True)) + lambda_init)
    o = _flash_out(0, state) - lam * _flash_out(1, state)
    o_ref[...] = (_rms(o, g_ref[...], 1e-5) * (1.0 - lambda_init)).astype(BF16)


def _diff_attention(q, k, v, lam_p, g_subln, lambda_init, b, s):
    t = min(ATTN_TILE, s)
    nq = s // t
    nh = q.shape[1] // LANES
    return pl.pallas_call(
        functools.partial(_diff_kernel, t=t, lambda_init=lambda_init),
        grid=(b, nh, nq),
        in_specs=[pl.BlockSpec((t, LANES), lambda bi, h, qi: (bi * nq + qi, h)),
                  pl.BlockSpec((s, LANES), lambda bi, h, qi: (bi, h)),
                  pl.BlockSpec((s, LANES), lambda bi, h, qi: (bi, h)),
                  pl.BlockSpec(lam_p.shape, lambda bi, h, qi: (0, 0)),
                  pl.BlockSpec(g_subln.shape, lambda bi, h, qi: (0, 0))],
        out_specs=pl.BlockSpec((t, LANES), lambda bi, h, qi: (bi * nq + qi, h)),
        out_shape=jax.ShapeDtypeStruct((b * s, nh * LANES), BF16),
        scratch_shapes=[pltpu.VMEM((2, t, LANES), BF16)] + _flash_state(2, t),
        compiler_params=_cparams("parallel", "parallel", "arbitrary"),
        name="diff_attention",
    )(q, k, v, lam_p, g_subln)


def _swa_kernel(sink_ref, q_ref, k_ref, v_ref, o_ref, *, rows):
    ti = pl.program_id(1)
    w = SWA_WINDOW
    nsub = rows // w
    n_q = q_ref.shape[1] // HEAD_DIM
    n_kv = k_ref.shape[1] // HEAD_DIM
    grp = n_q // n_kv
    lane_k = lax.broadcasted_iota(I32, (2 * w, LANES), 1)
    lane_o = _lane_iota(w)
    qrow = lax.broadcasted_iota(I32, (grp * w, 2 * w), 0) % w
    kcol = lax.broadcasted_iota(I32, (grp * w, 2 * w), 1)
    for i in range(nsub):
        sb = ti * nsub + i
        start = pl.multiple_of(jnp.maximum(sb - 1, 0) * w, w)
        kb = k_ref[pl.ds(start, 2 * w), :]
        vb = v_ref[pl.ds(start, 2 * w), :]
        rel = qrow + (sb * w - start) - kcol
        bias = jnp.where((rel >= 0) & (rel < w), 0.0, NEG).astype(F32)
        qf = q_ref[i * w:(i + 1) * w, :].astype(F32)
        head_out = [None] * n_q
        for kvh in range(n_kv):
            km = jnp.where(lane_k // HEAD_DIM == kvh, kb.astype(F32), 0.0).astype(BF16)
            qs, sinks = [], []
            for a in range(grp):
                h = kvh * grp + a
                qp = qf[:, (h // 2) * LANES:(h // 2 + 1) * LANES]
                if h % 2 != kvh:
                    qp = pltpu.roll(qp, HEAD_DIM, 1)
                qs.append(qp.astype(BF16))
                sinks.append(jnp.full((w, 1), sink_ref[h], F32))
            sink = jnp.concatenate(sinks, 0)
            s = lax.dot_general(jnp.concatenate(qs, 0), km, _NT, preferred_element_type=F32) + bias
            m = jnp.maximum(jnp.max(s, -1, keepdims=True), sink)
            p = jnp.exp(s - m)
            den = jnp.sum(p, -1, keepdims=True) + jnp.exp(sink - m)
            o = jnp.dot(p.astype(BF16), vb, preferred_element_type=F32) / den
            for a in range(grp):
                h = kvh * grp + a
                oh = o[a * w:(a + 1) * w, :]
                if h % 2 != kvh:
                    oh = pltpu.roll(oh, HEAD_DIM, 1)
                head_out[h] = oh
        for hp in range(n_q // 2):
            pair = jnp.where(lane_o < HEAD_DIM, head_out[2 * hp], head_out[2 * hp + 1])
            o_ref[i * w:(i + 1) * w, hp * LANES:(hp + 1) * LANES] = pair.astype(BF16)


def _swa_attention(q, k, v, sinks, b, s):
    rows = min(SWA_ROWS, s)
    nt = s // rows
    return pl.pallas_call(
        functools.partial(_swa_kernel, rows=rows),
        grid=(b, nt),
        in_specs=[pl.BlockSpec(memory_space=pltpu.SMEM),
                  pl.BlockSpec((rows, q.shape[1]), lambda bi, ti: (bi * nt + ti, 0)),
                  pl.BlockSpec((s, k.shape[1]), lambda bi, ti: (bi, 0)),
                  pl.BlockSpec((s, v.shape[1]), lambda bi, ti: (bi, 0))],
        out_specs=pl.BlockSpec((rows, q.shape[1]), lambda bi, ti: (bi * nt + ti, 0)),
        out_shape=jax.ShapeDtypeStruct(q.shape, BF16),
        compiler_params=_cparams("parallel", "arbitrary"),
        name="swa_attention",
    )(sinks, q, k, v)


def _layer_norm(y, g, b):
    mu = jnp.mean(y, -1, keepdims=True)
    yc = y - mu
    var = jnp.mean(yc * yc, -1, keepdims=True)
    return yc * lax.rsqrt(var + 1e-5) * g + b


PIECES = 8


def _tm_load(ref, r0, nrows):
    return jnp.concatenate([ref[pl.ds(r0 * PIECES + p, nrows, stride=PIECES), :] for p in range(PIECES)], axis=1)


def _tm_store(ref, r0, val):
    for p in range(PIECES):
        ref[pl.ds(r0 * PIECES + p, val.shape[0], stride=PIECES), :] = val[:, p * LANES:(p + 1) * LANES]


def _out_ln_kernel(x_ref, a_ref, b_ref, w_ref, g_ref, beta_ref, o_ref, otm_ref, *, alpha):
    ka = a_ref.shape[1]
    mix = (jnp.dot(a_ref[...], w_ref[:ka, :], preferred_element_type=F32)
           + jnp.dot(b_ref[...], w_ref[ka:, :], preferred_element_type=F32))
    y = _layer_norm(alpha * x_ref[...] + mix, g_ref[...], beta_ref[...])
    o_ref[...] = y
    _tm_store(otm_ref, 0, y)


def _out_ln(xf, a, b, w, g, beta, alpha):
    n, d = xf.shape
    assert d == PIECES * LANES
    tm = min(PROJ_ROWS, n)
    row = lambda width: pl.BlockSpec((tm, width), lambda i: (i, 0))
    full = lambda arr: pl.BlockSpec(arr.shape, lambda i: (0, 0))
    return pl.pallas_call(
        functools.partial(_out_ln_kernel, alpha=alpha),
        grid=(n // tm,),
        in_specs=[row(d), row(a.shape[1]), row(b.shape[1]), full(w), full(g), full(beta)],
        out_specs=[row(d), pl.BlockSpec((tm * PIECES, LANES), lambda i: (i, 0))],
        out_shape=[jax.ShapeDtypeStruct((n, d), F32), jax.ShapeDtypeStruct((n * PIECES, LANES), F32)],
        compiler_params=_cparams("parallel"),
        name="out_proj_ln",
    )(xf, a, b, w, g, beta)


def _row_argmax(v, row, nrow):
    mx = jnp.max(v, 0, keepdims=True)
    idx = jnp.min(jnp.where(v == mx, row, nrow), 0, keepdims=True)
    return mx, idx


def _router_kernel(x_ref, w_ref, bias_ref, eidx_ref, wts_ref):
    logits = jnp.dot(x_ref[...], w_ref[...], preferred_element_type=F32, precision=lax.Precision.HIGHEST)
    tm = logits.shape[0]
    scores = jax.nn.sigmoid(logits.T[:N_EXPERTS, :])
    biased = scores + bias_ref[...]
    per = N_EXPERTS // N_GROUPS
    row_g = lax.broadcasted_iota(I32, (per, tm), 0)
    gsc = []
    for g in range(N_GROUPS):
        v = biased[g * per:(g + 1) * per, :]
        m1, i1 = _row_argmax(v, row_g, per)
        m2 = jnp.max(jnp.where(row_g == i1, -jnp.inf, v), 0, keepdims=True)
        gsc.append(m1 + m2)
    gscore = jnp.concatenate(gsc, 0)
    row_n = lax.broadcasted_iota(I32, (N_GROUPS, tm), 0)
    gok = jnp.zeros((N_GROUPS, tm), F32)
    for _ in range(TOPK_GROUPS):
        _, ig = _row_argmax(gscore, row_n, N_GROUPS)
        hit = row_n == ig
        gok = jnp.where(hit, 1.0, gok)
        gscore = jnp.where(hit, -jnp.inf, gscore)
    cand = jnp.concatenate([jnp.where(gok[g:g + 1, :] > 0.0, biased[g * per:(g + 1) * per, :], -jnp.inf)
                            for g in range(N_GROUPS)], 0)
    row_e = lax.broadcasted_iota(I32, (N_EXPERTS, tm), 0)
    ids, ws = [], []
    for _ in range(TOP_K):
        _, ie = _row_argmax(cand, row_e, N_EXPERTS)
        hit = row_e == ie
        ids.append(ie)
        ws.append(jnp.sum(jnp.where(hit, scores, 0.0), 0, keepdims=True))
        cand = jnp.where(hit, -jnp.inf, cand)
    w = jnp.concatenate(ws, 0)
    eidx_ref[...] = jnp.concatenate(ids, 0)
    wts_ref[...] = w / jnp.sum(w, 0, keepdims=True) * ROUTED_SCALE


def _router(xf, w_router, e_bias):
    n, d = xf.shape
    tm = min(ROUTER_ROWS, n)
    wpad = jnp.pad(w_router, ((0, 0), (0, LANES - N_EXPERTS)))
    bcol = e_bias.astype(F32).reshape(N_EXPERTS, 1)
    out = pl.BlockSpec((TOP_K, tm), lambda i: (0, i))
    return pl.pallas_call(
        _router_kernel,
        grid=(n // tm,),
        in_specs=[pl.BlockSpec((tm, d), lambda i: (i, 0)), pl.BlockSpec(wpad.shape, lambda i: (0, 0)),
                  pl.BlockSpec(bcol.shape, lambda i: (0, 0))],
        out_specs=[out, out],
        out_shape=[jax.ShapeDtypeStruct((TOP_K, n), I32), jax.ShapeDtypeStruct((TOP_K, n), F32)],
        compiler_params=_cparams("parallel"),
        name="router",
    )(xf, wpad, bcol)


def _dispatch(eidx_t, wts_t, chunk):
    n = eidx_t.shape[1]
    nc = n // chunk
    a = chunk * TOP_K
    ec = eidx_t.reshape(TOP_K, nc, chunk).transpose(1, 0, 2).reshape(nc, a)
    wc = wts_t.reshape(TOP_K, nc, chunk).transpose(1, 0, 2).reshape(nc, a)
    ex = jnp.arange(N_EXPERTS, dtype=I32)
    counts = jnp.sum((ec[:, :, None] == ex[None, None, :]).astype(I32), axis=1)
    ends = jnp.cumsum(counts, axis=1).astype(I32)
    starts = jnp.concatenate([jnp.zeros((nc, 1), I32), ends], axis=1)
    mult = 1 << int(chunk - 1).bit_length()
    tok_ids = jnp.broadcast_to(jnp.arange(a, dtype=I32)[None, :] % chunk, (nc, a))
    keys, gates = lax.sort((ec * mult + tok_ids, wc), dimension=1, num_keys=1)
    return starts.reshape(-1), (keys % mult).reshape(-1), gates.reshape(-1)


def _swiglu(xb, wg, wu):
    hg = jnp.dot(xb, wg, preferred_element_type=F32)
    hu = jnp.dot(xb, wu, preferred_element_type=F32)
    return hg * jax.nn.sigmoid(hg) * hu


def _moe_kernel(start_ref, tok_ref, gate_ref, x_ref, wg_ref, wu_ref, wd_ref, sg_ref, su_ref, sd_ref,
                g_ref, beta_ref, o_ref, acc_ref, xga_ref, xgb_ref, oga_ref, ogb_ref, xgo_ref, ogo_ref,
                *, chunk, rows, head, alpha):
    c = pl.program_id(0)
    e = pl.program_id(1)
    n_e = pl.num_programs(1)
    group = 8
    sub = min(512, chunk)
    last = tok_ref.shape[0] - 1

    @pl.when((c == 0) & (e == 0))
    def _():
        for ref in (xga_ref, xgb_ref, xgo_ref):
            ref[...] = jnp.zeros_like(ref)

    @pl.when(e == 0)
    def _():
        for s0 in range(0, chunk, sub):
            xb = _tm_load(x_ref, s0, sub).astype(BF16)
            h = _swiglu(xb, sg_ref[...], su_ref[...]).astype(BF16)
            _tm_store(acc_ref, s0, jnp.dot(h, sd_ref[...], preferred_element_type=F32))
        acc_ref[chunk * PIECES:(chunk + 1) * PIECES, :] = jnp.zeros((PIECES, LANES), F32)

    def run(ei):
        return start_ref[c * (n_e + 1) + ei], start_ref[c * (n_e + 1) + ei + 1]

    lo, hi = run(e)
    n_cur = jnp.minimum(hi - lo, rows)
    lo_next, _ = run(jnp.minimum(e + 1, n_e - 1))
    lo_prev, hi_prev = run(jnp.maximum(e - 1, 0))
    n_prev = jnp.where(e > 0, jnp.minimum(hi_prev - lo_prev, rows), 0)

    def token_rows(r):
        return pl.ds(pl.multiple_of(r * PIECES, PIECES), PIECES)

    def gather_group(xg, base, r0, clamp):
        for i in range(group):
            idx = jnp.minimum(base + r0 + i, last) if clamp else base + r0 + i
            xg[token_rows(r0 + i), :] = x_ref[token_rows(tok_ref[idx]), :]

    def scatter_group(og, base, nrow, r0, mask):
        toks, vals = [], []
        for i in range(group):
            idx = jnp.minimum(base + r0 + i, last) if mask else base + r0 + i
            t = tok_ref[idx]
            if mask:
                t = jnp.where(r0 + i < nrow, t, chunk)
            toks.append(t)
            vals.append(acc_ref[token_rows(t), :] + gate_ref[idx] * og[token_rows(r0 + i), :])
        for i in range(group):
            acc_ref[token_rows(toks[i]), :] = vals[i]

    def sweep(fn, r_lo, r_hi):
        n = jnp.maximum(r_hi - r_lo, 0)
        nfull = n // group

        def body(gi, carry):
            fn(pl.multiple_of(r_lo + gi * group, group), False)
            return carry

        lax.fori_loop(0, nfull, body, 0)
        pl.when(n % group != 0)(lambda: fn(pl.multiple_of(r_lo + nfull * group, group), True))

    def expert(xg, og):
        xb = _tm_load(xg, 0, rows).astype(BF16)
        h = _swiglu(xb, wg_ref[0], wu_ref[0]).astype(BF16)
        _tm_store(og, 0, jnp.dot(h, wd_ref[0], preferred_element_type=F32))

    def step(xg_cur, xg_next, og_cur, og_prev):
        sweep(lambda r0, p: gather_group(xg_cur, lo, r0, p), jnp.where(e == 0, 0, head), n_cur)
        for r0 in range(0, head, group):
            gather_group(xg_next, lo_next, r0, True)
        expert(xg_cur, og_cur)
        for r0 in range(0, head, group):
            scatter_group(og_prev, lo_prev, n_prev, r0, True)
        sweep(lambda r0, p: scatter_group(og_cur, lo, n_cur, r0, p), head, n_cur)

        def later_block(bi, carry):
            base = lo + bi * rows
            nrow = jnp.minimum(hi - base, rows)
            sweep(lambda r0, p: gather_group(xgo_ref, base, r0, p), 0, nrow)
            expert(xgo_ref, ogo_ref)
            sweep(lambda r0, p: scatter_group(ogo_ref, base, nrow, r0, p), 0, nrow)
            return carry

        lax.fori_loop(1, (hi - lo + rows - 1) // rows, later_block, 0)
        pl.when(e == n_e - 1)(
            lambda: sweep(lambda r0, p: scatter_group(og_cur, lo, n_cur, r0, p), 0, jnp.minimum(n_cur, head)))

    pl.when(e % 2 == 0)(lambda: step(xga_ref, xgb_ref, oga_ref, ogb_ref))
    pl.when(e % 2 == 1)(lambda: step(xgb_ref, xga_ref, ogb_ref, oga_ref))

    @pl.when(e == n_e - 1)
    def _():
        for s0 in range(0, chunk, sub):
            z = alpha * _tm_load(x_ref, s0, sub) + _tm_load(acc_ref, s0, sub)
            o_ref[s0:s0 + sub, :] = _layer_norm(z, g_ref[...], beta_ref[...])


def _moe_ln(x_tm, eidx, wts, wg, wu, wd, sg, su, sd, g, beta, alpha):
    n = x_tm.shape[0] // PIECES
    d = PIECES * LANES
    chunk = min(MOE_CHUNK, n)
    rows = MOE_ROWS
    starts, tok, gate = _dispatch(eidx, wts, chunk)
    nc = n // chunk
    a = chunk * TOP_K
    n_e, _, de = wg.shape
    full = lambda arr: pl.BlockSpec(arr.shape, lambda c, e, s_r: (0,) * arr.ndim)
    lists = pl.BlockSpec((a,), lambda c, e, s_r: (c,), memory_space=pltpu.SMEM)
    grid_spec = pltpu.PrefetchScalarGridSpec(
        num_scalar_prefetch=1,
        grid=(nc, n_e),
        in_specs=[
            lists, lists,
            pl.BlockSpec((chunk * PIECES, LANES), lambda c, e, s_r: (c, 0)),
            pl.BlockSpec((1, d, de), lambda c, e, s_r: (e, 0, 0)),
            pl.BlockSpec((1, d, de), lambda c, e, s_r: (e, 0, 0)),
            pl.BlockSpec((1, de, d), lambda c, e, s_r: (e, 0, 0)),
            full(sg), full(su), full(sd), full(g), full(beta),
        ],
        out_specs=pl.BlockSpec((chunk, d), lambda c, e, s_r: (c, 0)),
        scratch_shapes=[pltpu.VMEM(((chunk + 1) * PIECES, LANES), F32)]
        + [pltpu.VMEM((rows * PIECES, LANES), F32) for _ in range(6)],
    )
    return pl.pallas_call(
        functools.partial(_moe_kernel, chunk=chunk, rows=rows, head=min(MOE_HEAD, rows), alpha=alpha),
        grid_spec=grid_spec,
        out_shape=jax.ShapeDtypeStruct((n, d), F32),
        compiler_params=_cparams("arbitrary", "arbitrary"),
        name="moe_experts",
    )(starts, tok, gate, x_tm, wg, wu, wd, sg, su, sd, g, beta)


def _rope_tables(positions):
    pos = positions.reshape(-1).astype(F32)[:, None]
    n = pos.shape[0]

    def cs(dim):
        inv = ROPE_THETA ** (-jnp.arange(0, dim, 2, dtype=F32) / dim)
        ang = pos * inv
        return jnp.cos(ang), jnp.sin(ang)

    c, s = cs(HEAD_DIM)
    c64 = jnp.concatenate([c, c, c, c], -1)
    s64 = jnp.concatenate([-s, s, -s, s], -1)
    c, s = cs(MLA_ROPE)
    one = jnp.ones((n, MLA_NOPE), F32)
    pad = LANES - MLA_NOPE - MLA_ROPE
    cm = jnp.concatenate([one, c, c, jnp.ones((n, pad), F32)], -1)
    sm = jnp.concatenate([0 * one, -s, s, jnp.zeros((n, pad), F32)], -1)
    return (c64, s64), (cm, sm)


def _even_mixer(xf, b, s, tabs, w_in, g_qa, w_qb, g_kva, w_kvb):
    d = xf.shape[1]
    tab64, tabm = tabs
    na = 3 * 512
    kpe_lo = na + 256 + 128
    wpad = jnp.concatenate([w_in[:, :kpe_lo], jnp.zeros((d, MLA_NOPE), F32), w_in[:, kpe_lo:],
                            jnp.zeros((d, LANES - MLA_NOPE - MLA_ROPE), F32)], 1).astype(BF16)
    nh = w_qb.shape[1] // (MLA_NOPE + MLA_ROPE)
    wqb = jnp.pad(w_qb.reshape(-1, nh, MLA_NOPE + MLA_ROPE), ((0, 0), (0, 0), (0, LANES - MLA_NOPE - MLA_ROPE)))
    wqb = wqb.reshape(-1, nh * LANES).astype(BF16)
    wkv3 = w_kvb.reshape(-1, nh, MLA_NOPE + MLA_V)
    wkk = jnp.pad(wkv3[:, :, :MLA_NOPE], ((0, 0), (0, 0), (0, LANES - MLA_NOPE))).reshape(-1, nh * LANES).astype(BF16)
    wkv = wkv3[:, :, MLA_NOPE:].reshape(-1, nh * MLA_V).astype(BF16)
    segs = [(0, 512, True, QK_SCALE), (512, 1024, True, 1.0), (1024, 1536, False, 1.0)]
    qa, ka, va, qm, km, vm = _projection(
        xf, wpad, segs, tab64,
        mla_args=(tabm[0], tabm[1], g_qa.reshape(1, -1), wqb, g_kva.reshape(1, -1), wkk, wkv),
        mla_out_widths=(nh * LANES, nh * LANES, nh * MLA_V))
    o_a = _moba_attention(qa, ka, va, b, s)
    o_b = _mla_attention(qm, km, vm, b, s)
    return o_a, o_b


def _odd_mixer(xf, b, s, tabs, w_in, sinks, lam_p, g_subln, lambda_init):
    segs = [(0, 512, True, QK_SCALE), (512, 640, True, 1.0), (640, 768, False, 1.0),
            (768, 1280, True, QK_SCALE), (1280, 1792, True, 1.0), (1792, 2304, False, 1.0)]
    qc, kc, vc, qd, kd, vd = _projection(xf, w_in.astype(BF16), segs, tabs[0])
    o_c = _swa_attention(qc, kc, vc, sinks.astype(F32), b, s)
    o_d = _diff_attention(qd, kd, vd, lam_p.astype(F32), g_subln.reshape(1, -1), lambda_init, b, s)
    return o_c, o_d


def kernel(x, positions, ev_w_in, ev_w_out, mla_g_qa, mla_w_qb, mla_g_kva, mla_w_kvb, od_w_in, od_w_out, swa_sinks, diff_lambda, diff_g_subln, ln_g, ln_b, moe_w_router, moe_bias, moe_w_gate, moe_w_up, moe_w_down, sh_w_gate, sh_w_up, sh_w_down):
    b, s, d = x.shape
    depth = ln_g.shape[0]
    alpha = float((2 * depth) ** 0.25)
    xf = x.reshape(b * s, d)
    tabs = _rope_tables(positions)
    for l in range(depth):
        i = l // 2
        if l % 2 == 0:
            o1, o2 = _even_mixer(xf, b, s, tabs, ev_w_in[i], mla_g_qa[i], mla_w_qb[i], mla_g_kva[i], mla_w_kvb[i])
            w_out = ev_w_out[i]
        else:
            lambda_init = 0.8 - 0.6 * float(np.exp(-0.3 * l))
            o1, o2 = _odd_mixer(xf, b, s, tabs, od_w_in[i], swa_sinks[i], diff_lambda[i], diff_g_subln[i], lambda_init)
            w_out = od_w_out[i]
        xf, x_tm = _out_ln(xf, o1, o2, w_out.astype(BF16), ln_g[l, 0].reshape(1, d), ln_b[l, 0].reshape(1, d), alpha)
        eidx, wts = _router(xf, moe_w_router[l], moe_bias[l])
        xf = _moe_ln(x_tm, eidx, wts, moe_w_gate[l].astype(BF16), moe_w_up[l].astype(BF16),
                     moe_w_down[l].astype(BF16), sh_w_gate[l].astype(BF16), sh_w_up[l].astype(BF16),
                     sh_w_down[l].astype(BF16), ln_g[l, 1].reshape(1, d), ln_b[l, 1].reshape(1, d), alpha)
    return xf.reshape(b, s, d)
```

```python
import functools

import numpy as np
import jax
import jax.numpy as jnp
from jax import lax
from jax.experimental import pallas as pl
from jax.experimental.pallas import tpu as pltpu

F32 = jnp.float32
BF16 = jnp.bfloat16
I32 = jnp.int32

LANES = 128
NEG = -1e30

HEAD_DIM = 64
MOBA_BLOCK = 256
MOBA_TOPK = 3
MLA_NOPE = 64
MLA_ROPE = 32
MLA_V = 64
SWA_WINDOW = 128
N_EXPERTS = 64
N_GROUPS = 8
TOPK_GROUPS = 4
TOP_K = 8
ROUTED_SCALE = 2.5
ROPE_THETA = 10000.0
QK_SCALE = HEAD_DIM ** -0.5
MLA_SCALE = (MLA_NOPE + MLA_ROPE) ** -0.5

PROJ_ROWS = 512
ATTN_TILE = 1024
SWA_ROWS = 512
ROUTER_ROWS = 512
MOE_CHUNK = 2048
MOE_ROWS = 320
MOE_HEAD = 256
VMEM_LIMIT = 56 * 1024 * 1024

_NT = (((1,), (1,)), ((), ()))


def _cparams(*sem):
    return pltpu.CompilerParams(dimension_semantics=sem, vmem_limit_bytes=VMEM_LIMIT)


def _lane_iota(rows):
    return lax.broadcasted_iota(I32, (rows, LANES), 1)


def _rope(x, c, s, first_half, half):
    rot = jnp.where(first_half, pltpu.roll(x, LANES - half, 1), pltpu.roll(x, half, 1))
    return x * c + rot * s


def _rms(x, g, eps):
    return x * lax.rsqrt(jnp.mean(x * x, -1, keepdims=True) + eps) * g


def _proj_kernel(segs, mla, x_ref, w_ref, c64_ref, s64_ref, *refs):
    xb = x_ref[...].astype(BF16)
    rows = xb.shape[0]
    lane = _lane_iota(rows)
    fh64 = (lane % HEAD_DIM) < HEAD_DIM // 2
    c64 = c64_ref[...]
    s64 = s64_ref[...]

    def seg(lo, hi):
        return jnp.dot(xb, w_ref[:, lo:hi], preferred_element_type=F32)

    if mla:
        cm_ref, sm_ref, gq_ref, wqb_ref, gkv_ref, wkk_ref, wkv_ref = refs[:7]
        refs = refs[7:]
    for (lo, hi, rope, scale), o_ref in zip(segs, refs):
        h = seg(lo, hi)
        if rope:
            for g in range((hi - lo) // LANES):
                sl = slice(g * LANES, (g + 1) * LANES)
                o_ref[:, sl] = (_rope(h[:, sl], c64, s64, fh64, HEAD_DIM // 2) * scale).astype(BF16)
        else:
            o_ref[...] = h.astype(BF16)
    if mla:
        qm_ref, km_ref, vm_ref = refs[len(segs):]
        cm = cm_ref[...]
        sm = sm_ref[...]
        fhm = (lane >= MLA_NOPE) & (lane < MLA_NOPE + MLA_ROPE // 2)
        lo = segs[-1][1]
        qn = _rms(seg(lo, lo + 256), gq_ref[...], 1e-6).astype(BF16)
        qm = jnp.dot(qn, wqb_ref[...], preferred_element_type=F32)
        kvn = _rms(seg(lo + 256, lo + 384), gkv_ref[...], 1e-6).astype(BF16)
        kk = jnp.dot(kvn, wkk_ref[...], preferred_element_type=F32)
        kpe = _rope(seg(lo + 384, lo + 512), cm, sm, fhm, MLA_ROPE // 2)
        for g in range(qm.shape[1] // LANES):
            sl = slice(g * LANES, (g + 1) * LANES)
            qm_ref[:, sl] = (_rope(qm[:, sl], cm, sm, fhm, MLA_ROPE // 2) * MLA_SCALE).astype(BF16)
            km_ref[:, sl] = (kk[:, sl] + kpe).astype(BF16)
        vm_ref[...] = jnp.dot(kvn, wkv_ref[...], preferred_element_type=F32).astype(BF16)


def _projection(xf, w, segs, tabs, mla_args=None, mla_out_widths=()):
    n, d = xf.shape
    tm = min(PROJ_ROWS, n)
    row = lambda width: pl.BlockSpec((tm, width), lambda i: (i, 0))
    full = lambda a: pl.BlockSpec(a.shape, lambda i: (0,) * a.ndim)
    ins = [xf, w, tabs[0], tabs[1]]
    in_specs = [row(d), full(w), row(LANES), row(LANES)]
    if mla_args is not None:
        cm, sm, gq, wqb, gkv, wkk, wkv = mla_args
        ins += [cm, sm, gq, wqb, gkv, wkk, wkv]
        in_specs += [row(LANES), row(LANES), full(gq), full(wqb), full(gkv), full(wkk), full(wkv)]
    widths = [hi - lo for lo, hi, _, _ in segs] + list(mla_out_widths)
    return pl.pallas_call(
        functools.partial(_proj_kernel, tuple(segs), mla_args is not None),
        grid=(n // tm,),
        in_specs=in_specs,
        out_specs=[row(wd) for wd in widths],
        out_shape=[jax.ShapeDtypeStruct((n, wd), BF16) for wd in widths],
        compiler_params=_cparams("parallel"),
        name="projection",
    )(*ins)


def _flash_update(c, q, k, v, bias, state, first):
    m_ref, l_ref, acc_ref = state
    s = lax.dot_general(q, k, _NT, preferred_element_type=F32)
    if bias is not None:
        s = s + bias
    t, tk = s.shape
    mx = jnp.max(s, -1, keepdims=True)
    if first:
        m_new = jnp.broadcast_to(mx, (t, LANES))
    else:
        m_prev = m_ref[c]
        m_new = jnp.maximum(m_prev, mx)
        alpha = jnp.exp(m_prev - m_new)
    ps, psum = [], None
    for j in range(tk // LANES):
        pj = jnp.exp(s[:, j * LANES:(j + 1) * LANES] - m_new)
        ps.append(pj.astype(BF16))
        psum = pj if psum is None else psum + pj
    pv = jnp.dot(jnp.concatenate(ps, axis=1), v, preferred_element_type=F32)
    if first:
        l_ref[c] = psum
        acc_ref[c] = pv
    else:
        l_ref[c] = alpha * l_ref[c] + psum
        acc_ref[c] = alpha * acc_ref[c] + pv
    m_ref[c] = m_new


def _flash_out(c, state):
    _, l_ref, acc_ref = state
    return acc_ref[c] / jnp.sum(l_ref[c], -1, keepdims=True)


def _flash_state(chains, t):
    return [pltpu.VMEM((chains, t, LANES), F32) for _ in range(3)]


def _causal_sweep(qi, step):
    step(qi, True)

    def body(j, carry):
        step(j, False)
        return carry

    lax.fori_loop(0, qi, body, 0)


def _causal_bias(t, block=None):
    r = lax.broadcasted_iota(I32, (t, t), 0)
    c = lax.broadcasted_iota(I32, (t, t), 1)
    hide = c > r
    if block is not None:
        hide = hide & ((r // block) == (c // block))
    return jnp.where(hide, NEG, 0.0).astype(F32)


def _mla_kernel(q_ref, k_ref, v_ref, o_ref, m_ref, l_ref, acc_ref, *, t):
    qi = pl.program_id(2)
    dbias = _causal_bias(t)
    state = (m_ref, l_ref, acc_ref)

    def step(j, first):
        ks = pl.multiple_of(j * t, t)
        v = v_ref[pl.ds(ks, t), :]
        for h in range(2):
            hs = slice(h * LANES, (h + 1) * LANES)
            _flash_update(h, q_ref[:, hs], k_ref[pl.ds(ks, t), hs], v, dbias if first else None, state, first)

    _causal_sweep(qi, step)
    lane = _lane_iota(t)
    o_ref[...] = jnp.where(lane < MLA_V, _flash_out(0, state), _flash_out(1, state)).astype(BF16)


def _mla_attention(qm, km, vm, b, s):
    t = min(ATTN_TILE, s)
    nq = s // t
    hp = qm.shape[1] // (2 * LANES)
    return pl.pallas_call(
        functools.partial(_mla_kernel, t=t),
        grid=(b, hp, nq),
        in_specs=[pl.BlockSpec((t, 2 * LANES), lambda bi, h, qi: (bi * nq + qi, h)),
                  pl.BlockSpec((s, 2 * LANES), lambda bi, h, qi: (bi, h)),
                  pl.BlockSpec((s, LANES), lambda bi, h, qi: (bi, h))],
        out_specs=pl.BlockSpec((t, LANES), lambda bi, h, qi: (bi * nq + qi, h)),
        out_shape=jax.ShapeDtypeStruct((b * s, hp * LANES), BF16),
        scratch_shapes=_flash_state(2, t),
        compiler_params=_cparams("parallel", "parallel", "arbitrary"),
        name="mla_attention",
    )(qm, km, vm)


def _moba_kernel(q_ref, k_ref, v_ref, o_ref, kmean_ref, kaug_ref, qaug_ref, m_ref, l_ref, acc_ref, *, t, nb):
    qi = pl.program_id(2)
    bs = MOBA_BLOCK
    s_len = k_ref.shape[0]
    half = LANES // 2

    @pl.when(qi == 0)
    def _():
        kmean_ref[...] = jnp.zeros_like(kmean_ref)
        for n in range(nb):
            blk = k_ref[n * bs:(n + 1) * bs, :].astype(F32)
            kmean_ref[n:n + 1, :] = jnp.sum(blk, 0, keepdims=True) * (1.0 / bs)
        lane_s = _lane_iota(s_len)
        row_blk = lax.broadcasted_iota(I32, (s_len, LANES), 0) // bs
        kf = k_ref[...].astype(F32)
        kaug_ref[0] = jnp.where(lane_s < half, kf, (lane_s - half == row_blk).astype(F32)).astype(BF16)
        kaug_ref[1] = jnp.where(lane_s >= half, kf, (lane_s == row_blk).astype(F32)).astype(BF16)

    lane = _lane_iota(t)
    own = qi * (t // bs) + lax.broadcasted_iota(I32, (t, LANES), 0) // bs
    qf = q_ref[...].astype(F32)
    kmean = kmean_ref[...]
    for h in range(2):
        head = (lane >= h * half) & (lane < (h + 1) * half)
        qh = jnp.where(head, qf, 0.0)
        gate = lax.dot_general(qh, kmean, _NT, preferred_element_type=F32, precision=lax.Precision.HIGHEST)
        g = jnp.where(lane < own, gate, -jnp.inf)
        sel = lane == own
        for _ in range(MOBA_TOPK):
            mx = jnp.max(g, -1, keepdims=True)
            idx = jnp.min(jnp.where(g == mx, lane, LANES), -1, keepdims=True)
            hit = lane == idx
            sel = sel | (hit & (lane < own))
            g = jnp.where(hit, -jnp.inf, g)
        selb = jnp.where(sel | (lane >= nb), 0.0, NEG).astype(F32)
        if h == 0:
            selb = pltpu.roll(selb, half, 1)
        qaug_ref[h] = jnp.where(head, qf, selb).astype(BF16)

    dbias = _causal_bias(t, bs)
    state = (m_ref, l_ref, acc_ref)

    def step(j, first):
        ks = pl.multiple_of(j * t, t)
        v = v_ref[pl.ds(ks, t), :]
        for h in range(2):
            _flash_update(h, qaug_ref[h], kaug_ref[h, pl.ds(ks, t), :], v, dbias if first else None, state, first)

    _causal_sweep(qi, step)
    o_ref[...] = jnp.where(lane < half, _flash_out(0, state), _flash_out(1, state)).astype(BF16)


def _moba_attention(q, k, v, b, s):
    bs = MOBA_BLOCK
    nb = s // bs
    assert nb <= LANES // 2, "block gate must fit the spare half of the lanes"
    t = min(ATTN_TILE, s)
    nq = s // t
    hp = q.shape[1] // LANES
    return pl.pallas_call(
        functools.partial(_moba_kernel, t=t, nb=nb),
        grid=(b, hp, nq),
        in_specs=[pl.BlockSpec((t, LANES), lambda bi, h, qi: (bi * nq + qi, h)),
                  pl.BlockSpec((s, LANES), lambda bi, h, qi: (bi, h)),
                  pl.BlockSpec((s, LANES), lambda bi, h, qi: (bi, h))],
        out_specs=pl.BlockSpec((t, LANES), lambda bi, h, qi: (bi * nq + qi, h)),
        out_shape=jax.ShapeDtypeStruct((b * s, hp * LANES), BF16),
        scratch_shapes=[pltpu.VMEM((LANES, LANES), F32), pltpu.VMEM((2, s, LANES), BF16),
                        pltpu.VMEM((2, t, LANES), BF16)] + _flash_state(2, t),
        compiler_params=_cparams("parallel", "parallel", "arbitrary"),
        name="moba_attention",
    )(q, k, v)


def _diff_kernel(q_ref, k_ref, v_ref, lam_ref, g_ref, o_ref, qs_ref, m_ref, l_ref, acc_ref, *, t, lambda_init):
    qi = pl.program_id(2)
    lane = _lane_iota(t)
    qf = q_ref[...].astype(F32)
    qs_ref[0] = jnp.where(lane < HEAD_DIM, qf, 0.0).astype(BF16)
    qs_ref[1] = jnp.where(lane >= HEAD_DIM, qf, 0.0).astype(BF16)
    dbias = _causal_bias(t)
    state = (m_ref, l_ref, acc_ref)

    def step(j, first):
        ks = pl.multiple_of(j * t, t)
        k = k_ref[pl.ds(ks, t), :]
        v = v_ref[pl.ds(ks, t), :]
        for c in range(2):
            _flash_update(c, qs_ref[c], k, v, dbias if first else None, state, first)

    _causal_sweep(qi, step)
    lp = lam_ref[...]
    lam = (jnp.exp(jnp.sum(lp[0:1] * lp[1:2], -1, keepdims=True))
           - jnp.exp(jnp.sum(lp[2:3] * lp[3:4], -1, keepdims=True)) + lambda_init)
    o = _flash_out(0, state) - lam * _flash_out(1, state)
    o_ref[...] = (_rms(o, g_ref[...], 1e-5) * (1.0 - lambda_init)).astype(BF16)


def _diff_attention(q, k, v, lam_p, g_subln, lambda_init, b, s):
    t = min(ATTN_TILE, s)
    nq = s // t
    nh = q.shape[1] // LANES
    return pl.pallas_call(
        functools.partial(_diff_kernel, t=t, lambda_init=lambda_init),
        grid=(b, nh, nq),
        in_specs=[pl.BlockSpec((t, LANES), lambda bi, h, qi: (bi * nq + qi, h)),
                  pl.BlockSpec((s, LANES), lambda bi, h, qi: (bi, h)),
                  pl.BlockSpec((s, LANES), lambda bi, h, qi: (bi, h)),
                  pl.BlockSpec(lam_p.shape, lambda bi, h, qi: (0, 0)),
                  pl.BlockSpec(g_subln.shape, lambda bi, h, qi: (0, 0))],
        out_specs=pl.BlockSpec((t, LANES), lambda bi, h, qi: (bi * nq + qi, h)),
        out_shape=jax.ShapeDtypeStruct((b * s, nh * LANES), BF16),
        scratch_shapes=[pltpu.VMEM((2, t, LANES), BF16)] + _flash_state(2, t),
        compiler_params=_cparams("parallel", "parallel", "arbitrary"),
        name="diff_attention",
    )(q, k, v, lam_p, g_subln)


def _swa_kernel(sink_ref, q_ref, k_ref, v_ref, o_ref, *, rows):
    ti = pl.program_id(1)
    w = SWA_WINDOW
    nsub = rows // w
    n_q = q_ref.shape[1] // HEAD_DIM
    n_kv = k_ref.shape[1] // HEAD_DIM
    grp = n_q // n_kv
    lane_k = lax.broadcasted_iota(I32, (2 * w, LANES), 1)
    lane_o = _lane_iota(w)
    qrow = lax.broadcasted_iota(I32, (grp * w, 2 * w), 0) % w
    kcol = lax.broadcasted_iota(I32, (grp * w, 2 * w), 1)
    for i in range(nsub):
        sb = ti * nsub + i
        start = pl.multiple_of(jnp.maximum(sb - 1, 0) * w, w)
        kb = k_ref[pl.ds(start, 2 * w), :]
        vb = v_ref[pl.ds(start, 2 * w), :]
        rel = qrow + (sb * w - start) - kcol
        bias = jnp.where((rel >= 0) & (rel < w), 0.0, NEG).astype(F32)
        qf = q_ref[i * w:(i + 1) * w, :].astype(F32)
        head_out = [None] * n_q
        for kvh in range(n_kv):
            km = jnp.where(lane_k // HEAD_DIM == kvh, kb.astype(F32), 0.0).astype(BF16)
            qs, sinks = [], []
            for a in range(grp):
                h = kvh * grp + a
                qp = qf[:, (h // 2) * LANES:(h // 2 + 1) * LANES]
                if h % 2 != kvh:
                    qp = pltpu.roll(qp, HEAD_DIM, 1)
                qs.append(qp.astype(BF16))
                sinks.append(jnp.full((w, 1), sink_ref[h], F32))
            sink = jnp.concatenate(sinks, 0)
            s = lax.dot_general(jnp.concatenate(qs, 0), km, _NT, preferred_element_type=F32) + bias
            m = jnp.maximum(jnp.max(s, -1, keepdims=True), sink)
            p = jnp.exp(s - m)
            den = jnp.sum(p, -1, keepdims=True) + jnp.exp(sink - m)
            o = jnp.dot(p.astype(BF16), vb, preferred_element_type=F32) / den
            for a in range(grp):
                h = kvh * grp + a
                oh = o[a * w:(a + 1) * w, :]
                if h % 2 != kvh:
                    oh = pltpu.roll(oh, HEAD_DIM, 1)
                head_out[h] = oh
        for hp in range(n_q // 2):
            pair = jnp.where(lane_o < HEAD_DIM, head_out[2 * hp], head_out[2 * hp + 1])
            o_ref[i * w:(i + 1) * w, hp * LANES:(hp + 1) * LANES] = pair.astype(BF16)


def _swa_attention(q, k, v, sinks, b, s):
    rows = min(SWA_ROWS, s)
    nt = s // rows
    return pl.pallas_call(
        functools.partial(_swa_kernel, rows=rows),
        grid=(b, nt),
        in_specs=[pl.BlockSpec(memory_space=pltpu.SMEM),
                  pl.BlockSpec((rows, q.shape[1]), lambda bi, ti: (bi * nt + ti, 0)),
                  pl.BlockSpec((s, k.shape[1]), lambda bi, ti: (bi, 0)),
                  pl.BlockSpec((s, v.shape[1]), lambda bi, ti: (bi, 0))],
        out_specs=pl.BlockSpec((rows, q.shape[1]), lambda bi, ti: (bi * nt + ti, 0)),
        out_shape=jax.ShapeDtypeStruct(q.shape, BF16),
        compiler_params=_cparams("parallel", "arbitrary"),
        name="swa_attention",
    )(sinks, q, k, v)


def _layer_norm(y, g, b):
    mu = jnp.mean(y, -1, keepdims=True)
    yc = y - mu
    var = jnp.mean(yc * yc, -1, keepdims=True)
    return yc * lax.rsqrt(var + 1e-5) * g + b


PIECES = 8


def _tm_load(ref, r0, nrows):
    return jnp.concatenate([ref[pl.ds(r0 * PIECES + p, nrows, stride=PIECES), :] for p in range(PIECES)], axis=1)


def _tm_store(ref, r0, val):
    for p in range(PIECES):
        ref[pl.ds(r0 * PIECES + p, val.shape[0], stride=PIECES), :] = val[:, p * LANES:(p + 1) * LANES]


def _out_ln_kernel(x_ref, a_ref, b_ref, w_ref, g_ref, beta_ref, o_ref, otm_ref, *, alpha):
    ka = a_ref.shape[1]
    mix = (jnp.dot(a_ref[...], w_ref[:ka, :], preferred_element_type=F32)
           + jnp.dot(b_ref[...], w_ref[ka:, :], preferred_element_type=F32))
    y = _layer_norm(alpha * x_ref[...] + mix, g_ref[...], beta_ref[...])
    o_ref[...] = y
    _tm_store(otm_ref, 0, y)


def _out_ln(xf, a, b, w, g, beta, alpha):
    n, d = xf.shape
    assert d == PIECES * LANES
    tm = min(PROJ_ROWS, n)
    row = lambda width: pl.BlockSpec((tm, width), lambda i: (i, 0))
    full = lambda arr: pl.BlockSpec(arr.shape, lambda i: (0, 0))
    return pl.pallas_call(
        functools.partial(_out_ln_kernel, alpha=alpha),
        grid=(n // tm,),
        in_specs=[row(d), row(a.shape[1]), row(b.shape[1]), full(w), full(g), full(beta)],
        out_specs=[row(d), pl.BlockSpec((tm * PIECES, LANES), lambda i: (i, 0))],
        out_shape=[jax.ShapeDtypeStruct((n, d), F32), jax.ShapeDtypeStruct((n * PIECES, LANES), F32)],
        compiler_params=_cparams("parallel"),
        name="out_proj_ln",
    )(xf, a, b, w, g, beta)


def _row_argmax(v, row, nrow):
    mx = jnp.max(v, 0, keepdims=True)
    idx = jnp.min(jnp.where(v == mx, row, nrow), 0, keepdims=True)
    return mx, idx


def _router_kernel(x_ref, w_ref, bias_ref, eidx_ref, wts_ref):
    logits = jnp.dot(x_ref[...], w_ref[...], preferred_element_type=F32, precision=lax.Precision.HIGHEST)
    tm = logits.shape[0]
    scores = jax.nn.sigmoid(logits.T[:N_EXPERTS, :])
    biased = scores + bias_ref[...]
    per = N_EXPERTS // N_GROUPS
    row_g = lax.broadcasted_iota(I32, (per, tm), 0)
    gsc = []
    for g in range(N_GROUPS):
        v = biased[g * per:(g + 1) * per, :]
        m1, i1 = _row_argmax(v, row_g, per)
        m2 = jnp.max(jnp.where(row_g == i1, -jnp.inf, v), 0, keepdims=True)
        gsc.append(m1 + m2)
    gscore = jnp.concatenate(gsc, 0)
    row_n = lax.broadcasted_iota(I32, (N_GROUPS, tm), 0)
    gok = jnp.zeros((N_GROUPS, tm), F32)
    for _ in range(TOPK_GROUPS):
        _, ig = _row_argmax(gscore, row_n, N_GROUPS)
        hit = row_n == ig
        gok = jnp.where(hit, 1.0, gok)
        gscore = jnp.where(hit, -jnp.inf, gscore)
    cand = jnp.concatenate([jnp.where(gok[g:g + 1, :] > 0.0, biased[g * per:(g + 1) * per, :], -jnp.inf)
                            for g in range(N_GROUPS)], 0)
    row_e = lax.broadcasted_iota(I32, (N_EXPERTS, tm), 0)
    ids, ws = [], []
    for _ in range(TOP_K):
        _, ie = _row_argmax(cand, row_e, N_EXPERTS)
        hit = row_e == ie
        ids.append(ie)
        ws.append(jnp.sum(jnp.where(hit, scores, 0.0), 0, keepdims=True))
        cand = jnp.where(hit, -jnp.inf, cand)
    w = jnp.concatenate(ws, 0)
    eidx_ref[...] = jnp.concatenate(ids, 0)
    wts_ref[...] = w / jnp.sum(w, 0, keepdims=True) * ROUTED_SCALE


def _router(xf, w_router, e_bias):
    n, d = xf.shape
    tm = min(ROUTER_ROWS, n)
    wpad = jnp.pad(w_router, ((0, 0), (0, LANES - N_EXPERTS)))
    bcol = e_bias.astype(F32).reshape(N_EXPERTS, 1)
    out = pl.BlockSpec((TOP_K, tm), lambda i: (0, i))
    return pl.pallas_call(
        _router_kernel,
        grid=(n // tm,),
        in_specs=[pl.BlockSpec((tm, d), lambda i: (i, 0)), pl.BlockSpec(wpad.shape, lambda i: (0, 0)),
                  pl.BlockSpec(bcol.shape, lambda i: (0, 0))],
        out_specs=[out, out],
        out_shape=[jax.ShapeDtypeStruct((TOP_K, n), I32), jax.ShapeDtypeStruct((TOP_K, n), F32)],
        compiler_params=_cparams("parallel"),
        name="router",
    )(xf, wpad, bcol)


def _dispatch(eidx_t, wts_t, chunk, tail):
    n = eidx_t.shape[1]
    nc = n // chunk
    a = chunk * TOP_K
    ec = eidx_t.reshape(TOP_K, nc, chunk).transpose(1, 0, 2).reshape(nc, a)
    wc = wts_t.reshape(TOP_K, nc, chunk).transpose(1, 0, 2).reshape(nc, a)
    ex = jnp.arange(N_EXPERTS, dtype=I32)
    counts = jnp.sum((ec[:, :, None] == ex[None, None, :]).astype(I32), axis=1)
    ends = jnp.cumsum(counts, axis=1).astype(I32)
    starts = jnp.concatenate([jnp.zeros((nc, 1), I32), ends], axis=1)
    mult = 1 << int(chunk - 1).bit_length()
    tok_ids = jnp.broadcast_to(jnp.arange(a, dtype=I32)[None, :] % chunk, (nc, a))
    keys, gates = lax.sort((ec * mult + tok_ids, wc), dimension=1, num_keys=1)
    rows = jnp.pad((keys % mult) * PIECES, ((0, 0), (0, tail)))
    gates = jnp.pad(gates, ((0, 0), (0, tail)))
    return starts.reshape(-1), rows.reshape(-1), gates.reshape(-1)


def _swiglu(xb, wg, wu):
    hg = jnp.dot(xb, wg, preferred_element_type=F32)
    hu = jnp.dot(xb, wu, preferred_element_type=F32)
    return hg * jax.nn.sigmoid(hg) * hu


def _moe_kernel(start_ref, tok_ref, gate_ref, x_ref, wg_ref, wu_ref, wd_ref, sg_ref, su_ref, sd_ref,
                g_ref, beta_ref, o_ref, acc_ref, xga_ref, xgb_ref, oga_ref, ogb_ref, xgo_ref, ogo_ref,
                *, chunk, rows, head, alpha):
    c = pl.program_id(0)
    e = pl.program_id(1)
    n_e = pl.num_programs(1)
    group = 8
    sub = min(512, chunk)

    @pl.when((c == 0) & (e == 0))
    def _():
        for ref in (xga_ref, xgb_ref, xgo_ref):
            ref[...] = jnp.zeros_like(ref)

    @pl.when(e == 0)
    def _():
        for s0 in range(0, chunk, sub):
            xb = _tm_load(x_ref, s0, sub).astype(BF16)
            h = _swiglu(xb, sg_ref[...], su_ref[...]).astype(BF16)
            _tm_store(acc_ref, s0, jnp.dot(h, sd_ref[...], preferred_element_type=F32))
        acc_ref[chunk * PIECES:(chunk + 1) * PIECES, :] = jnp.zeros((PIECES, LANES), F32)

    def run(ei):
        return start_ref[c * (n_e + 1) + ei], start_ref[c * (n_e + 1) + ei + 1]

    lo, hi = run(e)
    n_cur = jnp.minimum(hi - lo, rows)
    lo_next, _ = run(jnp.minimum(e + 1, n_e - 1))
    lo_prev, hi_prev = run(jnp.maximum(e - 1, 0))
    n_prev = jnp.where(e > 0, jnp.minimum(hi_prev - lo_prev, rows), 0)

    def at_row(first):
        return pl.ds(pl.multiple_of(first, PIECES), PIECES)

    def gather_group(xg, base, r0, _):
        for i in range(group):
            xg[at_row((r0 + i) * PIECES), :] = x_ref[at_row(tok_ref[base + r0 + i]), :]

    def scatter_group(og, base, nrow, r0, mask):
        firsts, vals = [], []
        for i in range(group):
            idx = base + r0 + i
            first = tok_ref[idx]
            if mask:
                first = jnp.where(r0 + i < nrow, first, chunk * PIECES)
            firsts.append(first)
            vals.append(acc_ref[at_row(first), :] + gate_ref[idx] * og[at_row((r0 + i) * PIECES), :])
        for i in range(group):
            acc_ref[at_row(firsts[i]), :] = vals[i]

    def sweep(fn, r_lo, r_hi):
        n = jnp.maximum(r_hi - r_lo, 0)
        nfull = n // group

        def body(gi, carry):
            fn(pl.multiple_of(r_lo + gi * group, group), False)
            return carry

        lax.fori_loop(0, nfull, body, 0)
        pl.when(n % group != 0)(lambda: fn(pl.multiple_of(r_lo + nfull * group, group), True))

    def expert(xg, og):
        xb = _tm_load(xg, 0, rows).astype(BF16)
        h = _swiglu(xb, wg_ref[0], wu_ref[0]).astype(BF16)
        _tm_store(og, 0, jnp.dot(h, wd_ref[0], preferred_element_type=F32))

    def step(xg_cur, xg_next, og_cur, og_prev):
        sweep(lambda r0, p: gather_group(xg_cur, lo, r0, p), jnp.where(e == 0, 0, head), n_cur)
        for r0 in range(0, head, group):
            gather_group(xg_next, lo_next, r0, True)
        expert(xg_cur, og_cur)
        for r0 in range(0, head, group):
            scatter_group(og_prev, lo_prev, n_prev, r0, True)
        sweep(lambda r0, p: scatter_group(og_cur, lo, n_cur, r0, p), head, n_cur)

        def later_block(bi, carry):
            base = lo + bi * rows
            nrow = jnp.minimum(hi - base, rows)
            sweep(lambda r0, p: gather_group(xgo_ref, base, r0, p), 0, nrow)
            expert(xgo_ref, ogo_ref)
            sweep(lambda r0, p: scatter_group(ogo_ref, base, nrow, r0, p), 0, nrow)
            return carry

        lax.fori_loop(1, (hi - lo + rows - 1) // rows, later_block, 0)
        pl.when(e == n_e - 1)(
            lambda: sweep(lambda r0, p: scatter_group(og_cur, lo, n_cur, r0, p), 0, jnp.minimum(n_cur, head)))

    pl.when(e % 2 == 0)(lambda: step(xga_ref, xgb_ref, oga_ref, ogb_ref))
    pl.when(e % 2 == 1)(lambda: step(xgb_ref, xga_ref, ogb_ref, oga_ref))

    @pl.when(e == n_e - 1)
    def _():
        for s0 in range(0, chunk, sub):
            z = alpha * _tm_load(x_ref, s0, sub) + _tm_load(acc_ref, s0, sub)
            o_ref[s0:s0 + sub, :] = _layer_norm(z, g_ref[...], beta_ref[...])


def _moe_ln(x_tm, eidx, wts, wg, wu, wd, sg, su, sd, g, beta, alpha):
    n = x_tm.shape[0] // PIECES
    d = PIECES * LANES
    chunk = min(MOE_CHUNK, n)
    rows = MOE_ROWS
    tail = -(-rows // 1024) * 1024
    starts, tok, gate = _dispatch(eidx, wts, chunk, tail)
    nc = n // chunk
    n_e, _, de = wg.shape
    full = lambda arr: pl.BlockSpec(arr.shape, lambda c, e, s_r: (0,) * arr.ndim)
    lists = pl.BlockSpec((chunk * TOP_K + tail,), lambda c, e, s_r: (c,), memory_space=pltpu.SMEM)
    grid_spec = pltpu.PrefetchScalarGridSpec(
        num_scalar_prefetch=1,
        grid=(nc, n_e),
        in_specs=[
            lists, lists,
            pl.BlockSpec((chunk * PIECES, LANES), lambda c, e, s_r: (c, 0)),
            pl.BlockSpec((1, d, de), lambda c, e, s_r: (e, 0, 0)),
            pl.BlockSpec((1, d, de), lambda c, e, s_r: (e, 0, 0)),
            pl.BlockSpec((1, de, d), lambda c, e, s_r: (e, 0, 0)),
            full(sg), full(su), full(sd), full(g), full(beta),
        ],
        out_specs=pl.BlockSpec((chunk, d), lambda c, e, s_r: (c, 0)),
        scratch_shapes=[pltpu.VMEM(((chunk + 1) * PIECES, LANES), F32)]
        + [pltpu.VMEM((rows * PIECES, LANES), F32) for _ in range(6)],
    )
    return pl.pallas_call(
        functools.partial(_moe_kernel, chunk=chunk, rows=rows, head=min(MOE_HEAD, rows), alpha=alpha),
        grid_spec=grid_spec,
        out_shape=jax.ShapeDtypeStruct((n, d), F32),
        compiler_params=_cparams("arbitrary", "arbitrary"),
        name="moe_experts",
    )(starts, tok, gate, x_tm, wg, wu, wd, sg, su, sd, g, beta)


def _rope_tables(positions):
    pos = positions.reshape(-1).astype(F32)[:, None]
    n = pos.shape[0]

    def cs(dim):
        inv = ROPE_THETA ** (-jnp.arange(0, dim, 2, dtype=F32) / dim)
        ang = pos * inv
        return jnp.cos(ang), jnp.sin(ang)

    c, s = cs(HEAD_DIM)
    c64 = jnp.concatenate([c, c, c, c], -1)
    s64 = jnp.concatenate([-s, s, -s, s], -1)
    c, s = cs(MLA_ROPE)
    one = jnp.ones((n, MLA_NOPE), F32)
    pad = LANES - MLA_NOPE - MLA_ROPE
    cm = jnp.concatenate([one, c, c, jnp.ones((n, pad), F32)], -1)
    sm = jnp.concatenate([0 * one, -s, s, jnp.zeros((n, pad), F32)], -1)
    return (c64, s64), (cm, sm)


def _even_mixer(xf, b, s, tabs, w_in, g_qa, w_qb, g_kva, w_kvb):
    d = xf.shape[1]
    tab64, tabm = tabs
    na = 3 * 512
    kpe_lo = na + 256 + 128
    wpad = jnp.concatenate([w_in[:, :kpe_lo], jnp.zeros((d, MLA_NOPE), F32), w_in[:, kpe_lo:],
                            jnp.zeros((d, LANES - MLA_NOPE - MLA_ROPE), F32)], 1).astype(BF16)
    nh = w_qb.shape[1] // (MLA_NOPE + MLA_ROPE)
    wqb = jnp.pad(w_qb.reshape(-1, nh, MLA_NOPE + MLA_ROPE), ((0, 0), (0, 0), (0, LANES - MLA_NOPE - MLA_ROPE)))
    wqb = wqb.reshape(-1, nh * LANES).astype(BF16)
    wkv3 = w_kvb.reshape(-1, nh, MLA_NOPE + MLA_V)
    wkk = jnp.pad(wkv3[:, :, :MLA_NOPE], ((0, 0), (0, 0), (0, LANES - MLA_NOPE))).reshape(-1, nh * LANES).astype(BF16)
    wkv = wkv3[:, :, MLA_NOPE:].reshape(-1, nh * MLA_V).astype(BF16)
    segs = [(0, 512, True, QK_SCALE), (512, 1024, True, 1.0), (1024, 1536, False, 1.0)]
    qa, ka, va, qm, km, vm = _projection(
        xf, wpad, segs, tab64,
        mla_args=(tabm[0], tabm[1], g_qa.reshape(1, -1), wqb, g_kva.reshape(1, -1), wkk, wkv),
        mla_out_widths=(nh * LANES, nh * LANES, nh * MLA_V))
    o_a = _moba_attention(qa, ka, va, b, s)
    o_b = _mla_attention(qm, km, vm, b, s)
    return o_a, o_b


def _odd_mixer(xf, b, s, tabs, w_in, sinks, lam_p, g_subln, lambda_init):
    segs = [(0, 512, True, QK_SCALE), (512, 640, True, 1.0), (640, 768, False, 1.0),
            (768, 1280, True, QK_SCALE), (1280, 1792, True, 1.0), (1792, 2304, False, 1.0)]
    qc, kc, vc, qd, kd, vd = _projection(xf, w_in.astype(BF16), segs, tabs[0])
    o_c = _swa_attention(qc, kc, vc, sinks.astype(F32), b, s)
    o_d = _diff_attention(qd, kd, vd, lam_p.astype(F32), g_subln.reshape(1, -1), lambda_init, b, s)
    return o_c, o_d


def kernel(x, positions, ev_w_in, ev_w_out, mla_g_qa, mla_w_qb, mla_g_kva, mla_w_kvb, od_w_in, od_w_out, swa_sinks, diff_lambda, diff_g_subln, ln_g, ln_b, moe_w_router, moe_bias, moe_w_gate, moe_w_up, moe_w_down, sh_w_gate, sh_w_up, sh_w_down):
    b, s, d = x.shape
    depth = ln_g.shape[0]
    alpha = float((2 * depth) ** 0.25)
    xf = x.reshape(b * s, d)
    tabs = _rope_tables(positions)
    for l in range(depth):
        i = l // 2
        if l % 2 == 0:
            o1, o2 = _even_mixer(xf, b, s, tabs, ev_w_in[i], mla_g_qa[i], mla_w_qb[i], mla_g_kva[i], mla_w_kvb[i])
            w_out = ev_w_out[i]
        else:
            lambda_init = 0.8 - 0.6 * float(np.exp(-0.3 * l))
            o1, o2 = _odd_mixer(xf, b, s, tabs, od_w_in[i], swa_sinks[i], diff_lambda[i], diff_g_subln[i], lambda_init)
            w_out = od_w_out[i]
        xf, x_tm = _out_ln(xf, o1, o2, w_out.astype(BF16), ln_g[l, 0].reshape(1, d), ln_b[l, 0].reshape(1, d), alpha)
        eidx, wts = _router(xf, moe_w_router[l], moe_bias[l])
        xf = _moe_ln(x_tm, eidx, wts, moe_w_gate[l].astype(BF16), moe_w_up[l].astype(BF16),
                     moe_w_down[l].astype(BF16), sh_w_gate[l].astype(BF16), sh_w_up[l].astype(BF16),
                     sh_w_down[l].astype(BF16), ln_g[l, 1].reshape(1, d), ln_b[l, 1].reshape(1, d), alpha)
    return xf.reshape(b, s, d)
```

```python
import functools

import numpy as np
import jax
import jax.numpy as jnp
from jax import lax
from jax.experimental import pallas as pl
from jax.experimental.pallas import tpu as pltpu

F32 = jnp.float32
BF16 = jnp.bfloat16
I32 = jnp.int32

LANES = 128
NEG = -1e30

HEAD_DIM = 64
MOBA_BLOCK = 256
MOBA_TOPK = 3
MLA_NOPE = 64
MLA_ROPE = 32
MLA_V = 64
SWA_WINDOW = 128
N_EXPERTS = 64
N_GROUPS = 8
TOPK_GROUPS = 4
TOP_K = 8
ROUTED_SCALE = 2.5
ROPE_THETA = 10000.0
QK_SCALE = HEAD_DIM ** -0.5
MLA_SCALE = (MLA_NOPE + MLA_ROPE) ** -0.5

PROJ_ROWS = 512
ATTN_TILE = 1024
SWA_ROWS = 512
ROUTER_ROWS = 512
MOE_CHUNK = 2048
MOE_ROWS = 320
MOE_HEAD = 256
VMEM_LIMIT = 56 * 1024 * 1024

_NT = (((1,), (1,)), ((), ()))


def _cparams(*sem):
    return pltpu.CompilerParams(dimension_semantics=sem, vmem_limit_bytes=VMEM_LIMIT)


def _lane_iota(rows):
    return lax.broadcasted_iota(I32, (rows, LANES), 1)


def _rope(x, c, s, first_half, half):
    rot = jnp.where(first_half, pltpu.roll(x, LANES - half, 1), pltpu.roll(x, half, 1))
    return x * c + rot * s


def _rms(x, g, eps):
    return x * lax.rsqrt(jnp.mean(x * x, -1, keepdims=True) + eps) * g


def _proj_kernel(segs, mla, x_ref, w_ref, c64_ref, s64_ref, *refs):
    xb = x_ref[...].astype(BF16)
    rows = xb.shape[0]
    lane = _lane_iota(rows)
    fh64 = (lane % HEAD_DIM) < HEAD_DIM // 2
    c64 = c64_ref[...]
    s64 = s64_ref[...]

    def seg(lo, hi):
        return jnp.dot(xb, w_ref[:, lo:hi], preferred_element_type=F32)

    if mla:
        cm_ref, sm_ref, gq_ref, wqb_ref, gkv_ref, wkk_ref, wkv_ref = refs[:7]
        refs = refs[7:]
    for (lo, hi, rope, scale), o_ref in zip(segs, refs):
        h = seg(lo, hi)
        if rope:
            for g in range((hi - lo) // LANES):
                sl = slice(g * LANES, (g + 1) * LANES)
                o_ref[:, sl] = (_rope(h[:, sl], c64, s64, fh64, HEAD_DIM // 2) * scale).astype(BF16)
        else:
            o_ref[...] = h.astype(BF16)
    if mla:
        qm_ref, km_ref, vm_ref = refs[len(segs):]
        cm = cm_ref[...]
        sm = sm_ref[...]
        fhm = (lane >= MLA_NOPE) & (lane < MLA_NOPE + MLA_ROPE // 2)
        lo = segs[-1][1]
        qn = _rms(seg(lo, lo + 256), gq_ref[...], 1e-6).astype(BF16)
        qm = jnp.dot(qn, wqb_ref[...], preferred_element_type=F32)
        kvn = _rms(seg(lo + 256, lo + 384), gkv_ref[...], 1e-6).astype(BF16)
        kk = jnp.dot(kvn, wkk_ref[...], preferred_element_type=F32)
        kpe = _rope(seg(lo + 384, lo + 512), cm, sm, fhm, MLA_ROPE // 2)
        for g in range(qm.shape[1] // LANES):
            sl = slice(g * LANES, (g + 1) * LANES)
            qm_ref[:, sl] = (_rope(qm[:, sl], cm, sm, fhm, MLA_ROPE // 2) * MLA_SCALE).astype(BF16)
            km_ref[:, sl] = (kk[:, sl] + kpe).astype(BF16)
        vm_ref[...] = jnp.dot(kvn, wkv_ref[...], preferred_element_type=F32).astype(BF16)


def _projection(xf, w, segs, tabs, mla_args=None, mla_out_widths=()):
    n, d = xf.shape
    tm = min(PROJ_ROWS, n)
    row = lambda width: pl.BlockSpec((tm, width), lambda i: (i, 0))
    full = lambda a: pl.BlockSpec(a.shape, lambda i: (0,) * a.ndim)
    ins = [xf, w, tabs[0], tabs[1]]
    in_specs = [row(d), full(w), row(LANES), row(LANES)]
    if mla_args is not None:
        cm, sm, gq, wqb, gkv, wkk, wkv = mla_args
        ins += [cm, sm, gq, wqb, gkv, wkk, wkv]
        in_specs += [row(LANES), row(LANES), full(gq), full(wqb), full(gkv), full(wkk), full(wkv)]
    widths = [hi - lo for lo, hi, _, _ in segs] + list(mla_out_widths)
    return pl.pallas_call(
        functools.partial(_proj_kernel, tuple(segs), mla_args is not None),
        grid=(n // tm,),
        in_specs=in_specs,
        out_specs=[row(wd) for wd in widths],
        out_shape=[jax.ShapeDtypeStruct((n, wd), BF16) for wd in widths],
        compiler_params=_cparams("parallel"),
        name="projection",
    )(*ins)


def _flash_update(c, q, k, v, bias, state, first):
    m_ref, l_ref, acc_ref = state
    s = lax.dot_general(q, k, _NT, preferred_element_type=F32)
    if bias is not None:
        s = s + bias
    t, tk = s.shape
    mx = jnp.max(s, -1, keepdims=True)
    if first:
        m_new = jnp.broadcast_to(mx, (t, LANES))
    else:
        m_prev = m_ref[c]
        m_new = jnp.maximum(m_prev, mx)
        alpha = jnp.exp(m_prev - m_new)
    ps, psum = [], None
    for j in range(tk // LANES):
        pj = jnp.exp(s[:, j * LANES:(j + 1) * LANES] - m_new)
        ps.append(pj.astype(BF16))
        psum = pj if psum is None else psum + pj
    pv = jnp.dot(jnp.concatenate(ps, axis=1), v, preferred_element_type=F32)
    if first:
        l_ref[c] = psum
        acc_ref[c] = pv
    else:
        l_ref[c] = alpha * l_ref[c] + psum
        acc_ref[c] = alpha * acc_ref[c] + pv
    m_ref[c] = m_new


def _flash_out(c, state):
    _, l_ref, acc_ref = state
    return acc_ref[c] / jnp.sum(l_ref[c], -1, keepdims=True)


def _flash_state(chains, t):
    return [pltpu.VMEM((chains, t, LANES), F32) for _ in range(3)]


def _causal_sweep(qi, step):
    step(qi, True)

    def body(j, carry):
        step(j, False)
        return carry

    lax.fori_loop(0, qi, body, 0)


def _causal_bias(t, block=None):
    r = lax.broadcasted_iota(I32, (t, t), 0)
    c = lax.broadcasted_iota(I32, (t, t), 1)
    hide = c > r
    if block is not None:
        hide = hide & ((r // block) == (c // block))
    return jnp.where(hide, NEG, 0.0).astype(F32)


def _mla_kernel(q_ref, k_ref, v_ref, o_ref, m_ref, l_ref, acc_ref, *, t):
    qi = pl.program_id(2)
    dbias = _causal_bias(t)
    state = (m_ref, l_ref, acc_ref)

    def step(j, first):
        ks = pl.multiple_of(j * t, t)
        v = v_ref[pl.ds(ks, t), :]
        for h in range(2):
            hs = slice(h * LANES, (h + 1) * LANES)
            _flash_update(h, q_ref[:, hs], k_ref[pl.ds(ks, t), hs], v, dbias if first else None, state, first)

    _causal_sweep(qi, step)
    lane = _lane_iota(t)
    o_ref[...] = jnp.where(lane < MLA_V, _flash_out(0, state), _flash_out(1, state)).astype(BF16)


def _mla_attention(qm, km, vm, b, s):
    t = min(ATTN_TILE, s)
    nq = s // t
    hp = qm.shape[1] // (2 * LANES)
    return pl.pallas_call(
        functools.partial(_mla_kernel, t=t),
        grid=(b, hp, nq),
        in_specs=[pl.BlockSpec((t, 2 * LANES), lambda bi, h, qi: (bi * nq + qi, h)),
                  pl.BlockSpec((s, 2 * LANES), lambda bi, h, qi: (bi, h)),
                  pl.BlockSpec((s, LANES), lambda bi, h, qi: (bi, h))],
        out_specs=pl.BlockSpec((t, LANES), lambda bi, h, qi: (bi * nq + qi, h)),
        out_shape=jax.ShapeDtypeStruct((b * s, hp * LANES), BF16),
        scratch_shapes=_flash_state(2, t),
        compiler_params=_cparams("parallel", "parallel", "arbitrary"),
        name="mla_attention",
    )(qm, km, vm)


def _moba_kernel(q_ref, k_ref, v_ref, o_ref, kmean_ref, kaug_ref, qaug_ref, m_ref, l_ref, acc_ref, *, t, nb):
    qi = pl.program_id(2)
    bs = MOBA_BLOCK
    s_len = k_ref.shape[0]
    half = LANES // 2

    @pl.when(qi == 0)
    def _():
        kmean_ref[...] = jnp.zeros_like(kmean_ref)
        for n in range(nb):
            blk = k_ref[n * bs:(n + 1) * bs, :].astype(F32)
            kmean_ref[n:n + 1, :] = jnp.sum(blk, 0, keepdims=True) * (1.0 / bs)
        lane_s = _lane_iota(s_len)
        row_blk = lax.broadcasted_iota(I32, (s_len, LANES), 0) // bs
        kf = k_ref[...].astype(F32)
        kaug_ref[0] = jnp.where(lane_s < half, kf, (lane_s - half == row_blk).astype(F32)).astype(BF16)
        kaug_ref[1] = jnp.where(lane_s >= half, kf, (lane_s == row_blk).astype(F32)).astype(BF16)

    lane = _lane_iota(t)
    nbp = -(-nb // 8) * 8
    blk = lax.broadcasted_iota(I32, (nbp, t), 0)
    own = qi * (t // bs) + lax.broadcasted_iota(I32, (nbp, t), 1) // bs
    qf = q_ref[...].astype(F32)
    kmean = kmean_ref[0:nbp, :]
    for h in range(2):
        head = (lane >= h * half) & (lane < (h + 1) * half)
        qh = jnp.where(head, qf, 0.0)
        gate = lax.dot_general(kmean, qh, _NT, preferred_element_type=F32, precision=lax.Precision.HIGHEST)
        g = jnp.where(blk < own, gate, -jnp.inf)
        sel = blk == own
        for _ in range(MOBA_TOPK):
            mx = jnp.max(g, 0, keepdims=True)
            idx = jnp.min(jnp.where(g == mx, blk, nbp), 0, keepdims=True)
            hit = blk == idx
            sel = sel | (hit & (blk < own))
            g = jnp.where(hit, -jnp.inf, g)
        selb = jnp.where(sel | (blk >= nb), 0.0, NEG).astype(F32)
        selb = jnp.concatenate([selb, jnp.zeros((LANES - nbp, t), F32)], 0).T
        if h == 0:
            selb = pltpu.roll(selb, half, 1)
        qaug_ref[h] = jnp.where(head, qf, selb).astype(BF16)

    dbias = _causal_bias(t, bs)
    state = (m_ref, l_ref, acc_ref)

    def step(j, first):
        ks = pl.multiple_of(j * t, t)
        v = v_ref[pl.ds(ks, t), :]
        for h in range(2):
            _flash_update(h, qaug_ref[h], kaug_ref[h, pl.ds(ks, t), :], v, dbias if first else None, state, first)

    _causal_sweep(qi, step)
    o_ref[...] = jnp.where(lane < half, _flash_out(0, state), _flash_out(1, state)).astype(BF16)


def _moba_attention(q, k, v, b, s):
    bs = MOBA_BLOCK
    nb = s // bs
    assert nb <= LANES // 2, "block gate must fit the spare half of the lanes"
    t = min(ATTN_TILE, s)
    nq = s // t
    hp = q.shape[1] // LANES
    return pl.pallas_call(
        functools.partial(_moba_kernel, t=t, nb=nb),
        grid=(b, hp, nq),
        in_specs=[pl.BlockSpec((t, LANES), lambda bi, h, qi: (bi * nq + qi, h)),
                  pl.BlockSpec((s, LANES), lambda bi, h, qi: (bi, h)),
                  pl.BlockSpec((s, LANES), lambda bi, h, qi: (bi, h))],
        out_specs=pl.BlockSpec((t, LANES), lambda bi, h, qi: (bi * nq + qi, h)),
        out_shape=jax.ShapeDtypeStruct((b * s, hp * LANES), BF16),
        scratch_shapes=[pltpu.VMEM((LANES, LANES), F32), pltpu.VMEM((2, s, LANES), BF16),
                        pltpu.VMEM((2, t, LANES), BF16)] + _flash_state(2, t),
        compiler_params=_cparams("parallel", "parallel", "arbitrary"),
        name="moba_attention",
    )(q, k, v)


def _diff_kernel(q_ref, k_ref, v_ref, lam_ref, g_ref, o_ref, qs_ref, m_ref, l_ref, acc_ref, *, t, lambda_init):
    qi = pl.program_id(2)
    lane = _lane_iota(t)
    qf = q_ref[...].astype(F32)
    qs_ref[0] = jnp.where(lane < HEAD_DIM, qf, 0.0).astype(BF16)
    qs_ref[1] = jnp.where(lane >= HEAD_DIM, qf, 0.0).astype(BF16)
    dbias = _causal_bias(t)
    state = (m_ref, l_ref, acc_ref)

    def step(j, first):
        ks = pl.multiple_of(j * t, t)
        k = k_ref[pl.ds(ks, t), :]
        v = v_ref[pl.ds(ks, t), :]
        for c in range(2):
            _flash_update(c, qs_ref[c], k, v, dbias if first else None, state, first)

    _causal_sweep(qi, step)
    lp = lam_ref[...]
    lam = (jnp.exp(jnp.sum(lp[0:1] * lp[1:2], -1, keepdims=True))
           - jnp.exp(jnp.sum(lp[2:3] * lp[3:4], -1, keepdims=True)) + lambda_init)
    o = _flash_out(0, state) - lam * _flash_out(1, state)
    o_ref[...] = (_rms(o, g_ref[...], 1e-5) * (1.0 - lambda_init)).astype(BF16)


def _diff_attention(q, k, v, lam_p, g_subln, lambda_init, b, s):
    t = min(ATTN_TILE, s)
    nq = s // t
    nh = q.shape[1] // LANES
    return pl.pallas_call(
        functools.partial(_diff_kernel, t=t, lambda_init=lambda_init),
        grid=(b, nh, nq),
        in_specs=[pl.BlockSpec((t, LANES), lambda bi, h, qi: (bi * nq + qi, h)),
                  pl.BlockSpec((s, LANES), lambda bi, h, qi: (bi, h)),
                  pl.BlockSpec((s, LANES), lambda bi, h, qi: (bi, h)),
                  pl.BlockSpec(lam_p.shape, lambda bi, h, qi: (0, 0)),
                  pl.BlockSpec(g_subln.shape, lambda bi, h, qi: (0, 0))],
        out_specs=pl.BlockSpec((t, LANES), lambda bi, h, qi: (bi * nq + qi, h)),
        out_shape=jax.ShapeDtypeStruct((b * s, nh * LANES), BF16),
        scratch_shapes=[pltpu.VMEM((2, t, LANES), BF16)] + _flash_state(2, t),
        compiler_params=_cparams("parallel", "parallel", "arbitrary"),
        name="diff_attention",
    )(q, k, v, lam_p, g_subln)


def _swa_kernel(sink_ref, q_ref, k_ref, v_ref, o_ref, *, rows):
    ti = pl.program_id(1)
    w = SWA_WINDOW
    nsub = rows // w
    n_q = q_ref.shape[1] // HEAD_DIM
    n_kv = k_ref.shape[1] // HEAD_DIM
    grp = n_q // n_kv
    lane_k = lax.broadcasted_iota(I32, (2 * w, LANES), 1)
    lane_o = _lane_iota(w)
    qrow = lax.broadcasted_iota(I32, (grp * w, 2 * w), 0) % w
    kcol = lax.broadcasted_iota(I32, (grp * w, 2 * w), 1)
    for i in range(nsub):
        sb = ti * nsub + i
        start = pl.multiple_of(jnp.maximum(sb - 1, 0) * w, w)
        kb = k_ref[pl.ds(start, 2 * w), :]
        vb = v_ref[pl.ds(start, 2 * w), :]
        rel = qrow + (sb * w - start) - kcol
        bias = jnp.where((rel >= 0) & (rel < w), 0.0, NEG).astype(F32)
        qf = q_ref[i * w:(i + 1) * w, :].astype(F32)
        head_out = [None] * n_q
        for kvh in range(n_kv):
            km = jnp.where(lane_k // HEAD_DIM == kvh, kb.astype(F32), 0.0).astype(BF16)
            qs, sinks = [], []
            for a in range(grp):
                h = kvh * grp + a
                qp = qf[:, (h // 2) * LANES:(h // 2 + 1) * LANES]
                if h % 2 != kvh:
                    qp = pltpu.roll(qp, HEAD_DIM, 1)
                qs.append(qp.astype(BF16))
                sinks.append(jnp.full((w, 1), sink_ref[h], F32))
            sink = jnp.concatenate(sinks, 0)
            s = lax.dot_general(jnp.concatenate(qs, 0), km, _NT, preferred_element_type=F32) + bias
            m = jnp.maximum(jnp.max(s, -1, keepdims=True), sink)
            p = jnp.exp(s - m)
            den = jnp.sum(p, -1, keepdims=True) + jnp.exp(sink - m)
            o = jnp.dot(p.astype(BF16), vb, preferred_element_type=F32) / den
            for a in range(grp):
                h = kvh * grp + a
                oh = o[a * w:(a + 1) * w, :]
                if h % 2 != kvh:
                    oh = pltpu.roll(oh, HEAD_DIM, 1)
                head_out[h] = oh
        for hp in range(n_q // 2):
            pair = jnp.where(lane_o < HEAD_DIM, head_out[2 * hp], head_out[2 * hp + 1])
            o_ref[i * w:(i + 1) * w, hp * LANES:(hp + 1) * LANES] = pair.astype(BF16)


def _swa_attention(q, k, v, sinks, b, s):
    rows = min(SWA_ROWS, s)
    nt = s // rows
    return pl.pallas_call(
        functools.partial(_swa_kernel, rows=rows),
        grid=(b, nt),
        in_specs=[pl.BlockSpec(memory_space=pltpu.SMEM),
                  pl.BlockSpec((rows, q.shape[1]), lambda bi, ti: (bi * nt + ti, 0)),
                  pl.BlockSpec((s, k.shape[1]), lambda bi, ti: (bi, 0)),
                  pl.BlockSpec((s, v.shape[1]), lambda bi, ti: (bi, 0))],
        out_specs=pl.BlockSpec((rows, q.shape[1]), lambda bi, ti: (bi * nt + ti, 0)),
        out_shape=jax.ShapeDtypeStruct(q.shape, BF16),
        compiler_params=_cparams("parallel", "arbitrary"),
        name="swa_attention",
    )(sinks, q, k, v)


def _layer_norm(y, g, b):
    mu = jnp.mean(y, -1, keepdims=True)
    yc = y - mu
    var = jnp.mean(yc * yc, -1, keepdims=True)
    return yc * lax.rsqrt(var + 1e-5) * g + b


PIECES = 8


def _tm_load(ref, r0, nrows):
    return jnp.concatenate([ref[pl.ds(r0 * PIECES + p, nrows, stride=PIECES), :] for p in range(PIECES)], axis=1)


def _tm_store(ref, r0, val):
    for p in range(PIECES):
        ref[pl.ds(r0 * PIECES + p, val.shape[0], stride=PIECES), :] = val[:, p * LANES:(p + 1) * LANES]


def _out_ln_kernel(x_ref, a_ref, b_ref, w_ref, g_ref, beta_ref, otm_ref, *, alpha):
    ka = a_ref.shape[1]
    mix = (jnp.dot(a_ref[...], w_ref[:ka, :], preferred_element_type=F32)
           + jnp.dot(b_ref[...], w_ref[ka:, :], preferred_element_type=F32))
    _tm_store(otm_ref, 0, _layer_norm(alpha * x_ref[...] + mix, g_ref[...], beta_ref[...]))


def _out_ln(xf, a, b, w, g, beta, alpha):
    n, d = xf.shape
    assert d == PIECES * LANES
    tm = min(PROJ_ROWS, n)
    row = lambda width: pl.BlockSpec((tm, width), lambda i: (i, 0))
    full = lambda arr: pl.BlockSpec(arr.shape, lambda i: (0, 0))
    return pl.pallas_call(
        functools.partial(_out_ln_kernel, alpha=alpha),
        grid=(n // tm,),
        in_specs=[row(d), row(a.shape[1]), row(b.shape[1]), full(w), full(g), full(beta)],
        out_specs=pl.BlockSpec((tm * PIECES, LANES), lambda i: (i, 0)),
        out_shape=jax.ShapeDtypeStruct((n * PIECES, LANES), F32),
        compiler_params=_cparams("parallel"),
        name="out_proj_ln",
    )(xf, a, b, w, g, beta)


def _row_argmax(v, row, nrow):
    mx = jnp.max(v, 0, keepdims=True)
    idx = jnp.min(jnp.where(v == mx, row, nrow), 0, keepdims=True)
    return mx, idx


def _router_kernel(x_ref, w_ref, bias_ref, eidx_ref, wts_ref):
    tm = x_ref.shape[0] // PIECES
    logits = jnp.dot(_tm_load(x_ref, 0, tm), w_ref[...], preferred_element_type=F32,
                     precision=lax.Precision.HIGHEST)
    scores = jax.nn.sigmoid(logits.T[:N_EXPERTS, :])
    biased = scores + bias_ref[...]
    per = N_EXPERTS // N_GROUPS
    row_g = lax.broadcasted_iota(I32, (per, tm), 0)
    gsc = []
    for g in range(N_GROUPS):
        v = biased[g * per:(g + 1) * per, :]
        m1, i1 = _row_argmax(v, row_g, per)
        m2 = jnp.max(jnp.where(row_g == i1, -jnp.inf, v), 0, keepdims=True)
        gsc.append(m1 + m2)
    gscore = jnp.concatenate(gsc, 0)
    row_n = lax.broadcasted_iota(I32, (N_GROUPS, tm), 0)
    gok = jnp.zeros((N_GROUPS, tm), F32)
    for _ in range(TOPK_GROUPS):
        _, ig = _row_argmax(gscore, row_n, N_GROUPS)
        hit = row_n == ig
        gok = jnp.where(hit, 1.0, gok)
        gscore = jnp.where(hit, -jnp.inf, gscore)
    cand = jnp.concatenate([jnp.where(gok[g:g + 1, :] > 0.0, biased[g * per:(g + 1) * per, :], -jnp.inf)
                            for g in range(N_GROUPS)], 0)
    row_e = lax.broadcasted_iota(I32, (N_EXPERTS, tm), 0)
    ids, ws = [], []
    for _ in range(TOP_K):
        _, ie = _row_argmax(cand, row_e, N_EXPERTS)
        hit = row_e == ie
        ids.append(ie)
        ws.append(jnp.sum(jnp.where(hit, scores, 0.0), 0, keepdims=True))
        cand = jnp.where(hit, -jnp.inf, cand)
    w = jnp.concatenate(ws, 0)
    eidx_ref[...] = jnp.concatenate(ids, 0)
    wts_ref[...] = w / jnp.sum(w, 0, keepdims=True) * ROUTED_SCALE


def _router(x_tm, w_router, e_bias):
    n = x_tm.shape[0] // PIECES
    tm = min(ROUTER_ROWS, n)
    wpad = jnp.pad(w_router, ((0, 0), (0, LANES - N_EXPERTS)))
    bcol = e_bias.astype(F32).reshape(N_EXPERTS, 1)
    out = pl.BlockSpec((TOP_K, tm), lambda i: (0, i))
    return pl.pallas_call(
        _router_kernel,
        grid=(n // tm,),
        in_specs=[pl.BlockSpec((tm * PIECES, LANES), lambda i: (i, 0)), pl.BlockSpec(wpad.shape, lambda i: (0, 0)),
                  pl.BlockSpec(bcol.shape, lambda i: (0, 0))],
        out_specs=[out, out],
        out_shape=[jax.ShapeDtypeStruct((TOP_K, n), I32), jax.ShapeDtypeStruct((TOP_K, n), F32)],
        compiler_params=_cparams("parallel"),
        name="router",
    )(x_tm, wpad, bcol)


def _dispatch(eidx_t, wts_t, chunk, tail):
    n = eidx_t.shape[1]
    nc = n // chunk
    a = chunk * TOP_K
    ec = eidx_t.reshape(TOP_K, nc, chunk).transpose(1, 0, 2).reshape(nc, a)
    wc = wts_t.reshape(TOP_K, nc, chunk).transpose(1, 0, 2).reshape(nc, a)
    ex = jnp.arange(N_EXPERTS, dtype=I32)
    counts = jnp.sum((ec[:, :, None] == ex[None, None, :]).astype(I32), axis=1)
    ends = jnp.cumsum(counts, axis=1).astype(I32)
    starts = jnp.concatenate([jnp.zeros((nc, 1), I32), ends], axis=1)
    mult = 1 << int(chunk - 1).bit_length()
    tok_ids = jnp.broadcast_to(jnp.arange(a, dtype=I32)[None, :] % chunk, (nc, a))
    keys, gates = lax.sort((ec * mult + tok_ids, wc), dimension=1, num_keys=1)
    rows = jnp.pad((keys % mult) * PIECES, ((0, 0), (0, tail)))
    gates = jnp.pad(gates, ((0, 0), (0, tail)))
    return starts.reshape(-1), rows.reshape(-1), gates.reshape(-1)


def _swiglu(xb, wg, wu):
    hg = jnp.dot(xb, wg, preferred_element_type=F32)
    hu = jnp.dot(xb, wu, preferred_element_type=F32)
    return hg * jax.nn.sigmoid(hg) * hu


def _moe_kernel(start_ref, tok_ref, gate_ref, x_ref, wg_ref, wu_ref, wd_ref, sg_ref, su_ref, sd_ref,
                g_ref, beta_ref, o_ref, acc_ref, xga_ref, xgb_ref, oga_ref, ogb_ref, xgo_ref, ogo_ref,
                *, chunk, rows, head, alpha):
    c = pl.program_id(0)
    e = pl.program_id(1)
    n_e = pl.num_programs(1)
    group = 8
    sub = min(512, chunk)

    @pl.when((c == 0) & (e == 0))
    def _():
        for ref in (xga_ref, xgb_ref, xgo_ref):
            ref[...] = jnp.zeros_like(ref)

    @pl.when(e == 0)
    def _():
        for s0 in range(0, chunk, sub):
            xb = _tm_load(x_ref, s0, sub).astype(BF16)
            h = _swiglu(xb, sg_ref[...], su_ref[...]).astype(BF16)
            _tm_store(acc_ref, s0, jnp.dot(h, sd_ref[...], preferred_element_type=F32))
        acc_ref[chunk * PIECES:(chunk + 1) * PIECES, :] = jnp.zeros((PIECES, LANES), F32)

    def run(ei):
        return start_ref[c * (n_e + 1) + ei], start_ref[c * (n_e + 1) + ei + 1]

    lo, hi = run(e)
    n_cur = jnp.minimum(hi - lo, rows)
    lo_next, _ = run(jnp.minimum(e + 1, n_e - 1))
    lo_prev, hi_prev = run(jnp.maximum(e - 1, 0))
    n_prev = jnp.where(e > 0, jnp.minimum(hi_prev - lo_prev, rows), 0)

    def at_row(first):
        return pl.ds(pl.multiple_of(first, PIECES), PIECES)

    def gather_group(xg, base, r0, _):
        for i in range(group):
            xg[at_row((r0 + i) * PIECES), :] = x_ref[at_row(tok_ref[base + r0 + i]), :]

    def scatter_group(og, base, nrow, r0, mask):
        firsts, vals = [], []
        for i in range(group):
            idx = base + r0 + i
            first = tok_ref[idx]
            if mask:
                first = jnp.where(r0 + i < nrow, first, chunk * PIECES)
            firsts.append(first)
            vals.append(acc_ref[at_row(first), :] + gate_ref[idx] * og[at_row((r0 + i) * PIECES), :])
        for i in range(group):
            acc_ref[at_row(firsts[i]), :] = vals[i]

    def sweep(fn, r_lo, r_hi):
        n = jnp.maximum(r_hi - r_lo, 0)
        nfull = n // group

        def body(gi, carry):
            fn(pl.multiple_of(r_lo + gi * group, group), False)
            return carry

        lax.fori_loop(0, nfull, body, 0)
        pl.when(n % group != 0)(lambda: fn(pl.multiple_of(r_lo + nfull * group, group), True))

    def expert(xg, og):
        xb = _tm_load(xg, 0, rows).astype(BF16)
        h = _swiglu(xb, wg_ref[0], wu_ref[0]).astype(BF16)
        _tm_store(og, 0, jnp.dot(h, wd_ref[0], preferred_element_type=F32))

    def step(xg_cur, xg_next, og_cur, og_prev):
        sweep(lambda r0, p: gather_group(xg_cur, lo, r0, p), jnp.where(e == 0, 0, head), n_cur)
        for r0 in range(0, head, group):
            gather_group(xg_next, lo_next, r0, True)
        expert(xg_cur, og_cur)
        for r0 in range(0, head, group):
            scatter_group(og_prev, lo_prev, n_prev, r0, True)
        sweep(lambda r0, p: scatter_group(og_cur, lo, n_cur, r0, p), head, n_cur)

        def later_block(bi, carry):
            base = lo + bi * rows
            nrow = jnp.minimum(hi - base, rows)
            sweep(lambda r0, p: gather_group(xgo_ref, base, r0, p), 0, nrow)
            expert(xgo_ref, ogo_ref)
            sweep(lambda r0, p: scatter_group(ogo_ref, base, nrow, r0, p), 0, nrow)
            return carry

        lax.fori_loop(1, (hi - lo + rows - 1) // rows, later_block, 0)
        pl.when(e == n_e - 1)(
            lambda: sweep(lambda r0, p: scatter_group(og_cur, lo, n_cur, r0, p), 0, jnp.minimum(n_cur, head)))

    pl.when(e % 2 == 0)(lambda: step(xga_ref, xgb_ref, oga_ref, ogb_ref))
    pl.when(e % 2 == 1)(lambda: step(xgb_ref, xga_ref, ogb_ref, oga_ref))

    @pl.when(e == n_e - 1)
    def _():
        for s0 in range(0, chunk, sub):
            z = alpha * _tm_load(x_ref, s0, sub) + _tm_load(acc_ref, s0, sub)
            o_ref[s0:s0 + sub, :] = _layer_norm(z, g_ref[...], beta_ref[...])


def _moe_ln(x_tm, eidx, wts, wg, wu, wd, sg, su, sd, g, beta, alpha):
    n = x_tm.shape[0] // PIECES
    d = PIECES * LANES
    chunk = min(MOE_CHUNK, n)
    rows = MOE_ROWS
    tail = -(-rows // 1024) * 1024
    starts, tok, gate = _dispatch(eidx, wts, chunk, tail)
    nc = n // chunk
    n_e, _, de = wg.shape
    full = lambda arr: pl.BlockSpec(arr.shape, lambda c, e, s_r: (0,) * arr.ndim)
    lists = pl.BlockSpec((chunk * TOP_K + tail,), lambda c, e, s_r: (c,), memory_space=pltpu.SMEM)
    grid_spec = pltpu.PrefetchScalarGridSpec(
        num_scalar_prefetch=1,
        grid=(nc, n_e),
        in_specs=[
            lists, lists,
            pl.BlockSpec((chunk * PIECES, LANES), lambda c, e, s_r: (c, 0)),
            pl.BlockSpec((1, d, de), lambda c, e, s_r: (e, 0, 0)),
            pl.BlockSpec((1, d, de), lambda c, e, s_r: (e, 0, 0)),
            pl.BlockSpec((1, de, d), lambda c, e, s_r: (e, 0, 0)),
            full(sg), full(su), full(sd), full(g), full(beta),
        ],
        out_specs=pl.BlockSpec((chunk, d), lambda c, e, s_r: (c, 0)),
        scratch_shapes=[pltpu.VMEM(((chunk + 1) * PIECES, LANES), F32)]
        + [pltpu.VMEM((rows * PIECES, LANES), F32) for _ in range(6)],
    )
    return pl.pallas_call(
        functools.partial(_moe_kernel, chunk=chunk, rows=rows, head=min(MOE_HEAD, rows), alpha=alpha),
        grid_spec=grid_spec,
        out_shape=jax.ShapeDtypeStruct((n, d), F32),
        compiler_params=_cparams("arbitrary", "arbitrary"),
        name="moe_experts",
    )(starts, tok, gate, x_tm, wg, wu, wd, sg, su, sd, g, beta)


def _rope_tables(positions):
    pos = positions.reshape(-1).astype(F32)[:, None]
    n = pos.shape[0]

    def cs(dim):
        inv = ROPE_THETA ** (-jnp.arange(0, dim, 2, dtype=F32) / dim)
        ang = pos * inv
        return jnp.cos(ang), jnp.sin(ang)

    c, s = cs(HEAD_DIM)
    c64 = jnp.concatenate([c, c, c, c], -1)
    s64 = jnp.concatenate([-s, s, -s, s], -1)
    c, s = cs(MLA_ROPE)
    one = jnp.ones((n, MLA_NOPE), F32)
    pad = LANES - MLA_NOPE - MLA_ROPE
    cm = jnp.concatenate([one, c, c, jnp.ones((n, pad), F32)], -1)
    sm = jnp.concatenate([0 * one, -s, s, jnp.zeros((n, pad), F32)], -1)
    return (c64, s64), (cm, sm)


def _even_mixer(xf, b, s, tabs, w_in, g_qa, w_qb, g_kva, w_kvb):
    d = xf.shape[1]
    tab64, tabm = tabs
    na = 3 * 512
    kpe_lo = na + 256 + 128
    wpad = jnp.concatenate([w_in[:, :kpe_lo], jnp.zeros((d, MLA_NOPE), F32), w_in[:, kpe_lo:],
                            jnp.zeros((d, LANES - MLA_NOPE - MLA_ROPE), F32)], 1).astype(BF16)
    nh = w_qb.shape[1] // (MLA_NOPE + MLA_ROPE)
    wqb = jnp.pad(w_qb.reshape(-1, nh, MLA_NOPE + MLA_ROPE), ((0, 0), (0, 0), (0, LANES - MLA_NOPE - MLA_ROPE)))
    wqb = wqb.reshape(-1, nh * LANES).astype(BF16)
    wkv3 = w_kvb.reshape(-1, nh, MLA_NOPE + MLA_V)
    wkk = jnp.pad(wkv3[:, :, :MLA_NOPE], ((0, 0), (0, 0), (0, LANES - MLA_NOPE))).reshape(-1, nh * LANES).astype(BF16)
    wkv = wkv3[:, :, MLA_NOPE:].reshape(-1, nh * MLA_V).astype(BF16)
    segs = [(0, 512, True, QK_SCALE), (512, 1024, True, 1.0), (1024, 1536, False, 1.0)]
    qa, ka, va, qm, km, vm = _projection(
        xf, wpad, segs, tab64,
        mla_args=(tabm[0], tabm[1], g_qa.reshape(1, -1), wqb, g_kva.reshape(1, -1), wkk, wkv),
        mla_out_widths=(nh * LANES, nh * LANES, nh * MLA_V))
    o_a = _moba_attention(qa, ka, va, b, s)
    o_b = _mla_attention(qm, km, vm, b, s)
    return o_a, o_b


def _odd_mixer(xf, b, s, tabs, w_in, sinks, lam_p, g_subln, lambda_init):
    segs = [(0, 512, True, QK_SCALE), (512, 640, True, 1.0), (640, 768, False, 1.0),
            (768, 1280, True, QK_SCALE), (1280, 1792, True, 1.0), (1792, 2304, False, 1.0)]
    qc, kc, vc, qd, kd, vd = _projection(xf, w_in.astype(BF16), segs, tabs[0])
    o_c = _swa_attention(qc, kc, vc, sinks.astype(F32), b, s)
    o_d = _diff_attention(qd, kd, vd, lam_p.astype(F32), g_subln.reshape(1, -1), lambda_init, b, s)
    return o_c, o_d


def kernel(x, positions, ev_w_in, ev_w_out, mla_g_qa, mla_w_qb, mla_g_kva, mla_w_kvb, od_w_in, od_w_out, swa_sinks, diff_lambda, diff_g_subln, ln_g, ln_b, moe_w_router, moe_bias, moe_w_gate, moe_w_up, moe_w_down, sh_w_gate, sh_w_up, sh_w_down):
    b, s, d = x.shape
    depth = ln_g.shape[0]
    alpha = float((2 * depth) ** 0.25)
    xf = x.reshape(b * s, d)
    tabs = _rope_tables(positions)
    for l in range(depth):
        i = l // 2
        if l % 2 == 0:
            o1, o2 = _even_mixer(xf, b, s, tabs, ev_w_in[i], mla_g_qa[i], mla_w_qb[i], mla_g_kva[i], mla_w_kvb[i])
            w_out = ev_w_out[i]
        else:
            lambda_init = 0.8 - 0.6 * float(np.exp(-0.3 * l))
            o1, o2 = _odd_mixer(xf, b, s, tabs, od_w_in[i], swa_sinks[i], diff_lambda[i], diff_g_subln[i], lambda_init)
            w_out = od_w_out[i]
        x_tm = _out_ln(xf, o1, o2, w_out.astype(BF16), ln_g[l, 0].reshape(1, d), ln_b[l, 0].reshape(1, d), alpha)
        eidx, wts = _router(x_tm, moe_w_router[l], moe_bias[l])
        xf = _moe_ln(x_tm, eidx, wts, moe_w_gate[l].astype(BF16), moe_w_up[l].astype(BF16),
                     moe_w_down[l].astype(BF16), sh_w_gate[l].astype(BF16), sh_w_up[l].astype(BF16),
                     sh_w_down[l].astype(BF16), ln_g[l, 1].reshape(1, d), ln_b[l, 1].reshape(1, d), alpha)
    return xf.reshape(b, s, d)
```

```python
import functools

import numpy as np
import jax
import jax.numpy as jnp
from jax import lax
from jax.experimental import pallas as pl
from jax.experimental.pallas import tpu as pltpu

F32 = jnp.float32
BF16 = jnp.bfloat16
I32 = jnp.int32

LANES = 128
NEG = -1e30

HEAD_DIM = 64
MOBA_BLOCK = 256
MOBA_TOPK = 3
MLA_NOPE = 64
MLA_ROPE = 32
MLA_V = 64
SWA_WINDOW = 128
N_EXPERTS = 64
N_GROUPS = 8
TOPK_GROUPS = 4
TOP_K = 8
ROUTED_SCALE = 2.5
ROPE_THETA = 10000.0
QK_SCALE = HEAD_DIM ** -0.5
MLA_SCALE = (MLA_NOPE + MLA_ROPE) ** -0.5

PROJ_ROWS = 512
ATTN_TILE = 1024
SWA_ROWS = 512
ROUTER_ROWS = 512
MOE_CHUNK = 2048
MOE_ROWS = 320
MOE_HEAD = 256
VMEM_LIMIT = 56 * 1024 * 1024

_NT = (((1,), (1,)), ((), ()))


def _cparams(*sem):
    return pltpu.CompilerParams(dimension_semantics=sem, vmem_limit_bytes=VMEM_LIMIT)


def _lane_iota(rows):
    return lax.broadcasted_iota(I32, (rows, LANES), 1)


def _rope(x, c, s, first_half, half):
    rot = jnp.where(first_half, pltpu.roll(x, LANES - half, 1), pltpu.roll(x, half, 1))
    return x * c + rot * s


def _rms(x, g, eps):
    return x * lax.rsqrt(jnp.mean(x * x, -1, keepdims=True) + eps) * g


def _proj_kernel(segs, mla, x_ref, w_ref, c64_ref, s64_ref, *refs):
    xb = x_ref[...].astype(BF16)
    rows = xb.shape[0]
    lane = _lane_iota(rows)
    fh64 = (lane % HEAD_DIM) < HEAD_DIM // 2
    c64 = c64_ref[...]
    s64 = s64_ref[...]

    def seg(lo, hi):
        return jnp.dot(xb, w_ref[:, lo:hi], preferred_element_type=F32)

    if mla:
        cm_ref, sm_ref, gq_ref, wqb_ref, gkv_ref, wkk_ref, wkv_ref = refs[:7]
        refs = refs[7:]
    for (lo, hi, rope, scale), o_ref in zip(segs, refs):
        h = seg(lo, hi)
        if rope:
            for g in range((hi - lo) // LANES):
                sl = slice(g * LANES, (g + 1) * LANES)
                o_ref[:, sl] = (_rope(h[:, sl], c64, s64, fh64, HEAD_DIM // 2) * scale).astype(BF16)
        else:
            o_ref[...] = h.astype(BF16)
    if mla:
        qm_ref, km_ref, vm_ref = refs[len(segs):]
        cm = cm_ref[...]
        sm = sm_ref[...]
        fhm = (lane >= MLA_NOPE) & (lane < MLA_NOPE + MLA_ROPE // 2)
        lo = segs[-1][1]
        qn = _rms(seg(lo, lo + 256), gq_ref[...], 1e-6).astype(BF16)
        qm = jnp.dot(qn, wqb_ref[...], preferred_element_type=F32)
        kvn = _rms(seg(lo + 256, lo + 384), gkv_ref[...], 1e-6).astype(BF16)
        kk = jnp.dot(kvn, wkk_ref[...], preferred_element_type=F32)
        kpe = _rope(seg(lo + 384, lo + 512), cm, sm, fhm, MLA_ROPE // 2)
        for g in range(qm.shape[1] // LANES):
            sl = slice(g * LANES, (g + 1) * LANES)
            qm_ref[:, sl] = (_rope(qm[:, sl], cm, sm, fhm, MLA_ROPE // 2) * MLA_SCALE).astype(BF16)
            km_ref[:, sl] = (kk[:, sl] + kpe).astype(BF16)
        vm_ref[...] = jnp.dot(kvn, wkv_ref[...], preferred_element_type=F32).astype(BF16)


def _projection(xf, w, segs, tabs, mla_args=None, mla_out_widths=()):
    n, d = xf.shape
    tm = min(PROJ_ROWS, n)
    row = lambda width: pl.BlockSpec((tm, width), lambda i: (i, 0))
    full = lambda a: pl.BlockSpec(a.shape, lambda i: (0,) * a.ndim)
    ins = [xf, w, tabs[0], tabs[1]]
    in_specs = [row(d), full(w), row(LANES), row(LANES)]
    if mla_args is not None:
        cm, sm, gq, wqb, gkv, wkk, wkv = mla_args
        ins += [cm, sm, gq, wqb, gkv, wkk, wkv]
        in_specs += [row(LANES), row(LANES), full(gq), full(wqb), full(gkv), full(wkk), full(wkv)]
    widths = [hi - lo for lo, hi, _, _ in segs] + list(mla_out_widths)
    return pl.pallas_call(
        functools.partial(_proj_kernel, tuple(segs), mla_args is not None),
        grid=(n // tm,),
        in_specs=in_specs,
        out_specs=[row(wd) for wd in widths],
        out_shape=[jax.ShapeDtypeStruct((n, wd), BF16) for wd in widths],
        compiler_params=_cparams("parallel"),
        name="projection",
    )(*ins)


def _flash_update(c, q, k, v, bias, state, first):
    m_ref, l_ref, acc_ref = state
    s = lax.dot_general(q, k, _NT, preferred_element_type=F32)
    if bias is not None:
        s = s + bias
    t, tk = s.shape
    mx = jnp.max(s, -1, keepdims=True)
    if first:
        m_new = jnp.broadcast_to(mx, (t, LANES))
    else:
        m_prev = m_ref[c]
        m_new = jnp.maximum(m_prev, mx)
        alpha = jnp.exp(m_prev - m_new)
    ps, psum = [], None
    for j in range(tk // LANES):
        pj = jnp.exp(s[:, j * LANES:(j + 1) * LANES] - m_new)
        ps.append(pj.astype(BF16))
        psum = pj if psum is None else psum + pj
    pv = jnp.dot(jnp.concatenate(ps, axis=1), v, preferred_element_type=F32)
    if first:
        l_ref[c] = psum
        acc_ref[c] = pv
    else:
        l_ref[c] = alpha * l_ref[c] + psum
        acc_ref[c] = alpha * acc_ref[c] + pv
    m_ref[c] = m_new


def _flash_out(c, state):
    _, l_ref, acc_ref = state
    return acc_ref[c] / jnp.sum(l_ref[c], -1, keepdims=True)


def _flash_state(chains, t):
    return [pltpu.VMEM((chains, t, LANES), F32) for _ in range(3)]


def _causal_sweep(qi, step):
    step(qi, True)

    def body(j, carry):
        step(j, False)
        return carry

    lax.fori_loop(0, qi, body, 0)


def _causal_bias(t, block=None):
    r = lax.broadcasted_iota(I32, (t, t), 0)
    c = lax.broadcasted_iota(I32, (t, t), 1)
    hide = c > r
    if block is not None:
        hide = hide & ((r // block) == (c // block))
    return jnp.where(hide, NEG, 0.0).astype(F32)


def _mla_kernel(q_ref, k_ref, v_ref, o_ref, m_ref, l_ref, acc_ref, *, t):
    qi = pl.program_id(2)
    dbias = _causal_bias(t)
    state = (m_ref, l_ref, acc_ref)

    def step(j, first):
        ks = pl.multiple_of(j * t, t)
        v = v_ref[pl.ds(ks, t), :]
        for h in range(2):
            hs = slice(h * LANES, (h + 1) * LANES)
            _flash_update(h, q_ref[:, hs], k_ref[pl.ds(ks, t), hs], v, dbias if first else None, state, first)

    _causal_sweep(qi, step)
    lane = _lane_iota(t)
    o_ref[...] = jnp.where(lane < MLA_V, _flash_out(0, state), _flash_out(1, state)).astype(BF16)


def _mla_attention(qm, km, vm, b, s):
    t = min(ATTN_TILE, s)
    nq = s // t
    hp = qm.shape[1] // (2 * LANES)
    return pl.pallas_call(
        functools.partial(_mla_kernel, t=t),
        grid=(b, hp, nq),
        in_specs=[pl.BlockSpec((t, 2 * LANES), lambda bi, h, qi: (bi * nq + qi, h)),
                  pl.BlockSpec((s, 2 * LANES), lambda bi, h, qi: (bi, h)),
                  pl.BlockSpec((s, LANES), lambda bi, h, qi: (bi, h))],
        out_specs=pl.BlockSpec((t, LANES), lambda bi, h, qi: (bi * nq + qi, h)),
        out_shape=jax.ShapeDtypeStruct((b * s, hp * LANES), BF16),
        scratch_shapes=_flash_state(2, t),
        compiler_params=_cparams("parallel", "parallel", "arbitrary"),
        name="mla_attention",
    )(qm, km, vm)


def _moba_kernel(q_ref, k_ref, v_ref, o_ref, kmean_ref, kaug_ref, qaug_ref, m_ref, l_ref, acc_ref, *, t, nb):
    qi = pl.program_id(2)
    bs = MOBA_BLOCK
    s_len = k_ref.shape[0]
    half = LANES // 2

    @pl.when(qi == 0)
    def _():
        kmean_ref[...] = jnp.zeros_like(kmean_ref)
        for n in range(nb):
            blk = k_ref[n * bs:(n + 1) * bs, :].astype(F32)
            kmean_ref[n:n + 1, :] = jnp.sum(blk, 0, keepdims=True) * (1.0 / bs)
        lane_s = _lane_iota(s_len)
        row_blk = lax.broadcasted_iota(I32, (s_len, LANES), 0) // bs
        kf = k_ref[...].astype(F32)
        kaug_ref[0] = jnp.where(lane_s < half, kf, (lane_s - half == row_blk).astype(F32)).astype(BF16)
        kaug_ref[1] = jnp.where(lane_s >= half, kf, (lane_s == row_blk).astype(F32)).astype(BF16)

    lane = _lane_iota(t)
    nbp = -(-nb // 8) * 8
    blk = lax.broadcasted_iota(I32, (nbp, t), 0)
    own = qi * (t // bs) + lax.broadcasted_iota(I32, (nbp, t), 1) // bs
    qf = q_ref[...].astype(F32)
    kmean = kmean_ref[0:nbp, :]
    for h in range(2):
        head = (lane >= h * half) & (lane < (h + 1) * half)
        qh = jnp.where(head, qf, 0.0)
        gate = lax.dot_general(kmean, qh, _NT, preferred_element_type=F32, precision=lax.Precision.HIGHEST)
        g = jnp.where(blk < own, gate, -jnp.inf)
        sel = blk == own
        for _ in range(MOBA_TOPK):
            mx = jnp.max(g, 0, keepdims=True)
            idx = jnp.min(jnp.where(g == mx, blk, nbp), 0, keepdims=True)
            hit = blk == idx
            sel = sel | (hit & (blk < own))
            g = jnp.where(hit, -jnp.inf, g)
        selb = jnp.where(sel | (blk >= nb), 0.0, NEG).astype(F32)
        selb = jnp.concatenate([selb, jnp.zeros((LANES - nbp, t), F32)], 0).T
        if h == 0:
            selb = pltpu.roll(selb, half, 1)
        qaug_ref[h] = jnp.where(head, qf, selb).astype(BF16)

    dbias = _causal_bias(t, bs)
    state = (m_ref, l_ref, acc_ref)

    def step(j, first):
        ks = pl.multiple_of(j * t, t)
        v = v_ref[pl.ds(ks, t), :]
        for h in range(2):
            _flash_update(h, qaug_ref[h], kaug_ref[h, pl.ds(ks, t), :], v, dbias if first else None, state, first)

    _causal_sweep(qi, step)
    o_ref[...] = jnp.where(lane < half, _flash_out(0, state), _flash_out(1, state)).astype(BF16)


def _moba_attention(q, k, v, b, s):
    bs = MOBA_BLOCK
    nb = s // bs
    assert nb <= LANES // 2, "block gate must fit the spare half of the lanes"
    t = min(ATTN_TILE, s)
    nq = s // t
    hp = q.shape[1] // LANES
    return pl.pallas_call(
        functools.partial(_moba_kernel, t=t, nb=nb),
        grid=(b, hp, nq),
        in_specs=[pl.BlockSpec((t, LANES), lambda bi, h, qi: (bi * nq + qi, h)),
                  pl.BlockSpec((s, LANES), lambda bi, h, qi: (bi, h)),
                  pl.BlockSpec((s, LANES), lambda bi, h, qi: (bi, h))],
        out_specs=pl.BlockSpec((t, LANES), lambda bi, h, qi: (bi * nq + qi, h)),
        out_shape=jax.ShapeDtypeStruct((b * s, hp * LANES), BF16),
        scratch_shapes=[pltpu.VMEM((LANES, LANES), F32), pltpu.VMEM((2, s, LANES), BF16),
                        pltpu.VMEM((2, t, LANES), BF16)] + _flash_state(2, t),
        compiler_params=_cparams("parallel", "parallel", "arbitrary"),
        name="moba_attention",
    )(q, k, v)


def _diff_kernel(q_ref, k_ref, v_ref, lam_ref, g_ref, o_ref, qs_ref, m_ref, l_ref, acc_ref, *, t, lambda_init):
    qi = pl.program_id(2)
    lane = _lane_iota(t)
    qf = q_ref[...].astype(F32)
    qs_ref[0] = jnp.where(lane < HEAD_DIM, qf, 0.0).astype(BF16)
    qs_ref[1] = jnp.where(lane >= HEAD_DIM, qf, 0.0).astype(BF16)
    dbias = _causal_bias(t)
    state = (m_ref, l_ref, acc_ref)

    def step(j, first):
        ks = pl.multiple_of(j * t, t)
        k = k_ref[pl.ds(ks, t), :]
        v = v_ref[pl.ds(ks, t), :]
        for c in range(2):
            _flash_update(c, qs_ref[c], k, v, dbias if first else None, state, first)

    _causal_sweep(qi, step)
    lp = lam_ref[...]
    lam = (jnp.exp(jnp.sum(lp[0:1] * lp[1:2], -1, keepdims=True))
           - jnp.exp(jnp.sum(lp[2:3] * lp[3:4], -1, keepdims=True)) + lambda_init)
    o = _flash_out(0, state) - lam * _flash_out(1, state)
    o_ref[...] = (_rms(o, g_ref[...], 1e-5) * (1.0 - lambda_init)).astype(BF16)


def _diff_attention(q, k, v, lam_p, g_subln, lambda_init, b, s):
    t = min(ATTN_TILE, s)
    nq = s // t
    nh = q.shape[1] // LANES
    return pl.pallas_call(
        functools.partial(_diff_kernel, t=t, lambda_init=lambda_init),
        grid=(b, nh, nq),
        in_specs=[pl.BlockSpec((t, LANES), lambda bi, h, qi: (bi * nq + qi, h)),
                  pl.BlockSpec((s, LANES), lambda bi, h, qi: (bi, h)),
                  pl.BlockSpec((s, LANES), lambda bi, h, qi: (bi, h)),
                  pl.BlockSpec(lam_p.shape, lambda bi, h, qi: (0, 0)),
                  pl.BlockSpec(g_subln.shape, lambda bi, h, qi: (0, 0))],
        out_specs=pl.BlockSpec((t, LANES), lambda bi, h, qi: (bi * nq + qi, h)),
        out_shape=jax.ShapeDtypeStruct((b * s, nh * LANES), BF16),
        scratch_shapes=[pltpu.VMEM((2, t, LANES), BF16)] + _flash_state(2, t),
        compiler_params=_cparams("parallel", "parallel", "arbitrary"),
        name="diff_attention",
    )(q, k, v, lam_p, g_subln)


def _swa_kernel(sink_ref, q_ref, k_ref, v_ref, o_ref, *, rows):
    ti = pl.program_id(1)
    w = SWA_WINDOW
    nsub = rows // w
    n_q = q_ref.shape[1] // HEAD_DIM
    n_kv = k_ref.shape[1] // HEAD_DIM
    grp = n_q // n_kv
    lane_k = lax.broadcasted_iota(I32, (2 * w, LANES), 1)
    lane_o = _lane_iota(w)
    qrow = lax.broadcasted_iota(I32, (grp * w, 2 * w), 0) % w
    kcol = lax.broadcasted_iota(I32, (grp * w, 2 * w), 1)
    for i in range(nsub):
        sb = ti * nsub + i
        start = pl.multiple_of(jnp.maximum(sb - 1, 0) * w, w)
        kb = k_ref[pl.ds(start, 2 * w), :]
        vb = v_ref[pl.ds(start, 2 * w), :]
        rel = qrow + (sb * w - start) - kcol
        bias = jnp.where((rel >= 0) & (rel < w), 0.0, NEG).astype(F32)
        qf = q_ref[i * w:(i + 1) * w, :].astype(F32)
        v_ones = jnp.concatenate([vb, jnp.ones((2 * w, LANES), BF16)], axis=1)
        head_out = [None] * n_q
        for kvh in range(n_kv):
            km = jnp.where(lane_k // HEAD_DIM == kvh, kb.astype(F32), 0.0).astype(BF16)
            qs, sinks = [], []
            for a in range(grp):
                h = kvh * grp + a
                qp = qf[:, (h // 2) * LANES:(h // 2 + 1) * LANES]
                if h % 2 != kvh:
                    qp = pltpu.roll(qp, HEAD_DIM, 1)
                qs.append(qp.astype(BF16))
                sinks.append(jnp.full((w, LANES), sink_ref[h], F32))
            sink = jnp.concatenate(sinks, 0)
            s = lax.dot_general(jnp.concatenate(qs, 0), km, _NT, preferred_element_type=F32) + bias
            m = jnp.maximum(jnp.max(s, -1, keepdims=True), sink)
            p = jnp.concatenate([jnp.exp(s[:, j * LANES:(j + 1) * LANES] - m) for j in range(2 * w // LANES)], 1)
            pv = jnp.dot(p.astype(BF16), v_ones, preferred_element_type=F32)
            o = pv[:, :LANES] / (pv[:, LANES:] + jnp.exp(sink - m))
            for a in range(grp):
                h = kvh * grp + a
                oh = o[a * w:(a + 1) * w, :]
                if h % 2 != kvh:
                    oh = pltpu.roll(oh, HEAD_DIM, 1)
                head_out[h] = oh
        for hp in range(n_q // 2):
            pair = jnp.where(lane_o < HEAD_DIM, head_out[2 * hp], head_out[2 * hp + 1])
            o_ref[i * w:(i + 1) * w, hp * LANES:(hp + 1) * LANES] = pair.astype(BF16)


def _swa_attention(q, k, v, sinks, b, s):
    rows = min(SWA_ROWS, s)
    nt = s // rows
    return pl.pallas_call(
        functools.partial(_swa_kernel, rows=rows),
        grid=(b, nt),
        in_specs=[pl.BlockSpec(memory_space=pltpu.SMEM),
                  pl.BlockSpec((rows, q.shape[1]), lambda bi, ti: (bi * nt + ti, 0)),
                  pl.BlockSpec((s, k.shape[1]), lambda bi, ti: (bi, 0)),
                  pl.BlockSpec((s, v.shape[1]), lambda bi, ti: (bi, 0))],
        out_specs=pl.BlockSpec((rows, q.shape[1]), lambda bi, ti: (bi * nt + ti, 0)),
        out_shape=jax.ShapeDtypeStruct(q.shape, BF16),
        compiler_params=_cparams("parallel", "arbitrary"),
        name="swa_attention",
    )(sinks, q, k, v)


def _layer_norm(y, g, b):
    mu = jnp.mean(y, -1, keepdims=True)
    yc = y - mu
    var = jnp.mean(yc * yc, -1, keepdims=True)
    return yc * lax.rsqrt(var + 1e-5) * g + b


PIECES = 8


def _tm_load(ref, r0, nrows):
    return jnp.concatenate([ref[pl.ds(r0 * PIECES + p, nrows, stride=PIECES), :] for p in range(PIECES)], axis=1)


def _tm_store(ref, r0, val):
    for p in range(PIECES):
        ref[pl.ds(r0 * PIECES + p, val.shape[0], stride=PIECES), :] = val[:, p * LANES:(p + 1) * LANES]


def _out_ln_kernel(x_ref, a_ref, b_ref, w_ref, g_ref, beta_ref, otm_ref, *, alpha):
    ka = a_ref.shape[1]
    mix = (jnp.dot(a_ref[...], w_ref[:ka, :], preferred_element_type=F32)
           + jnp.dot(b_ref[...], w_ref[ka:, :], preferred_element_type=F32))
    _tm_store(otm_ref, 0, _layer_norm(alpha * x_ref[...] + mix, g_ref[...], beta_ref[...]))


def _out_ln(xf, a, b, w, g, beta, alpha):
    n, d = xf.shape
    assert d == PIECES * LANES
    tm = min(PROJ_ROWS, n)
    row = lambda width: pl.BlockSpec((tm, width), lambda i: (i, 0))
    full = lambda arr: pl.BlockSpec(arr.shape, lambda i: (0, 0))
    return pl.pallas_call(
        functools.partial(_out_ln_kernel, alpha=alpha),
        grid=(n // tm,),
        in_specs=[row(d), row(a.shape[1]), row(b.shape[1]), full(w), full(g), full(beta)],
        out_specs=pl.BlockSpec((tm * PIECES, LANES), lambda i: (i, 0)),
        out_shape=jax.ShapeDtypeStruct((n * PIECES, LANES), F32),
        compiler_params=_cparams("parallel"),
        name="out_proj_ln",
    )(xf, a, b, w, g, beta)


def _row_argmax(v, row, nrow):
    mx = jnp.max(v, 0, keepdims=True)
    idx = jnp.min(jnp.where(v == mx, row, nrow), 0, keepdims=True)
    return mx, idx


def _router_kernel(x_ref, w_ref, bias_ref, eidx_ref, wts_ref):
    tm = x_ref.shape[0] // PIECES
    logits = jnp.dot(_tm_load(x_ref, 0, tm), w_ref[...], preferred_element_type=F32,
                     precision=lax.Precision.HIGHEST)
    scores = jax.nn.sigmoid(logits.T[:N_EXPERTS, :])
    biased = scores + bias_ref[...]
    per = N_EXPERTS // N_GROUPS
    row_g = lax.broadcasted_iota(I32, (per, tm), 0)
    gsc = []
    for g in range(N_GROUPS):
        v = biased[g * per:(g + 1) * per, :]
        m1, i1 = _row_argmax(v, row_g, per)
        m2 = jnp.max(jnp.where(row_g == i1, -jnp.inf, v), 0, keepdims=True)
        gsc.append(m1 + m2)
    gscore = jnp.concatenate(gsc, 0)
    row_n = lax.broadcasted_iota(I32, (N_GROUPS, tm), 0)
    gok = jnp.zeros((N_GROUPS, tm), F32)
    for _ in range(TOPK_GROUPS):
        _, ig = _row_argmax(gscore, row_n, N_GROUPS)
        hit = row_n == ig
        gok = jnp.where(hit, 1.0, gok)
        gscore = jnp.where(hit, -jnp.inf, gscore)
    cand = jnp.concatenate([jnp.where(gok[g:g + 1, :] > 0.0, biased[g * per:(g + 1) * per, :], -jnp.inf)
                            for g in range(N_GROUPS)], 0)
    row_e = lax.broadcasted_iota(I32, (N_EXPERTS, tm), 0)
    ids, ws = [], []
    for _ in range(TOP_K):
        _, ie = _row_argmax(cand, row_e, N_EXPERTS)
        hit = row_e == ie
        ids.append(ie)
        ws.append(jnp.sum(jnp.where(hit, scores, 0.0), 0, keepdims=True))
        cand = jnp.where(hit, -jnp.inf, cand)
    w = jnp.concatenate(ws, 0)
    eidx_ref[...] = jnp.concatenate(ids, 0)
    wts_ref[...] = w / jnp.sum(w, 0, keepdims=True) * ROUTED_SCALE


def _router(x_tm, w_router, e_bias):
    n = x_tm.shape[0] // PIECES
    tm = min(ROUTER_ROWS, n)
    wpad = jnp.pad(w_router, ((0, 0), (0, LANES - N_EXPERTS)))
    bcol = e_bias.astype(F32).reshape(N_EXPERTS, 1)
    out = pl.BlockSpec((TOP_K, tm), lambda i: (0, i))
    return pl.pallas_call(
        _router_kernel,
        grid=(n // tm,),
        in_specs=[pl.BlockSpec((tm * PIECES, LANES), lambda i: (i, 0)), pl.BlockSpec(wpad.shape, lambda i: (0, 0)),
                  pl.BlockSpec(bcol.shape, lambda i: (0, 0))],
        out_specs=[out, out],
        out_shape=[jax.ShapeDtypeStruct((TOP_K, n), I32), jax.ShapeDtypeStruct((TOP_K, n), F32)],
        compiler_params=_cparams("parallel"),
        name="router",
    )(x_tm, wpad, bcol)


def _dispatch(eidx_t, wts_t, chunk, tail):
    n = eidx_t.shape[1]
    nc = n // chunk
    a = chunk * TOP_K
    ec = eidx_t.reshape(TOP_K, nc, chunk).transpose(1, 0, 2).reshape(nc, a)
    wc = wts_t.reshape(TOP_K, nc, chunk).transpose(1, 0, 2).reshape(nc, a)
    ex = jnp.arange(N_EXPERTS, dtype=I32)
    counts = jnp.sum((ec[:, :, None] == ex[None, None, :]).astype(I32), axis=1)
    ends = jnp.cumsum(counts, axis=1).astype(I32)
    starts = jnp.concatenate([jnp.zeros((nc, 1), I32), ends], axis=1)
    mult = 1 << int(chunk - 1).bit_length()
    tok_ids = jnp.broadcast_to(jnp.arange(a, dtype=I32)[None, :] % chunk, (nc, a))
    keys, gates = lax.sort((ec * mult + tok_ids, wc), dimension=1, num_keys=1)
    rows = jnp.pad((keys % mult) * PIECES, ((0, 0), (0, tail)))
    gates = jnp.pad(gates, ((0, 0), (0, tail)))
    return starts.reshape(-1), rows.reshape(-1), gates.reshape(-1)


def _swiglu(xb, wg, wu):
    hg = jnp.dot(xb, wg, preferred_element_type=F32)
    hu = jnp.dot(xb, wu, preferred_element_type=F32)
    return hg * jax.nn.sigmoid(hg) * hu


def _moe_kernel(start_ref, tok_ref, gate_ref, x_ref, wg_ref, wu_ref, wd_ref, sg_ref, su_ref, sd_ref,
                g_ref, beta_ref, o_ref, acc_ref, xga_ref, xgb_ref, oga_ref, ogb_ref, xgo_ref, ogo_ref,
                *, chunk, rows, head, alpha):
    c = pl.program_id(0)
    e = pl.program_id(1)
    n_e = pl.num_programs(1)
    group = 8
    sub = min(512, chunk)

    @pl.when((c == 0) & (e == 0))
    def _():
        for ref in (xga_ref, xgb_ref, xgo_ref):
            ref[...] = jnp.zeros_like(ref)

    @pl.when(e == 0)
    def _():
        for s0 in range(0, chunk, sub):
            xb = _tm_load(x_ref, s0, sub).astype(BF16)
            h = _swiglu(xb, sg_ref[...], su_ref[...]).astype(BF16)
            _tm_store(acc_ref, s0, jnp.dot(h, sd_ref[...], preferred_element_type=F32))
        acc_ref[chunk * PIECES:(chunk + 1) * PIECES, :] = jnp.zeros((PIECES, LANES), F32)

    def run(ei):
        return start_ref[c * (n_e + 1) + ei], start_ref[c * (n_e + 1) + ei + 1]

    lo, hi = run(e)
    n_cur = jnp.minimum(hi - lo, rows)
    lo_next, _ = run(jnp.minimum(e + 1, n_e - 1))
    lo_prev, hi_prev = run(jnp.maximum(e - 1, 0))
    n_prev = jnp.where(e > 0, jnp.minimum(hi_prev - lo_prev, rows), 0)

    def at_row(first):
        return pl.ds(pl.multiple_of(first, PIECES), PIECES)

    def gather_group(xg, base, r0, _):
        for i in range(group):
            xg[at_row((r0 + i) * PIECES), :] = x_ref[at_row(tok_ref[base + r0 + i]), :]

    def scatter_group(og, base, nrow, r0, mask):
        firsts, vals = [], []
        for i in range(group):
            idx = base + r0 + i
            first = tok_ref[idx]
            if mask:
                first = jnp.where(r0 + i < nrow, first, chunk * PIECES)
            firsts.append(first)
            vals.append(acc_ref[at_row(first), :] + gate_ref[idx] * og[at_row((r0 + i) * PIECES), :])
        for i in range(group):
            acc_ref[at_row(firsts[i]), :] = vals[i]

    def sweep(fn, r_lo, r_hi):
        n = jnp.maximum(r_hi - r_lo, 0)
        nfull = n // group

        def body(gi, carry):
            fn(pl.multiple_of(r_lo + gi * group, group), False)
            return carry

        lax.fori_loop(0, nfull, body, 0)
        pl.when(n % group != 0)(lambda: fn(pl.multiple_of(r_lo + nfull * group, group), True))

    def expert(xg, og):
        xb = _tm_load(xg, 0, rows).astype(BF16)
        h = _swiglu(xb, wg_ref[0], wu_ref[0]).astype(BF16)
        _tm_store(og, 0, jnp.dot(h, wd_ref[0], preferred_element_type=F32))

    def step(xg_cur, xg_next, og_cur, og_prev):
        sweep(lambda r0, p: gather_group(xg_cur, lo, r0, p), jnp.where(e == 0, 0, head), n_cur)
        for r0 in range(0, head, group):
            gather_group(xg_next, lo_next, r0, True)
        expert(xg_cur, og_cur)
        for r0 in range(0, head, group):
            scatter_group(og_prev, lo_prev, n_prev, r0, True)
        sweep(lambda r0, p: scatter_group(og_cur, lo, n_cur, r0, p), head, n_cur)

        def later_block(bi, carry):
            base = lo + bi * rows
            nrow = jnp.minimum(hi - base, rows)
            sweep(lambda r0, p: gather_group(xgo_ref, base, r0, p), 0, nrow)
            expert(xgo_ref, ogo_ref)
            sweep(lambda r0, p: scatter_group(ogo_ref, base, nrow, r0, p), 0, nrow)
            return carry

        lax.fori_loop(1, (hi - lo + rows - 1) // rows, later_block, 0)
        pl.when(e == n_e - 1)(
            lambda: sweep(lambda r0, p: scatter_group(og_cur, lo, n_cur, r0, p), 0, jnp.minimum(n_cur, head)))

    pl.when(e % 2 == 0)(lambda: step(xga_ref, xgb_ref, oga_ref, ogb_ref))
    pl.when(e % 2 == 1)(lambda: step(xgb_ref, xga_ref, ogb_ref, oga_ref))

    @pl.when(e == n_e - 1)
    def _():
        for s0 in range(0, chunk, sub):
            z = alpha * _tm_load(x_ref, s0, sub) + _tm_load(acc_ref, s0, sub)
            o_ref[s0:s0 + sub, :] = _layer_norm(z, g_ref[...], beta_ref[...])


def _moe_ln(x_tm, eidx, wts, wg, wu, wd, sg, su, sd, g, beta, alpha):
    n = x_tm.shape[0] // PIECES
    d = PIECES * LANES
    chunk = min(MOE_CHUNK, n)
    rows = MOE_ROWS
    tail = -(-rows // 1024) * 1024
    starts, tok, gate = _dispatch(eidx, wts, chunk, tail)
    nc = n // chunk
    n_e, _, de = wg.shape
    full = lambda arr: pl.BlockSpec(arr.shape, lambda c, e, s_r: (0,) * arr.ndim)
    lists = pl.BlockSpec((chunk * TOP_K + tail,), lambda c, e, s_r: (c,), memory_space=pltpu.SMEM)
    grid_spec = pltpu.PrefetchScalarGridSpec(
        num_scalar_prefetch=1,
        grid=(nc, n_e),
        in_specs=[
            lists, lists,
            pl.BlockSpec((chunk * PIECES, LANES), lambda c, e, s_r: (c, 0)),
            pl.BlockSpec((1, d, de), lambda c, e, s_r: (e, 0, 0)),
            pl.BlockSpec((1, d, de), lambda c, e, s_r: (e, 0, 0)),
            pl.BlockSpec((1, de, d), lambda c, e, s_r: (e, 0, 0)),
            full(sg), full(su), full(sd), full(g), full(beta),
        ],
        out_specs=pl.BlockSpec((chunk, d), lambda c, e, s_r: (c, 0)),
        scratch_shapes=[pltpu.VMEM(((chunk + 1) * PIECES, LANES), F32)]
        + [pltpu.VMEM((rows * PIECES, LANES), F32) for _ in range(6)],
    )
    return pl.pallas_call(
        functools.partial(_moe_kernel, chunk=chunk, rows=rows, head=min(MOE_HEAD, rows), alpha=alpha),
        grid_spec=grid_spec,
        out_shape=jax.ShapeDtypeStruct((n, d), F32),
        compiler_params=_cparams("arbitrary", "arbitrary"),
        name="moe_experts",
    )(starts, tok, gate, x_tm, wg, wu, wd, sg, su, sd, g, beta)


def _rope_tables(positions):
    pos = positions.reshape(-1).astype(F32)[:, None]
    n = pos.shape[0]

    def cs(dim):
        inv = ROPE_THETA ** (-jnp.arange(0, dim, 2, dtype=F32) / dim)
        ang = pos * inv
        return jnp.cos(ang), jnp.sin(ang)

    c, s = cs(HEAD_DIM)
    c64 = jnp.concatenate([c, c, c, c], -1)
    s64 = jnp.concatenate([-s, s, -s, s], -1)
    c, s = cs(MLA_ROPE)
    one = jnp.ones((n, MLA_NOPE), F32)
    pad = LANES - MLA_NOPE - MLA_ROPE
    cm = jnp.concatenate([one, c, c, jnp.ones((n, pad), F32)], -1)
    sm = jnp.concatenate([0 * one, -s, s, jnp.zeros((n, pad), F32)], -1)
    return (c64, s64), (cm, sm)


def _even_mixer(xf, b, s, tabs, w_in, g_qa, w_qb, g_kva, w_kvb):
    d = xf.shape[1]
    tab64, tabm = tabs
    na = 3 * 512
    kpe_lo = na + 256 + 128
    wpad = jnp.concatenate([w_in[:, :kpe_lo], jnp.zeros((d, MLA_NOPE), F32), w_in[:, kpe_lo:],
                            jnp.zeros((d, LANES - MLA_NOPE - MLA_ROPE), F32)], 1).astype(BF16)
    nh = w_qb.shape[1] // (MLA_NOPE + MLA_ROPE)
    wqb = jnp.pad(w_qb.reshape(-1, nh, MLA_NOPE + MLA_ROPE), ((0, 0), (0, 0), (0, LANES - MLA_NOPE - MLA_ROPE)))
    wqb = wqb.reshape(-1, nh * LANES).astype(BF16)
    wkv3 = w_kvb.reshape(-1, nh, MLA_NOPE + MLA_V)
    wkk = jnp.pad(wkv3[:, :, :MLA_NOPE], ((0, 0), (0, 0), (0, LANES - MLA_NOPE))).reshape(-1, nh * LANES).astype(BF16)
    wkv = wkv3[:, :, MLA_NOPE:].reshape(-1, nh * MLA_V).astype(BF16)
    segs = [(0, 512, True, QK_SCALE), (512, 1024, True, 1.0), (1024, 1536, False, 1.0)]
    qa, ka, va, qm, km, vm = _projection(
        xf, wpad, segs, tab64,
        mla_args=(tabm[0], tabm[1], g_qa.reshape(1, -1), wqb, g_kva.reshape(1, -1), wkk, wkv),
        mla_out_widths=(nh * LANES, nh * LANES, nh * MLA_V))
    o_a = _moba_attention(qa, ka, va, b, s)
    o_b = _mla_attention(qm, km, vm, b, s)
    return o_a, o_b


def _odd_mixer(xf, b, s, tabs, w_in, sinks, lam_p, g_subln, lambda_init):
    segs = [(0, 512, True, QK_SCALE), (512, 640, True, 1.0), (640, 768, False, 1.0),
            (768, 1280, True, QK_SCALE), (1280, 1792, True, 1.0), (1792, 2304, False, 1.0)]
    qc, kc, vc, qd, kd, vd = _projection(xf, w_in.astype(BF16), segs, tabs[0])
    o_c = _swa_attention(qc, kc, vc, sinks.astype(F32), b, s)
    o_d = _diff_attention(qd, kd, vd, lam_p.astype(F32), g_subln.reshape(1, -1), lambda_init, b, s)
    return o_c, o_d


def kernel(x, positions, ev_w_in, ev_w_out, mla_g_qa, mla_w_qb, mla_g_kva, mla_w_kvb, od_w_in, od_w_out, swa_sinks, diff_lambda, diff_g_subln, ln_g, ln_b, moe_w_router, moe_bias, moe_w_gate, moe_w_up, moe_w_down, sh_w_gate, sh_w_up, sh_w_down):
    b, s, d = x.shape
    depth = ln_g.shape[0]
    alpha = float((2 * depth) ** 0.25)
    xf = x.reshape(b * s, d)
    tabs = _rope_tables(positions)
    for l in range(depth):
        i = l // 2
        if l % 2 == 0:
            o1, o2 = _even_mixer(xf, b, s, tabs, ev_w_in[i], mla_g_qa[i], mla_w_qb[i], mla_g_kva[i], mla_w_kvb[i])
            w_out = ev_w_out[i]
        else:
            lambda_init = 0.8 - 0.6 * float(np.exp(-0.3 * l))
            o1, o2 = _odd_mixer(xf, b, s, tabs, od_w_in[i], swa_sinks[i], diff_lambda[i], diff_g_subln[i], lambda_init)
            w_out = od_w_out[i]
        x_tm = _out_ln(xf, o1, o2, w_out.astype(BF16), ln_g[l, 0].reshape(1, d), ln_b[l, 0].reshape(1, d), alpha)
        eidx, wts = _router(x_tm, moe_w_router[l], moe_bias[l])
        xf = _moe_ln(x_tm, eidx, wts, moe_w_gate[l].astype(BF16), moe_w_up[l].astype(BF16),
                     moe_w_down[l].astype(BF16), sh_w_gate[l].astype(BF16), sh_w_up[l].astype(BF16),
                     sh_w_down[l].astype(BF16), ln_g[l, 1].reshape(1, d), ln_b[l, 1].reshape(1, d), alpha)
    return xf.reshape(b, s, d)
```

```python
import functools

import numpy as np
import jax
import jax.numpy as jnp
from jax import lax
from jax.experimental import pallas as pl
from jax.experimental.pallas import tpu as pltpu

F32 = jnp.float32
BF16 = jnp.bfloat16
I32 = jnp.int32

LANES = 128
NEG = -1e30

HEAD_DIM = 64
MOBA_BLOCK = 256
MOBA_TOPK = 3
MLA_NOPE = 64
MLA_ROPE = 32
MLA_V = 64
SWA_WINDOW = 128
N_EXPERTS = 64
N_GROUPS = 8
TOPK_GROUPS = 4
TOP_K = 8
ROUTED_SCALE = 2.5
ROPE_THETA = 10000.0
QK_SCALE = HEAD_DIM ** -0.5
MLA_SCALE = (MLA_NOPE + MLA_ROPE) ** -0.5

PROJ_ROWS = 512
ATTN_TILE = 1024
SWA_ROWS = 512
ROUTER_ROWS = 512
MOE_CHUNK = 2048
MOE_ROWS = 320
MOE_HEAD = 256
VMEM_LIMIT = 56 * 1024 * 1024

_NT = (((1,), (1,)), ((), ()))


def _cparams(*sem):
    return pltpu.CompilerParams(dimension_semantics=sem, vmem_limit_bytes=VMEM_LIMIT)


def _lane_iota(rows):
    return lax.broadcasted_iota(I32, (rows, LANES), 1)


def _rope(x, c, s, first_half, half):
    rot = jnp.where(first_half, pltpu.roll(x, LANES - half, 1), pltpu.roll(x, half, 1))
    return x * c + rot * s


def _rms(x, g, eps):
    return x * lax.rsqrt(jnp.mean(x * x, -1, keepdims=True) + eps) * g


def _proj_kernel(segs, mla, x_ref, w_ref, c64_ref, s64_ref, *refs):
    xb = x_ref[...].astype(BF16)
    rows = xb.shape[0]
    lane = _lane_iota(rows)
    fh64 = (lane % HEAD_DIM) < HEAD_DIM // 2
    c64 = c64_ref[...]
    s64 = s64_ref[...]

    def seg(lo, hi):
        return jnp.dot(xb, w_ref[:, lo:hi], preferred_element_type=F32)

    if mla:
        cm_ref, sm_ref, gq_ref, wqb_ref, gkv_ref, wkk_ref, wkv_ref = refs[:7]
        refs = refs[7:]
    for (lo, hi, rope, scale), o_ref in zip(segs, refs):
        h = seg(lo, hi)
        if rope:
            for g in range((hi - lo) // LANES):
                sl = slice(g * LANES, (g + 1) * LANES)
                o_ref[:, sl] = (_rope(h[:, sl], c64, s64, fh64, HEAD_DIM // 2) * scale).astype(BF16)
        else:
            o_ref[...] = h.astype(BF16)
    if mla:
        qm_ref, km_ref, vm_ref = refs[len(segs):]
        cm = cm_ref[...]
        sm = sm_ref[...]
        fhm = (lane >= MLA_NOPE) & (lane < MLA_NOPE + MLA_ROPE // 2)
        lo = segs[-1][1]
        qn = _rms(seg(lo, lo + 256), gq_ref[...], 1e-6).astype(BF16)
        qm = jnp.dot(qn, wqb_ref[...], preferred_element_type=F32)
        kvn = _rms(seg(lo + 256, lo + 384), gkv_ref[...], 1e-6).astype(BF16)
        kk = jnp.dot(kvn, wkk_ref[...], preferred_element_type=F32)
        kpe = _rope(seg(lo + 384, lo + 512), cm, sm, fhm, MLA_ROPE // 2)
        for g in range(qm.shape[1] // LANES):
            sl = slice(g * LANES, (g + 1) * LANES)
            qm_ref[:, sl] = (_rope(qm[:, sl], cm, sm, fhm, MLA_ROPE // 2) * MLA_SCALE).astype(BF16)
            km_ref[:, sl] = (kk[:, sl] + kpe).astype(BF16)
        vm_ref[...] = jnp.dot(kvn, wkv_ref[...], preferred_element_type=F32).astype(BF16)


def _projection(xf, w, segs, tabs, mla_args=None, mla_out_widths=()):
    n, d = xf.shape
    tm = min(PROJ_ROWS, n)
    row = lambda width: pl.BlockSpec((tm, width), lambda i: (i, 0))
    full = lambda a: pl.BlockSpec(a.shape, lambda i: (0,) * a.ndim)
    ins = [xf, w, tabs[0], tabs[1]]
    in_specs = [row(d), full(w), row(LANES), row(LANES)]
    if mla_args is not None:
        cm, sm, gq, wqb, gkv, wkk, wkv = mla_args
        ins += [cm, sm, gq, wqb, gkv, wkk, wkv]
        in_specs += [row(LANES), row(LANES), full(gq), full(wqb), full(gkv), full(wkk), full(wkv)]
    widths = [hi - lo for lo, hi, _, _ in segs] + list(mla_out_widths)
    return pl.pallas_call(
        functools.partial(_proj_kernel, tuple(segs), mla_args is not None),
        grid=(n // tm,),
        in_specs=in_specs,
        out_specs=[row(wd) for wd in widths],
        out_shape=[jax.ShapeDtypeStruct((n, wd), BF16) for wd in widths],
        compiler_params=_cparams("parallel"),
        name="projection",
    )(*ins)


def _flash_update(c, q, k, v, bias, state, first):
    m_ref, l_ref, acc_ref = state
    s = lax.dot_general(q, k, _NT, preferred_element_type=F32)
    if bias is not None:
        s = s + bias
    t, tk = s.shape
    mx = jnp.max(s, -1, keepdims=True)
    if first:
        m_new = jnp.broadcast_to(mx, (t, LANES))
    else:
        m_prev = m_ref[c]
        m_new = jnp.maximum(m_prev, mx)
        alpha = jnp.exp(m_prev - m_new)
    ps, psum = [], None
    for j in range(tk // LANES):
        pj = jnp.exp(s[:, j * LANES:(j + 1) * LANES] - m_new)
        ps.append(pj.astype(BF16))
        psum = pj if psum is None else psum + pj
    pv = jnp.dot(jnp.concatenate(ps, axis=1), v, preferred_element_type=F32)
    if first:
        l_ref[c] = psum
        acc_ref[c] = pv
    else:
        l_ref[c] = alpha * l_ref[c] + psum
        acc_ref[c] = alpha * acc_ref[c] + pv
    m_ref[c] = m_new


def _flash_out(c, state):
    _, l_ref, acc_ref = state
    return acc_ref[c] / jnp.sum(l_ref[c], -1, keepdims=True)


def _flash_state(chains, t):
    return [pltpu.VMEM((chains, t, LANES), F32) for _ in range(3)]


def _causal_sweep(qi, step):
    step(qi, True)

    def body(j, carry):
        step(j, False)
        return carry

    lax.fori_loop(0, qi, body, 0)


def _causal_bias(t, block=None):
    r = lax.broadcasted_iota(I32, (t, t), 0)
    c = lax.broadcasted_iota(I32, (t, t), 1)
    hide = c > r
    if block is not None:
        hide = hide & ((r // block) == (c // block))
    return jnp.where(hide, NEG, 0.0).astype(F32)


def _mla_kernel(q_ref, k_ref, v_ref, o_ref, m_ref, l_ref, acc_ref, *, t):
    qi = pl.program_id(2)
    dbias = _causal_bias(t)
    state = (m_ref, l_ref, acc_ref)

    def step(j, first):
        ks = pl.multiple_of(j * t, t)
        v = v_ref[pl.ds(ks, t), :]
        for h in range(2):
            hs = slice(h * LANES, (h + 1) * LANES)
            _flash_update(h, q_ref[:, hs], k_ref[pl.ds(ks, t), hs], v, dbias if first else None, state, first)

    _causal_sweep(qi, step)
    lane = _lane_iota(t)
    o_ref[...] = jnp.where(lane < MLA_V, _flash_out(0, state), _flash_out(1, state)).astype(BF16)


def _mla_attention(qm, km, vm, b, s):
    t = min(ATTN_TILE, s)
    nq = s // t
    hp = qm.shape[1] // (2 * LANES)
    return pl.pallas_call(
        functools.partial(_mla_kernel, t=t),
        grid=(b, hp, nq),
        in_specs=[pl.BlockSpec((t, 2 * LANES), lambda bi, h, qi: (bi * nq + qi, h)),
                  pl.BlockSpec((s, 2 * LANES), lambda bi, h, qi: (bi, h)),
                  pl.BlockSpec((s, LANES), lambda bi, h, qi: (bi, h))],
        out_specs=pl.BlockSpec((t, LANES), lambda bi, h, qi: (bi * nq + qi, h)),
        out_shape=jax.ShapeDtypeStruct((b * s, hp * LANES), BF16),
        scratch_shapes=_flash_state(2, t),
        compiler_params=_cparams("parallel", "parallel", "arbitrary"),
        name="mla_attention",
    )(qm, km, vm)


def _moba_kernel(q_ref, k_ref, v_ref, o_ref, kmean_ref, kaug_ref, qaug_ref, m_ref, l_ref, acc_ref, *, t, nb):
    qi = pl.program_id(2)
    bs = MOBA_BLOCK
    s_len = k_ref.shape[0]
    half = LANES // 2

    @pl.when(qi == 0)
    def _():
        kmean_ref[...] = jnp.zeros_like(kmean_ref)
        for n in range(nb):
            blk = k_ref[n * bs:(n + 1) * bs, :].astype(F32)
            kmean_ref[n:n + 1, :] = jnp.sum(blk, 0, keepdims=True) * (1.0 / bs)
        lane_s = _lane_iota(s_len)
        row_blk = lax.broadcasted_iota(I32, (s_len, LANES), 0) // bs
        kf = k_ref[...].astype(F32)
        kaug_ref[0] = jnp.where(lane_s < half, kf, (lane_s - half == row_blk).astype(F32)).astype(BF16)
        kaug_ref[1] = jnp.where(lane_s >= half, kf, (lane_s == row_blk).astype(F32)).astype(BF16)

    lane = _lane_iota(t)
    nbp = -(-nb // 8) * 8
    blk = lax.broadcasted_iota(I32, (nbp, t), 0)
    own = qi * (t // bs) + lax.broadcasted_iota(I32, (nbp, t), 1) // bs
    qf = q_ref[...].astype(F32)
    kmean = kmean_ref[0:nbp, :]
    for h in range(2):
        head = (lane >= h * half) & (lane < (h + 1) * half)
        qh = jnp.where(head, qf, 0.0)
        gate = lax.dot_general(kmean, qh, _NT, preferred_element_type=F32, precision=lax.Precision.HIGHEST)
        g = jnp.where(blk < own, gate, -jnp.inf)
        sel = blk == own
        for _ in range(MOBA_TOPK):
            mx = jnp.max(g, 0, keepdims=True)
            idx = jnp.min(jnp.where(g == mx, blk, nbp), 0, keepdims=True)
            hit = blk == idx
            sel = sel | (hit & (blk < own))
            g = jnp.where(hit, -jnp.inf, g)
        selb = jnp.where(sel | (blk >= nb), 0.0, NEG).astype(F32)
        selb = jnp.concatenate([selb, jnp.zeros((LANES - nbp, t), F32)], 0).T
        if h == 0:
            selb = pltpu.roll(selb, half, 1)
        qaug_ref[h] = jnp.where(head, qf, selb).astype(BF16)

    dbias = _causal_bias(t, bs)
    state = (m_ref, l_ref, acc_ref)

    def step(j, first):
        ks = pl.multiple_of(j * t, t)
        v = v_ref[pl.ds(ks, t), :]
        for h in range(2):
            _flash_update(h, qaug_ref[h], kaug_ref[h, pl.ds(ks, t), :], v, dbias if first else None, state, first)

    _causal_sweep(qi, step)
    o_ref[...] = jnp.where(lane < half, _flash_out(0, state), _flash_out(1, state)).astype(BF16)


def _moba_attention(q, k, v, b, s):
    bs = MOBA_BLOCK
    nb = s // bs
    assert nb <= LANES // 2, "block gate must fit the spare half of the lanes"
    t = min(ATTN_TILE, s)
    nq = s // t
    hp = q.shape[1] // LANES
    return pl.pallas_call(
        functools.partial(_moba_kernel, t=t, nb=nb),
        grid=(b, hp, nq),
        in_specs=[pl.BlockSpec((t, LANES), lambda bi, h, qi: (bi * nq + qi, h)),
                  pl.BlockSpec((s, LANES), lambda bi, h, qi: (bi, h)),
                  pl.BlockSpec((s, LANES), lambda bi, h, qi: (bi, h))],
        out_specs=pl.BlockSpec((t, LANES), lambda bi, h, qi: (bi * nq + qi, h)),
        out_shape=jax.ShapeDtypeStruct((b * s, hp * LANES), BF16),
        scratch_shapes=[pltpu.VMEM((LANES, LANES), F32), pltpu.VMEM((2, s, LANES), BF16),
                        pltpu.VMEM((2, t, LANES), BF16)] + _flash_state(2, t),
        compiler_params=_cparams("parallel", "parallel", "arbitrary"),
        name="moba_attention",
    )(q, k, v)


def _diff_kernel(q_ref, k_ref, v_ref, lam_ref, g_ref, o_ref, qs_ref, m_ref, l_ref, acc_ref, *, t, lambda_init):
    qi = pl.program_id(2)
    lane = _lane_iota(t)
    qf = q_ref[...].astype(F32)
    qs_ref[0] = jnp.where(lane < HEAD_DIM, qf, 0.0).astype(BF16)
    qs_ref[1] = jnp.where(lane >= HEAD_DIM, qf, 0.0).astype(BF16)
    dbias = _causal_bias(t)
    state = (m_ref, l_ref, acc_ref)

    def step(j, first):
        ks = pl.multiple_of(j * t, t)
        k = k_ref[pl.ds(ks, t), :]
        v = v_ref[pl.ds(ks, t), :]
        for c in range(2):
            _flash_update(c, qs_ref[c], k, v, dbias if first else None, state, first)

    _causal_sweep(qi, step)
    lp = lam_ref[...]
    lam = (jnp.exp(jnp.sum(lp[0:1] * lp[1:2], -1, keepdims=True))
           - jnp.exp(jnp.sum(lp[2:3] * lp[3:4], -1, keepdims=True)) + lambda_init)
    o = _flash_out(0, state) - lam * _flash_out(1, state)
    o_ref[...] = (_rms(o, g_ref[...], 1e-5) * (1.0 - lambda_init)).astype(BF16)


def _diff_attention(q, k, v, lam_p, g_subln, lambda_init, b, s):
    t = min(ATTN_TILE, s)
    nq = s // t
    nh = q.shape[1] // LANES
    return pl.pallas_call(
        functools.partial(_diff_kernel, t=t, lambda_init=lambda_init),
        grid=(b, nh, nq),
        in_specs=[pl.BlockSpec((t, LANES), lambda bi, h, qi: (bi * nq + qi, h)),
                  pl.BlockSpec((s, LANES), lambda bi, h, qi: (bi, h)),
                  pl.BlockSpec((s, LANES), lambda bi, h, qi: (bi, h)),
                  pl.BlockSpec(lam_p.shape, lambda bi, h, qi: (0, 0)),
                  pl.BlockSpec(g_subln.shape, lambda bi, h, qi: (0, 0))],
        out_specs=pl.BlockSpec((t, LANES), lambda bi, h, qi: (bi * nq + qi, h)),
        out_shape=jax.ShapeDtypeStruct((b * s, nh * LANES), BF16),
        scratch_shapes=[pltpu.VMEM((2, t, LANES), BF16)] + _flash_state(2, t),
        compiler_params=_cparams("parallel", "parallel", "arbitrary"),
        name="diff_attention",
    )(q, k, v, lam_p, g_subln)


def _swa_kernel(sink_ref, q_ref, k_ref, v_ref, o_ref, *, rows):
    ti = pl.program_id(1)
    w = SWA_WINDOW
    nsub = rows // w
    n_q = q_ref.shape[1] // HEAD_DIM
    n_kv = k_ref.shape[1] // HEAD_DIM
    grp = n_q // n_kv
    lane_k = lax.broadcasted_iota(I32, (2 * w, LANES), 1)
    lane_o = _lane_iota(w)
    qrow = lax.broadcasted_iota(I32, (grp * w, 2 * w), 0) % w
    kcol = lax.broadcasted_iota(I32, (grp * w, 2 * w), 1)
    for i in range(nsub):
        sb = ti * nsub + i
        start = pl.multiple_of(jnp.maximum(sb - 1, 0) * w, w)
        kb = k_ref[pl.ds(start, 2 * w), :]
        vb = v_ref[pl.ds(start, 2 * w), :]
        rel = qrow + (sb * w - start) - kcol
        bias = jnp.where((rel >= 0) & (rel < w), 0.0, NEG).astype(F32)
        qf = q_ref[i * w:(i + 1) * w, :].astype(F32)
        v_ones = jnp.concatenate([vb, jnp.ones((2 * w, LANES), BF16)], axis=1)
        head_out = [None] * n_q
        for kvh in range(n_kv):
            km = jnp.where(lane_k // HEAD_DIM == kvh, kb.astype(F32), 0.0).astype(BF16)
            qs, sinks = [], []
            for a in range(grp):
                h = kvh * grp + a
                qp = qf[:, (h // 2) * LANES:(h // 2 + 1) * LANES]
                if h % 2 != kvh:
                    qp = pltpu.roll(qp, HEAD_DIM, 1)
                qs.append(qp.astype(BF16))
                sinks.append(jnp.full((w, LANES), sink_ref[h], F32))
            sink = jnp.concatenate(sinks, 0)
            s = lax.dot_general(jnp.concatenate(qs, 0), km, _NT, preferred_element_type=F32) + bias
            m = jnp.maximum(jnp.max(s, -1, keepdims=True), sink)
            p = jnp.concatenate([jnp.exp(s[:, j * LANES:(j + 1) * LANES] - m) for j in range(2 * w // LANES)], 1)
            pv = jnp.dot(p.astype(BF16), v_ones, preferred_element_type=F32)
            o = pv[:, :LANES] / (pv[:, LANES:] + jnp.exp(sink - m))
            for a in range(grp):
                h = kvh * grp + a
                oh = o[a * w:(a + 1) * w, :]
                if h % 2 != kvh:
                    oh = pltpu.roll(oh, HEAD_DIM, 1)
                head_out[h] = oh
        for hp in range(n_q // 2):
            pair = jnp.where(lane_o < HEAD_DIM, head_out[2 * hp], head_out[2 * hp + 1])
            o_ref[i * w:(i + 1) * w, hp * LANES:(hp + 1) * LANES] = pair.astype(BF16)


def _swa_attention(q, k, v, sinks, b, s):
    rows = min(SWA_ROWS, s)
    nt = s // rows
    return pl.pallas_call(
        functools.partial(_swa_kernel, rows=rows),
        grid=(b, nt),
        in_specs=[pl.BlockSpec(memory_space=pltpu.SMEM),
                  pl.BlockSpec((rows, q.shape[1]), lambda bi, ti: (bi * nt + ti, 0)),
                  pl.BlockSpec((s, k.shape[1]), lambda bi, ti: (bi, 0)),
                  pl.BlockSpec((s, v.shape[1]), lambda bi, ti: (bi, 0))],
        out_specs=pl.BlockSpec((rows, q.shape[1]), lambda bi, ti: (bi * nt + ti, 0)),
        out_shape=jax.ShapeDtypeStruct(q.shape, BF16),
        compiler_params=_cparams("parallel", "arbitrary"),
        name="swa_attention",
    )(sinks, q, k, v)


def _layer_norm(y, g, b):
    mu = jnp.mean(y, -1, keepdims=True)
    yc = y - mu
    var = jnp.mean(yc * yc, -1, keepdims=True)
    return yc * lax.rsqrt(var + 1e-5) * g + b


PIECES = 8


def _tm_load(ref, r0, nrows):
    return jnp.concatenate([ref[pl.ds(r0 * PIECES + p, nrows, stride=PIECES), :] for p in range(PIECES)], axis=1)


def _tm_store(ref, r0, val):
    for p in range(PIECES):
        ref[pl.ds(r0 * PIECES + p, val.shape[0], stride=PIECES), :] = val[:, p * LANES:(p + 1) * LANES]


def _out_ln_kernel(x_ref, a_ref, b_ref, w_ref, g_ref, beta_ref, otm_ref, *, alpha):
    ka = a_ref.shape[1]
    mix = (jnp.dot(a_ref[...], w_ref[:ka, :], preferred_element_type=F32)
           + jnp.dot(b_ref[...], w_ref[ka:, :], preferred_element_type=F32))
    _tm_store(otm_ref, 0, _layer_norm(alpha * x_ref[...] + mix, g_ref[...], beta_ref[...]))


def _out_ln(xf, a, b, w, g, beta, alpha):
    n, d = xf.shape
    assert d == PIECES * LANES
    tm = min(PROJ_ROWS, n)
    row = lambda width: pl.BlockSpec((tm, width), lambda i: (i, 0))
    full = lambda arr: pl.BlockSpec(arr.shape, lambda i: (0, 0))
    return pl.pallas_call(
        functools.partial(_out_ln_kernel, alpha=alpha),
        grid=(n // tm,),
        in_specs=[row(d), row(a.shape[1]), row(b.shape[1]), full(w), full(g), full(beta)],
        out_specs=pl.BlockSpec((tm * PIECES, LANES), lambda i: (i, 0)),
        out_shape=jax.ShapeDtypeStruct((n * PIECES, LANES), F32),
        compiler_params=_cparams("parallel"),
        name="out_proj_ln",
    )(xf, a, b, w, g, beta)


def _row_argmax(v, row, nrow):
    mx = jnp.max(v, 0, keepdims=True)
    idx = jnp.min(jnp.where(v == mx, row, nrow), 0, keepdims=True)
    return mx, idx


def _router_kernel(x_ref, w_ref, bias_ref, eidx_ref, wts_ref):
    tm = x_ref.shape[0] // PIECES
    logits = jnp.dot(_tm_load(x_ref, 0, tm), w_ref[...], preferred_element_type=F32,
                     precision=lax.Precision.HIGHEST)
    scores = jax.nn.sigmoid(logits.T[:N_EXPERTS, :])
    biased = scores + bias_ref[...]
    per = N_EXPERTS // N_GROUPS
    row_g = lax.broadcasted_iota(I32, (per, tm), 0)
    gsc = []
    for g in range(N_GROUPS):
        v = biased[g * per:(g + 1) * per, :]
        m1, i1 = _row_argmax(v, row_g, per)
        m2 = jnp.max(jnp.where(row_g == i1, -jnp.inf, v), 0, keepdims=True)
        gsc.append(m1 + m2)
    gscore = jnp.concatenate(gsc, 0)
    row_n = lax.broadcasted_iota(I32, (N_GROUPS, tm), 0)
    gok = jnp.zeros((N_GROUPS, tm), F32)
    for _ in range(TOPK_GROUPS):
        _, ig = _row_argmax(gscore, row_n, N_GROUPS)
        hit = row_n == ig
        gok = jnp.where(hit, 1.0, gok)
        gscore = jnp.where(hit, -jnp.inf, gscore)
    cand = jnp.concatenate([jnp.where(gok[g:g + 1, :] > 0.0, biased[g * per:(g + 1) * per, :], -jnp.inf)
                            for g in range(N_GROUPS)], 0)
    row_e = lax.broadcasted_iota(I32, (N_EXPERTS, tm), 0)
    ids, ws = [], []
    for _ in range(TOP_K):
        _, ie = _row_argmax(cand, row_e, N_EXPERTS)
        hit = row_e == ie
        ids.append(ie)
        ws.append(jnp.sum(jnp.where(hit, scores, 0.0), 0, keepdims=True))
        cand = jnp.where(hit, -jnp.inf, cand)
    w = jnp.concatenate(ws, 0)
    eidx_ref[...] = jnp.concatenate(ids, 0)
    wts_ref[...] = w / jnp.sum(w, 0, keepdims=True) * ROUTED_SCALE


def _router(x_tm, w_router, e_bias):
    n = x_tm.shape[0] // PIECES
    tm = min(ROUTER_ROWS, n)
    wpad = jnp.pad(w_router, ((0, 0), (0, LANES - N_EXPERTS)))
    bcol = e_bias.astype(F32).reshape(N_EXPERTS, 1)
    out = pl.BlockSpec((TOP_K, tm), lambda i: (0, i))
    return pl.pallas_call(
        _router_kernel,
        grid=(n // tm,),
        in_specs=[pl.BlockSpec((tm * PIECES, LANES), lambda i: (i, 0)), pl.BlockSpec(wpad.shape, lambda i: (0, 0)),
                  pl.BlockSpec(bcol.shape, lambda i: (0, 0))],
        out_specs=[out, out],
        out_shape=[jax.ShapeDtypeStruct((TOP_K, n), I32), jax.ShapeDtypeStruct((TOP_K, n), F32)],
        compiler_params=_cparams("parallel"),
        name="router",
    )(x_tm, wpad, bcol)


def _dispatch(eidx_t, wts_t, chunk, tail):
    n = eidx_t.shape[1]
    nc = n // chunk
    a = chunk * TOP_K
    ec = eidx_t.reshape(TOP_K, nc, chunk).transpose(1, 0, 2).reshape(nc, a)
    wc = wts_t.reshape(TOP_K, nc, chunk).transpose(1, 0, 2).reshape(nc, a)
    ex = jnp.arange(N_EXPERTS, dtype=I32)
    counts = jnp.sum((ec[:, :, None] == ex[None, None, :]).astype(I32), axis=1)
    ends = jnp.cumsum(counts, axis=1).astype(I32)
    starts = jnp.concatenate([jnp.zeros((nc, 1), I32), ends], axis=1)
    mult = 1 << int(chunk - 1).bit_length()
    tok_ids = jnp.broadcast_to(jnp.arange(a, dtype=I32)[None, :] % chunk, (nc, a))
    keys, gates = lax.sort((ec * mult + tok_ids, wc), dimension=1, num_keys=1)
    rows = jnp.pad((keys % mult) * PIECES, ((0, 0), (0, tail)))
    gates = jnp.pad(gates, ((0, 0), (0, tail)))
    return starts.reshape(-1), rows.reshape(-1), gates.reshape(-1)


def _swiglu(xb, wg, wu):
    hg = jnp.dot(xb, wg, preferred_element_type=F32)
    hu = jnp.dot(xb, wu, preferred_element_type=F32)
    return hg * jax.nn.sigmoid(hg) * hu


def _moe_kernel(start_ref, tok_ref, gate_ref, x_ref, wg_ref, wu_ref, wd_ref, sg_ref, su_ref, sd_ref,
                g_ref, beta_ref, o_ref, acc_ref, xga_ref, xgb_ref, oga_ref, ogb_ref,
                *, chunk, rows, head, alpha):
    c = pl.program_id(0)
    pair = pl.program_id(1)
    n_e = 2 * pl.num_programs(1)
    group = 8
    sub = min(512, chunk)

    @pl.when((c == 0) & (pair == 0))
    def _():
        for ref in (xga_ref, xgb_ref):
            ref[...] = jnp.zeros_like(ref)

    @pl.when(pair == 0)
    def _():
        for s0 in range(0, chunk, sub):
            xb = _tm_load(x_ref, s0, sub).astype(BF16)
            h = _swiglu(xb, sg_ref[...], su_ref[...]).astype(BF16)
            _tm_store(acc_ref, s0, jnp.dot(h, sd_ref[...], preferred_element_type=F32))
        acc_ref[chunk * PIECES:(chunk + 1) * PIECES, :] = jnp.zeros((PIECES, LANES), F32)

    def run(ei):
        return start_ref[c * (n_e + 1) + ei], start_ref[c * (n_e + 1) + ei + 1]

    def at_row(first):
        return pl.ds(pl.multiple_of(first, PIECES), PIECES)

    def gather_group(xg, base, r0, _):
        for i in range(group):
            xg[at_row((r0 + i) * PIECES), :] = x_ref[at_row(tok_ref[base + r0 + i]), :]

    def scatter_group(og, base, nrow, r0, mask):
        firsts, vals = [], []
        for i in range(group):
            idx = base + r0 + i
            first = tok_ref[idx]
            if mask:
                first = jnp.where(r0 + i < nrow, first, chunk * PIECES)
            firsts.append(first)
            vals.append(acc_ref[at_row(first), :] + gate_ref[idx] * og[at_row((r0 + i) * PIECES), :])
        for i in range(group):
            acc_ref[at_row(firsts[i]), :] = vals[i]

    def sweep(fn, r_lo, r_hi):
        n = jnp.maximum(r_hi - r_lo, 0)
        nfull = n // group

        def body(gi, carry):
            fn(pl.multiple_of(r_lo + gi * group, group), False)
            return carry

        lax.fori_loop(0, nfull, body, 0)
        pl.when(n % group != 0)(lambda: fn(pl.multiple_of(r_lo + nfull * group, group), True))

    def expert(k, xg, og):
        xb = _tm_load(xg, 0, rows).astype(BF16)
        h = _swiglu(xb, wg_ref[k], wu_ref[k]).astype(BF16)
        _tm_store(og, 0, jnp.dot(h, wd_ref[k], preferred_element_type=F32))

    def step(k, xg_cur, xg_next, og_cur, og_prev):
        e = 2 * pair + k
        lo, hi = run(e)
        n_cur = jnp.minimum(hi - lo, rows)
        lo_next, _ = run(jnp.minimum(e + 1, n_e - 1))
        lo_prev, hi_prev = run(jnp.maximum(e - 1, 0))
        n_prev = jnp.where(e > 0, jnp.minimum(hi_prev - lo_prev, rows), 0)
        sweep(lambda r0, p: gather_group(xg_cur, lo, r0, p), jnp.where(e == 0, 0, head), n_cur)
        for r0 in range(0, head, group):
            gather_group(xg_next, lo_next, r0, True)
        expert(k, xg_cur, og_cur)
        for r0 in range(0, head, group):
            scatter_group(og_prev, lo_prev, n_prev, r0, True)
        sweep(lambda r0, p: scatter_group(og_cur, lo, n_cur, r0, p), head, n_cur)

        def later_block(bi, carry):
            base = lo + bi * rows
            nrow = jnp.minimum(hi - base, rows)
            sweep(lambda r0, p: gather_group(xg_cur, base, r0, p), 0, nrow)
            expert(k, xg_cur, og_prev)
            sweep(lambda r0, p: scatter_group(og_prev, base, nrow, r0, p), 0, nrow)
            return carry

        lax.fori_loop(1, (hi - lo + rows - 1) // rows, later_block, 0)
        if k == 1:
            pl.when(e == n_e - 1)(
                lambda: sweep(lambda r0, p: scatter_group(og_cur, lo, n_cur, r0, p), 0, jnp.minimum(n_cur, head)))

    step(0, xga_ref, xgb_ref, oga_ref, ogb_ref)
    step(1, xgb_ref, xga_ref, ogb_ref, oga_ref)

    @pl.when(pair == pl.num_programs(1) - 1)
    def _():
        for s0 in range(0, chunk, sub):
            z = alpha * _tm_load(x_ref, s0, sub) + _tm_load(acc_ref, s0, sub)
            o_ref[s0:s0 + sub, :] = _layer_norm(z, g_ref[...], beta_ref[...])


def _moe_ln(x_tm, eidx, wts, wg, wu, wd, sg, su, sd, g, beta, alpha):
    n = x_tm.shape[0] // PIECES
    d = PIECES * LANES
    chunk = min(MOE_CHUNK, n)
    rows = MOE_ROWS
    tail = -(-rows // 1024) * 1024
    starts, tok, gate = _dispatch(eidx, wts, chunk, tail)
    nc = n // chunk
    n_e, _, de = wg.shape
    assert n_e % 2 == 0
    full = lambda arr: pl.BlockSpec(arr.shape, lambda c, e, s_r: (0,) * arr.ndim)
    lists = pl.BlockSpec((chunk * TOP_K + tail,), lambda c, e, s_r: (c,), memory_space=pltpu.SMEM)
    grid_spec = pltpu.PrefetchScalarGridSpec(
        num_scalar_prefetch=1,
        grid=(nc, n_e // 2),
        in_specs=[
            lists, lists,
            pl.BlockSpec((chunk * PIECES, LANES), lambda c, e, s_r: (c, 0)),
            pl.BlockSpec((2, d, de), lambda c, e, s_r: (e, 0, 0)),
            pl.BlockSpec((2, d, de), lambda c, e, s_r: (e, 0, 0)),
            pl.BlockSpec((2, de, d), lambda c, e, s_r: (e, 0, 0)),
            full(sg), full(su), full(sd), full(g), full(beta),
        ],
        out_specs=pl.BlockSpec((chunk, d), lambda c, e, s_r: (c, 0)),
        scratch_shapes=[pltpu.VMEM(((chunk + 1) * PIECES, LANES), F32)]
        + [pltpu.VMEM((rows * PIECES, LANES), F32) for _ in range(4)],
    )
    return pl.pallas_call(
        functools.partial(_moe_kernel, chunk=chunk, rows=rows, head=min(MOE_HEAD, rows), alpha=alpha),
        grid_spec=grid_spec,
        out_shape=jax.ShapeDtypeStruct((n, d), F32),
        compiler_params=_cparams("arbitrary", "arbitrary"),
        name="moe_experts",
    )(starts, tok, gate, x_tm, wg, wu, wd, sg, su, sd, g, beta)


def _rope_tables(positions):
    pos = positions.reshape(-1).astype(F32)[:, None]
    n = pos.shape[0]

    def cs(dim):
        inv = ROPE_THETA ** (-jnp.arange(0, dim, 2, dtype=F32) / dim)
        ang = pos * inv
        return jnp.cos(ang), jnp.sin(ang)

    c, s = cs(HEAD_DIM)
    c64 = jnp.concatenate([c, c, c, c], -1)
    s64 = jnp.concatenate([-s, s, -s, s], -1)
    c, s = cs(MLA_ROPE)
    one = jnp.ones((n, MLA_NOPE), F32)
    pad = LANES - MLA_NOPE - MLA_ROPE
    cm = jnp.concatenate([one, c, c, jnp.ones((n, pad), F32)], -1)
    sm = jnp.concatenate([0 * one, -s, s, jnp.zeros((n, pad), F32)], -1)
    return (c64, s64), (cm, sm)


def _even_mixer(xf, b, s, tabs, w_in, g_qa, w_qb, g_kva, w_kvb):
    d = xf.shape[1]
    tab64, tabm = tabs
    na = 3 * 512
    kpe_lo = na + 256 + 128
    wpad = jnp.concatenate([w_in[:, :kpe_lo], jnp.zeros((d, MLA_NOPE), F32), w_in[:, kpe_lo:],
                            jnp.zeros((d, LANES - MLA_NOPE - MLA_ROPE), F32)], 1).astype(BF16)
    nh = w_qb.shape[1] // (MLA_NOPE + MLA_ROPE)
    wqb = jnp.pad(w_qb.reshape(-1, nh, MLA_NOPE + MLA_ROPE), ((0, 0), (0, 0), (0, LANES - MLA_NOPE - MLA_ROPE)))
    wqb = wqb.reshape(-1, nh * LANES).astype(BF16)
    wkv3 = w_kvb.reshape(-1, nh, MLA_NOPE + MLA_V)
    wkk = jnp.pad(wkv3[:, :, :MLA_NOPE], ((0, 0), (0, 0), (0, LANES - MLA_NOPE))).reshape(-1, nh * LANES).astype(BF16)
    wkv = wkv3[:, :, MLA_NOPE:].reshape(-1, nh * MLA_V).astype(BF16)
    segs = [(0, 512, True, QK_SCALE), (512, 1024, True, 1.0), (1024, 1536, False, 1.0)]
    qa, ka, va, qm, km, vm = _projection(
        xf, wpad, segs, tab64,
        mla_args=(tabm[0], tabm[1], g_qa.reshape(1, -1), wqb, g_kva.reshape(1, -1), wkk, wkv),
        mla_out_widths=(nh * LANES, nh * LANES, nh * MLA_V))
    o_a = _moba_attention(qa, ka, va, b, s)
    o_b = _mla_attention(qm, km, vm, b, s)
    return o_a, o_b


def _odd_mixer(xf, b, s, tabs, w_in, sinks, lam_p, g_subln, lambda_init):
    segs = [(0, 512, True, QK_SCALE), (512, 640, True, 1.0), (640, 768, False, 1.0),
            (768, 1280, True, QK_SCALE), (1280, 1792, True, 1.0), (1792, 2304, False, 1.0)]
    qc, kc, vc, qd, kd, vd = _projection(xf, w_in.astype(BF16), segs, tabs[0])
    o_c = _swa_attention(qc, kc, vc, sinks.astype(F32), b, s)
    o_d = _diff_attention(qd, kd, vd, lam_p.astype(F32), g_subln.reshape(1, -1), lambda_init, b, s)
    return o_c, o_d


def kernel(x, positions, ev_w_in, ev_w_out, mla_g_qa, mla_w_qb, mla_g_kva, mla_w_kvb, od_w_in, od_w_out, swa_sinks, diff_lambda, diff_g_subln, ln_g, ln_b, moe_w_router, moe_bias, moe_w_gate, moe_w_up, moe_w_down, sh_w_gate, sh_w_up, sh_w_down):
    b, s, d = x.shape
    depth = ln_g.shape[0]
    alpha = float((2 * depth) ** 0.25)
    xf = x.reshape(b * s, d)
    tabs = _rope_tables(positions)
    for l in range(depth):
        i = l // 2
        if l % 2 == 0:
            o1, o2 = _even_mixer(xf, b, s, tabs, ev_w_in[i], mla_g_qa[i], mla_w_qb[i], mla_g_kva[i], mla_w_kvb[i])
            w_out = ev_w_out[i]
        else:
            lambda_init = 0.8 - 0.6 * float(np.exp(-0.3 * l))
            o1, o2 = _odd_mixer(xf, b, s, tabs, od_w_in[i], swa_sinks[i], diff_lambda[i], diff_g_subln[i], lambda_init)
            w_out = od_w_out[i]
        x_tm = _out_ln(xf, o1, o2, w_out.astype(BF16), ln_g[l, 0].reshape(1, d), ln_b[l, 0].reshape(1, d), alpha)
        eidx, wts = _router(x_tm, moe_w_router[l], moe_bias[l])
        xf = _moe_ln(x_tm, eidx, wts, moe_w_gate[l].astype(BF16), moe_w_up[l].astype(BF16),
                     moe_w_down[l].astype(BF16), sh_w_gate[l].astype(BF16), sh_w_up[l].astype(BF16),
                     sh_w_down[l].astype(BF16), ln_g[l, 1].reshape(1, d), ln_b[l, 1].reshape(1, d), alpha)
    return xf.reshape(b, s, d)
```

```python
import functools

import numpy as np
import jax
import jax.numpy as jnp
from jax import lax
from jax.experimental import pallas as pl
from jax.experimental.pallas import tpu as pltpu

F32 = jnp.float32
BF16 = jnp.bfloat16
I32 = jnp.int32

LANES = 128
NEG = -1e30

HEAD_DIM = 64
MOBA_BLOCK = 256
MOBA_TOPK = 3
MLA_NOPE = 64
MLA_ROPE = 32
MLA_V = 64
SWA_WINDOW = 128
N_EXPERTS = 64
N_GROUPS = 8
TOPK_GROUPS = 4
TOP_K = 8
ROUTED_SCALE = 2.5
ROPE_THETA = 10000.0
QK_SCALE = HEAD_DIM ** -0.5
MLA_SCALE = (MLA_NOPE + MLA_ROPE) ** -0.5

PROJ_ROWS = 512
ATTN_TILE = 1024
SWA_ROWS = 512
ROUTER_ROWS = 512
MOE_CHUNK = 2048
MOE_ROWS = 320
MOE_HEAD = 256
VMEM_LIMIT = 56 * 1024 * 1024

_NT = (((1,), (1,)), ((), ()))


def _cparams(*sem):
    return pltpu.CompilerParams(dimension_semantics=sem, vmem_limit_bytes=VMEM_LIMIT)


def _lane_iota(rows):
    return lax.broadcasted_iota(I32, (rows, LANES), 1)


def _rope(x, c, s, first_half, half):
    rot = jnp.where(first_half, pltpu.roll(x, LANES - half, 1), pltpu.roll(x, half, 1))
    return x * c + rot * s


def _rms(x, g, eps):
    return x * lax.rsqrt(jnp.mean(x * x, -1, keepdims=True) + eps) * g


def _proj_kernel(segs, mla, x_ref, w_ref, c64_ref, s64_ref, *refs):
    xb = x_ref[...].astype(BF16)
    rows = xb.shape[0]
    lane = _lane_iota(rows)
    fh64 = (lane % HEAD_DIM) < HEAD_DIM // 2
    c64 = c64_ref[...]
    s64 = s64_ref[...]

    def seg(lo, hi):
        return jnp.dot(xb, w_ref[:, lo:hi], preferred_element_type=F32)

    if mla:
        cm_ref, sm_ref, gq_ref, wqb_ref, gkv_ref, wkk_ref, wkv_ref = refs[:7]
        refs = refs[7:]
    for (lo, hi, rope, scale), o_ref in zip(segs, refs):
        h = seg(lo, hi)
        if rope:
            for g in range((hi - lo) // LANES):
                sl = slice(g * LANES, (g + 1) * LANES)
                o_ref[:, sl] = (_rope(h[:, sl], c64, s64, fh64, HEAD_DIM // 2) * scale).astype(BF16)
        else:
            o_ref[...] = h.astype(BF16)
    if mla:
        qm_ref, km_ref, vm_ref = refs[len(segs):]
        cm = cm_ref[...]
        sm = sm_ref[...]
        fhm = (lane >= MLA_NOPE) & (lane < MLA_NOPE + MLA_ROPE // 2)
        lo = segs[-1][1]
        qn = _rms(seg(lo, lo + 256), gq_ref[...], 1e-6).astype(BF16)
        qm = jnp.dot(qn, wqb_ref[...], preferred_element_type=F32)
        kvn = _rms(seg(lo + 256, lo + 384), gkv_ref[...], 1e-6).astype(BF16)
        kk = jnp.dot(kvn, wkk_ref[...], preferred_element_type=F32)
        kpe = _rope(seg(lo + 384, lo + 512), cm, sm, fhm, MLA_ROPE // 2)
        for g in range(qm.shape[1] // LANES):
            sl = slice(g * LANES, (g + 1) * LANES)
            qm_ref[:, sl] = (_rope(qm[:, sl], cm, sm, fhm, MLA_ROPE // 2) * MLA_SCALE).astype(BF16)
            km_ref[:, sl] = (kk[:, sl] + kpe).astype(BF16)
        vm_ref[...] = jnp.dot(kvn, wkv_ref[...], preferred_element_type=F32).astype(BF16)


def _projection(xf, w, segs, tabs, mla_args=None, mla_out_widths=()):
    n, d = xf.shape
    tm = min(PROJ_ROWS, n)
    row = lambda width: pl.BlockSpec((tm, width), lambda i: (i, 0))
    full = lambda a: pl.BlockSpec(a.shape, lambda i: (0,) * a.ndim)
    ins = [xf, w, tabs[0], tabs[1]]
    in_specs = [row(d), full(w), row(LANES), row(LANES)]
    if mla_args is not None:
        cm, sm, gq, wqb, gkv, wkk, wkv = mla_args
        ins += [cm, sm, gq, wqb, gkv, wkk, wkv]
        in_specs += [row(LANES), row(LANES), full(gq), full(wqb), full(gkv), full(wkk), full(wkv)]
    widths = [hi - lo for lo, hi, _, _ in segs] + list(mla_out_widths)
    return pl.pallas_call(
        functools.partial(_proj_kernel, tuple(segs), mla_args is not None),
        grid=(n // tm,),
        in_specs=in_specs,
        out_specs=[row(wd) for wd in widths],
        out_shape=[jax.ShapeDtypeStruct((n, wd), BF16) for wd in widths],
        compiler_params=_cparams("parallel"),
        name="projection",
    )(*ins)


def _diag_strip(sj, j, block, tri):
    t = sj.shape[0]
    r0 = j * LANES
    lo = (r0 // block) * block
    parts = []
    if lo > 0:
        parts.append(sj[:lo])
    if r0 > lo:
        parts.append(jnp.full((r0 - lo, LANES), NEG, F32))
    parts.append(sj[r0:r0 + LANES] + tri)
    if r0 + LANES < t:
        parts.append(sj[r0 + LANES:])
    return jnp.concatenate(parts, 0)


def _flash_update(c, q, k, v, diag_block, state, first):
    m_ref, l_ref, acc_ref = state
    s = lax.dot_general(q, k, _NT, preferred_element_type=F32)
    t, tk = s.shape
    tiles = [s[:, j * LANES:(j + 1) * LANES] for j in range(tk // LANES)]
    if diag_block is None:
        mx = jnp.max(s, -1, keepdims=True)
    else:
        tri = jnp.where(lax.broadcasted_iota(I32, (LANES, LANES), 1) > lax.broadcasted_iota(I32, (LANES, LANES), 0),
                        NEG, 0.0).astype(F32)
        tiles = [_diag_strip(sj, j, diag_block, tri) for j, sj in enumerate(tiles)]
        pm = tiles[0]
        for sj in tiles[1:]:
            pm = jnp.maximum(pm, sj)
        mx = jnp.max(pm, -1, keepdims=True)
    if first:
        m_new = jnp.broadcast_to(mx, (t, LANES))
    else:
        m_prev = m_ref[c]
        m_new = jnp.maximum(m_prev, mx)
        alpha = jnp.exp(m_prev - m_new)
    ps, psum = [], None
    for sj in tiles:
        pj = jnp.exp(sj - m_new)
        ps.append(pj.astype(BF16))
        psum = pj if psum is None else psum + pj
    pv = jnp.dot(jnp.concatenate(ps, axis=1), v, preferred_element_type=F32)
    if first:
        l_ref[c] = psum
        acc_ref[c] = pv
    else:
        l_ref[c] = alpha * l_ref[c] + psum
        acc_ref[c] = alpha * acc_ref[c] + pv
    m_ref[c] = m_new


def _flash_out(c, state):
    _, l_ref, acc_ref = state
    return acc_ref[c] / jnp.sum(l_ref[c], -1, keepdims=True)


def _flash_state(chains, t):
    return [pltpu.VMEM((chains, t, LANES), F32) for _ in range(3)]


def _causal_sweep(qi, step):
    step(qi, True)

    def body(j, carry):
        step(j, False)
        return carry

    lax.fori_loop(0, qi, body, 0)


def _mla_kernel(q_ref, k_ref, v_ref, o_ref, m_ref, l_ref, acc_ref, *, t):
    qi = pl.program_id(2)
    state = (m_ref, l_ref, acc_ref)

    def step(j, first):
        ks = pl.multiple_of(j * t, t)
        v = v_ref[pl.ds(ks, t), :]
        for h in range(2):
            hs = slice(h * LANES, (h + 1) * LANES)
            _flash_update(h, q_ref[:, hs], k_ref[pl.ds(ks, t), hs], v, t if first else None, state, first)

    _causal_sweep(qi, step)
    lane = _lane_iota(t)
    o_ref[...] = jnp.where(lane < MLA_V, _flash_out(0, state), _flash_out(1, state)).astype(BF16)


def _mla_attention(qm, km, vm, b, s):
    t = min(ATTN_TILE, s)
    nq = s // t
    hp = qm.shape[1] // (2 * LANES)
    return pl.pallas_call(
        functools.partial(_mla_kernel, t=t),
        grid=(b, hp, nq),
        in_specs=[pl.BlockSpec((t, 2 * LANES), lambda bi, h, qi: (bi * nq + qi, h)),
                  pl.BlockSpec((s, 2 * LANES), lambda bi, h, qi: (bi, h)),
                  pl.BlockSpec((s, LANES), lambda bi, h, qi: (bi, h))],
        out_specs=pl.BlockSpec((t, LANES), lambda bi, h, qi: (bi * nq + qi, h)),
        out_shape=jax.ShapeDtypeStruct((b * s, hp * LANES), BF16),
        scratch_shapes=_flash_state(2, t),
        compiler_params=_cparams("parallel", "parallel", "arbitrary"),
        name="mla_attention",
    )(qm, km, vm)


def _moba_kernel(q_ref, k_ref, v_ref, o_ref, kmean_ref, kaug_ref, qaug_ref, m_ref, l_ref, acc_ref, *, t, nb):
    qi = pl.program_id(2)
    bs = MOBA_BLOCK
    s_len = k_ref.shape[0]
    half = LANES // 2

    @pl.when(qi == 0)
    def _():
        kmean_ref[...] = jnp.zeros_like(kmean_ref)
        for n in range(nb):
            blk = k_ref[n * bs:(n + 1) * bs, :].astype(F32)
            kmean_ref[n:n + 1, :] = jnp.sum(blk, 0, keepdims=True) * (1.0 / bs)
        lane_s = _lane_iota(s_len)
        row_blk = lax.broadcasted_iota(I32, (s_len, LANES), 0) // bs
        kf = k_ref[...].astype(F32)
        kaug_ref[0] = jnp.where(lane_s < half, kf, (lane_s - half == row_blk).astype(F32)).astype(BF16)
        kaug_ref[1] = jnp.where(lane_s >= half, kf, (lane_s == row_blk).astype(F32)).astype(BF16)

    lane = _lane_iota(t)
    nbp = -(-nb // 8) * 8
    blk = lax.broadcasted_iota(I32, (nbp, t), 0)
    own = qi * (t // bs) + lax.broadcasted_iota(I32, (nbp, t), 1) // bs
    qf = q_ref[...].astype(F32)
    kmean = kmean_ref[0:nbp, :]
    for h in range(2):
        head = (lane >= h * half) & (lane < (h + 1) * half)
        qh = jnp.where(head, qf, 0.0)
        gate = lax.dot_general(kmean, qh, _NT, preferred_element_type=F32, precision=lax.Precision.HIGHEST)
        g = jnp.where(blk < own, gate, -jnp.inf)
        sel = blk == own
        for _ in range(MOBA_TOPK):
            mx = jnp.max(g, 0, keepdims=True)
            idx = jnp.min(jnp.where(g == mx, blk, nbp), 0, keepdims=True)
            hit = blk == idx
            sel = sel | (hit & (blk < own))
            g = jnp.where(hit, -jnp.inf, g)
        selb = jnp.where(sel | (blk >= nb), 0.0, NEG).astype(F32)
        selb = jnp.concatenate([selb, jnp.zeros((LANES - nbp, t), F32)], 0).T
        if h == 0:
            selb = pltpu.roll(selb, half, 1)
        qaug_ref[h] = jnp.where(head, qf, selb).astype(BF16)

    state = (m_ref, l_ref, acc_ref)

    def step(j, first):
        ks = pl.multiple_of(j * t, t)
        v = v_ref[pl.ds(ks, t), :]
        for h in range(2):
            _flash_update(h, qaug_ref[h], kaug_ref[h, pl.ds(ks, t), :], v, bs if first else None, state, first)

    _causal_sweep(qi, step)
    o_ref[...] = jnp.where(lane < half, _flash_out(0, state), _flash_out(1, state)).astype(BF16)


def _moba_attention(q, k, v, b, s):
    bs = MOBA_BLOCK
    nb = s // bs
    assert nb <= LANES // 2, "block gate must fit the spare half of the lanes"
    t = min(ATTN_TILE, s)
    nq = s // t
    hp = q.shape[1] // LANES
    return pl.pallas_call(
        functools.partial(_moba_kernel, t=t, nb=nb),
        grid=(b, hp, nq),
        in_specs=[pl.BlockSpec((t, LANES), lambda bi, h, qi: (bi * nq + qi, h)),
                  pl.BlockSpec((s, LANES), lambda bi, h, qi: (bi, h)),
                  pl.BlockSpec((s, LANES), lambda bi, h, qi: (bi, h))],
        out_specs=pl.BlockSpec((t, LANES), lambda bi, h, qi: (bi * nq + qi, h)),
        out_shape=jax.ShapeDtypeStruct((b * s, hp * LANES), BF16),
        scratch_shapes=[pltpu.VMEM((LANES, LANES), F32), pltpu.VMEM((2, s, LANES), BF16),
                        pltpu.VMEM((2, t, LANES), BF16)] + _flash_state(2, t),
        compiler_params=_cparams("parallel", "parallel", "arbitrary"),
        name="moba_attention",
    )(q, k, v)


def _diff_kernel(q_ref, k_ref, v_ref, lam_ref, g_ref, o_ref, qs_ref, m_ref, l_ref, acc_ref, *, t, lambda_init):
    qi = pl.program_id(2)
    lane = _lane_iota(t)
    qf = q_ref[...].astype(F32)
    qs_ref[0] = jnp.where(lane < HEAD_DIM, qf, 0.0).astype(BF16)
    qs_ref[1] = jnp.where(lane >= HEAD_DIM, qf, 0.0).astype(BF16)
    state = (m_ref, l_ref, acc_ref)

    def step(j, first):
        ks = pl.multiple_of(j * t, t)
        k = k_ref[pl.ds(ks, t), :]
        v = v_ref[pl.ds(ks, t), :]
        for c in range(2):
            _flash_update(c, qs_ref[c], k, v, t if first else None, state, first)

    _causal_sweep(qi, step)
    lp = lam_ref[...]
    lam = (jnp.exp(jnp.sum(lp[0:1] * lp[1:2], -1, keepdims=True))
           - jnp.exp(jnp.sum(lp[2:3] * lp[3:4], -1, keepdims=True)) + lambda_init)
    o = _flash_out(0, state) - lam * _flash_out(1, state)
    o_ref[...] = (_rms(o, g_ref[...], 1e-5) * (1.0 - lambda_init)).astype(BF16)


def _diff_attention(q, k, v, lam_p, g_subln, lambda_init, b, s):
    t = min(ATTN_TILE, s)
    nq = s // t
    nh = q.shape[1] // LANES
    return pl.pallas_call(
        functools.partial(_diff_kernel, t=t, lambda_init=lambda_init),
        grid=(b, nh, nq),
        in_specs=[pl.BlockSpec((t, LANES), lambda bi, h, qi: (bi * nq + qi, h)),
                  pl.BlockSpec((s, LANES), lambda bi, h, qi: (bi, h)),
                  pl.BlockSpec((s, LANES), lambda bi, h, qi: (bi, h)),
                  pl.BlockSpec(lam_p.shape, lambda bi, h, qi: (0, 0)),
                  pl.BlockSpec(g_subln.shape, lambda bi, h, qi: (0, 0))],
        out_specs=pl.BlockSpec((t, LANES), lambda bi, h, qi: (bi * nq + qi, h)),
        out_shape=jax.ShapeDtypeStruct((b * s, nh * LANES), BF16),
        scratch_shapes=[pltpu.VMEM((2, t, LANES), BF16)] + _flash_state(2, t),
        compiler_params=_cparams("parallel", "parallel", "arbitrary"),
        name="diff_attention",
    )(q, k, v, lam_p, g_subln)


def _swa_kernel(sink_ref, q_ref, k_ref, v_ref, o_ref, *, rows):
    ti = pl.program_id(1)
    w = SWA_WINDOW
    nsub = rows // w
    n_q = q_ref.shape[1] // HEAD_DIM
    n_kv = k_ref.shape[1] // HEAD_DIM
    grp = n_q // n_kv
    lane_k = lax.broadcasted_iota(I32, (2 * w, LANES), 1)
    lane_o = _lane_iota(w)
    qrow = lax.broadcasted_iota(I32, (grp * w, 2 * w), 0) % w
    kcol = lax.broadcasted_iota(I32, (grp * w, 2 * w), 1)
    for i in range(nsub):
        sb = ti * nsub + i
        start = pl.multiple_of(jnp.maximum(sb - 1, 0) * w, w)
        kb = k_ref[pl.ds(start, 2 * w), :]
        vb = v_ref[pl.ds(start, 2 * w), :]
        rel = qrow + (sb * w - start) - kcol
        bias = jnp.where((rel >= 0) & (rel < w), 0.0, NEG).astype(F32)
        qf = q_ref[i * w:(i + 1) * w, :].astype(F32)
        v_ones = jnp.concatenate([vb, jnp.ones((2 * w, LANES), BF16)], axis=1)
        head_out = [None] * n_q
        for kvh in range(n_kv):
            km = jnp.where(lane_k // HEAD_DIM == kvh, kb.astype(F32), 0.0).astype(BF16)
            qs, sinks = [], []
            for a in range(grp):
                h = kvh * grp + a
                qp = qf[:, (h // 2) * LANES:(h // 2 + 1) * LANES]
                if h % 2 != kvh:
                    qp = pltpu.roll(qp, HEAD_DIM, 1)
                qs.append(qp.astype(BF16))
                sinks.append(jnp.full((w, LANES), sink_ref[h], F32))
            sink = jnp.concatenate(sinks, 0)
            s = lax.dot_general(jnp.concatenate(qs, 0), km, _NT, preferred_element_type=F32) + bias
            m = jnp.maximum(jnp.max(s, -1, keepdims=True), sink)
            p = jnp.concatenate([jnp.exp(s[:, j * LANES:(j + 1) * LANES] - m) for j in range(2 * w // LANES)], 1)
            pv = jnp.dot(p.astype(BF16), v_ones, preferred_element_type=F32)
            o = pv[:, :LANES] / (pv[:, LANES:] + jnp.exp(sink - m))
            for a in range(grp):
                h = kvh * grp + a
                oh = o[a * w:(a + 1) * w, :]
                if h % 2 != kvh:
                    oh = pltpu.roll(oh, HEAD_DIM, 1)
                head_out[h] = oh
        for hp in range(n_q // 2):
            pair = jnp.where(lane_o < HEAD_DIM, head_out[2 * hp], head_out[2 * hp + 1])
            o_ref[i * w:(i + 1) * w, hp * LANES:(hp + 1) * LANES] = pair.astype(BF16)


def _swa_attention(q, k, v, sinks, b, s):
    rows = min(SWA_ROWS, s)
    nt = s // rows
    return pl.pallas_call(
        functools.partial(_swa_kernel, rows=rows),
        grid=(b, nt),
        in_specs=[pl.BlockSpec(memory_space=pltpu.SMEM),
                  pl.BlockSpec((rows, q.shape[1]), lambda bi, ti: (bi * nt + ti, 0)),
                  pl.BlockSpec((s, k.shape[1]), lambda bi, ti: (bi, 0)),
                  pl.BlockSpec((s, v.shape[1]), lambda bi, ti: (bi, 0))],
        out_specs=pl.BlockSpec((rows, q.shape[1]), lambda bi, ti: (bi * nt + ti, 0)),
        out_shape=jax.ShapeDtypeStruct(q.shape, BF16),
        compiler_params=_cparams("parallel", "arbitrary"),
        name="swa_attention",
    )(sinks, q, k, v)


def _layer_norm(y, g, b):
    mu = jnp.mean(y, -1, keepdims=True)
    yc = y - mu
    var = jnp.mean(yc * yc, -1, keepdims=True)
    return yc * lax.rsqrt(var + 1e-5) * g + b


PIECES = 8


def _tm_load(ref, r0, nrows):
    return jnp.concatenate([ref[pl.ds(r0 * PIECES + p, nrows, stride=PIECES), :] for p in range(PIECES)], axis=1)


def _tm_store(ref, r0, val):
    for p in range(PIECES):
        ref[pl.ds(r0 * PIECES + p, val.shape[0], stride=PIECES), :] = val[:, p * LANES:(p + 1) * LANES]


def _out_ln_kernel(x_ref, a_ref, b_ref, w_ref, g_ref, beta_ref, otm_ref, *, alpha):
    ka = a_ref.shape[1]
    mix = (jnp.dot(a_ref[...], w_ref[:ka, :], preferred_element_type=F32)
           + jnp.dot(b_ref[...], w_ref[ka:, :], preferred_element_type=F32))
    _tm_store(otm_ref, 0, _layer_norm(alpha * x_ref[...] + mix, g_ref[...], beta_ref[...]))


def _out_ln(xf, a, b, w, g, beta, alpha):
    n, d = xf.shape
    assert d == PIECES * LANES
    tm = min(PROJ_ROWS, n)
    row = lambda width: pl.BlockSpec((tm, width), lambda i: (i, 0))
    full = lambda arr: pl.BlockSpec(arr.shape, lambda i: (0, 0))
    return pl.pallas_call(
        functools.partial(_out_ln_kernel, alpha=alpha),
        grid=(n // tm,),
        in_specs=[row(d), row(a.shape[1]), row(b.shape[1]), full(w), full(g), full(beta)],
        out_specs=pl.BlockSpec((tm * PIECES, LANES), lambda i: (i, 0)),
        out_shape=jax.ShapeDtypeStruct((n * PIECES, LANES), F32),
        compiler_params=_cparams("parallel"),
        name="out_proj_ln",
    )(xf, a, b, w, g, beta)


def _row_argmax(v, row, nrow):
    mx = jnp.max(v, 0, keepdims=True)
    idx = jnp.min(jnp.where(v == mx, row, nrow), 0, keepdims=True)
    return mx, idx


def _router_kernel(x_ref, w_ref, bias_ref, eidx_ref, wts_ref):
    tm = x_ref.shape[0] // PIECES
    logits = jnp.dot(_tm_load(x_ref, 0, tm), w_ref[...], preferred_element_type=F32,
                     precision=lax.Precision.HIGHEST)
    scores = jax.nn.sigmoid(logits.T[:N_EXPERTS, :])
    biased = scores + bias_ref[...]
    per = N_EXPERTS // N_GROUPS
    row_g = lax.broadcasted_iota(I32, (per, tm), 0)
    gsc = []
    for g in range(N_GROUPS):
        v = biased[g * per:(g + 1) * per, :]
        m1, i1 = _row_argmax(v, row_g, per)
        m2 = jnp.max(jnp.where(row_g == i1, -jnp.inf, v), 0, keepdims=True)
        gsc.append(m1 + m2)
    gscore = jnp.concatenate(gsc, 0)
    row_n = lax.broadcasted_iota(I32, (N_GROUPS, tm), 0)
    gok = jnp.zeros((N_GROUPS, tm), F32)
    for _ in range(TOPK_GROUPS):
        _, ig = _row_argmax(gscore, row_n, N_GROUPS)
        hit = row_n == ig
        gok = jnp.where(hit, 1.0, gok)
        gscore = jnp.where(hit, -jnp.inf, gscore)
    cand = jnp.concatenate([jnp.where(gok[g:g + 1, :] > 0.0, biased[g * per:(g + 1) * per, :], -jnp.inf)
                            for g in range(N_GROUPS)], 0)
    row_e = lax.broadcasted_iota(I32, (N_EXPERTS, tm), 0)
    ids, ws = [], []
    for _ in range(TOP_K):
        _, ie = _row_argmax(cand, row_e, N_EXPERTS)
        hit = row_e == ie
        ids.append(ie)
        ws.append(jnp.sum(jnp.where(hit, scores, 0.0), 0, keepdims=True))
        cand = jnp.where(hit, -jnp.inf, cand)
    w = jnp.concatenate(ws, 0)
    eidx_ref[...] = jnp.concatenate(ids, 0)
    wts_ref[...] = w / jnp.sum(w, 0, keepdims=True) * ROUTED_SCALE


def _router(x_tm, w_router, e_bias):
    n = x_tm.shape[0] // PIECES
    tm = min(ROUTER_ROWS, n)
    wpad = jnp.pad(w_router, ((0, 0), (0, LANES - N_EXPERTS)))
    bcol = e_bias.astype(F32).reshape(N_EXPERTS, 1)
    out = pl.BlockSpec((TOP_K, tm), lambda i: (0, i))
    return pl.pallas_call(
        _router_kernel,
        grid=(n // tm,),
        in_specs=[pl.BlockSpec((tm * PIECES, LANES), lambda i: (i, 0)), pl.BlockSpec(wpad.shape, lambda i: (0, 0)),
                  pl.BlockSpec(bcol.shape, lambda i: (0, 0))],
        out_specs=[out, out],
        out_shape=[jax.ShapeDtypeStruct((TOP_K, n), I32), jax.ShapeDtypeStruct((TOP_K, n), F32)],
        compiler_params=_cparams("parallel"),
        name="router",
    )(x_tm, wpad, bcol)


def _dispatch(eidx_t, wts_t, chunk, tail):
    n = eidx_t.shape[1]
    nc = n // chunk
    a = chunk * TOP_K
    ec = eidx_t.reshape(TOP_K, nc, chunk).transpose(1, 0, 2).reshape(nc, a)
    wc = wts_t.reshape(TOP_K, nc, chunk).transpose(1, 0, 2).reshape(nc, a)
    ex = jnp.arange(N_EXPERTS, dtype=I32)
    counts = jnp.sum((ec[:, :, None] == ex[None, None, :]).astype(I32), axis=1)
    ends = jnp.cumsum(counts, axis=1).astype(I32)
    starts = jnp.concatenate([jnp.zeros((nc, 1), I32), ends], axis=1)
    mult = 1 << int(chunk - 1).bit_length()
    tok_ids = jnp.broadcast_to(jnp.arange(a, dtype=I32)[None, :] % chunk, (nc, a))
    keys, gates = lax.sort((ec * mult + tok_ids, wc), dimension=1, num_keys=1)
    rows = jnp.pad((keys % mult) * PIECES, ((0, 0), (0, tail)))
    gates = jnp.pad(gates, ((0, 0), (0, tail)))
    return starts.reshape(-1), rows.reshape(-1), gates.reshape(-1)


def _swiglu(xb, wg, wu):
    hg = jnp.dot(xb, wg, preferred_element_type=F32)
    hu = jnp.dot(xb, wu, preferred_element_type=F32)
    return hg * jax.nn.sigmoid(hg) * hu


def _moe_kernel(start_ref, tok_ref, gate_ref, x_ref, wg_ref, wu_ref, wd_ref, sg_ref, su_ref, sd_ref,
                g_ref, beta_ref, o_ref, acc_ref, xga_ref, xgb_ref, oga_ref, ogb_ref,
                *, chunk, rows, head, alpha):
    c = pl.program_id(0)
    pair = pl.program_id(1)
    n_e = 2 * pl.num_programs(1)
    group = 8
    sub = min(512, chunk)

    @pl.when((c == 0) & (pair == 0))
    def _():
        for ref in (xga_ref, xgb_ref):
            ref[...] = jnp.zeros_like(ref)

    @pl.when(pair == 0)
    def _():
        for s0 in range(0, chunk, sub):
            xb = _tm_load(x_ref, s0, sub).astype(BF16)
            h = _swiglu(xb, sg_ref[...], su_ref[...]).astype(BF16)
            _tm_store(acc_ref, s0, jnp.dot(h, sd_ref[...], preferred_element_type=F32))
        acc_ref[chunk * PIECES:(chunk + 1) * PIECES, :] = jnp.zeros((PIECES, LANES), F32)

    def run(ei):
        return start_ref[c * (n_e + 1) + ei], start_ref[c * (n_e + 1) + ei + 1]

    def at_row(first):
        return pl.ds(pl.multiple_of(first, PIECES), PIECES)

    def gather_group(xg, base, r0, _):
        for i in range(group):
            xg[at_row((r0 + i) * PIECES), :] = x_ref[at_row(tok_ref[base + r0 + i]), :]

    def scatter_group(og, base, nrow, r0, mask):
        firsts, vals = [], []
        for i in range(group):
            idx = base + r0 + i
            first = tok_ref[idx]
            if mask:
                first = jnp.where(r0 + i < nrow, first, chunk * PIECES)
            firsts.append(first)
            vals.append(acc_ref[at_row(first), :] + gate_ref[idx] * og[at_row((r0 + i) * PIECES), :])
        for i in range(group):
            acc_ref[at_row(firsts[i]), :] = vals[i]

    def sweep(fn, r_lo, r_hi):
        n = jnp.maximum(r_hi - r_lo, 0)
        nfull = n // group

        def body(gi, carry):
            fn(pl.multiple_of(r_lo + gi * group, group), False)
            return carry

        lax.fori_loop(0, nfull, body, 0)
        pl.when(n % group != 0)(lambda: fn(pl.multiple_of(r_lo + nfull * group, group), True))

    def expert(k, xg, og):
        xb = _tm_load(xg, 0, rows).astype(BF16)
        h = _swiglu(xb, wg_ref[k], wu_ref[k]).astype(BF16)
        _tm_store(og, 0, jnp.dot(h, wd_ref[k], preferred_element_type=F32))

    def step(k, xg_cur, xg_next, og_cur, og_prev):
        e = 2 * pair + k
        lo, hi = run(e)
        n_cur = jnp.minimum(hi - lo, rows)
        lo_next, _ = run(jnp.minimum(e + 1, n_e - 1))
        lo_prev, hi_prev = run(jnp.maximum(e - 1, 0))
        n_prev = jnp.where(e > 0, jnp.minimum(hi_prev - lo_prev, rows), 0)
        sweep(lambda r0, p: gather_group(xg_cur, lo, r0, p), jnp.where(e == 0, 0, head), n_cur)
        for r0 in range(0, head, group):
            gather_group(xg_next, lo_next, r0, True)
        expert(k, xg_cur, og_cur)
        for r0 in range(0, head, group):
            scatter_group(og_prev, lo_prev, n_prev, r0, True)
        sweep(lambda r0, p: scatter_group(og_cur, lo, n_cur, r0, p), head, n_cur)

        def later_block(bi, carry):
            base = lo + bi * rows
            nrow = jnp.minimum(hi - base, rows)
            sweep(lambda r0, p: gather_group(xg_cur, base, r0, p), 0, nrow)
            expert(k, xg_cur, og_prev)
            sweep(lambda r0, p: scatter_group(og_prev, base, nrow, r0, p), 0, nrow)
            return carry

        lax.fori_loop(1, (hi - lo + rows - 1) // rows, later_block, 0)
        if k == 1:
            pl.when(e == n_e - 1)(
                lambda: sweep(lambda r0, p: scatter_group(og_cur, lo, n_cur, r0, p), 0, jnp.minimum(n_cur, head)))

    step(0, xga_ref, xgb_ref, oga_ref, ogb_ref)
    step(1, xgb_ref, xga_ref, ogb_ref, oga_ref)

    @pl.when(pair == pl.num_programs(1) - 1)
    def _():
        for s0 in range(0, chunk, sub):
            z = alpha * _tm_load(x_ref, s0, sub) + _tm_load(acc_ref, s0, sub)
            o_ref[s0:s0 + sub, :] = _layer_norm(z, g_ref[...], beta_ref[...])


def _moe_ln(x_tm, eidx, wts, wg, wu, wd, sg, su, sd, g, beta, alpha):
    n = x_tm.shape[0] // PIECES
    d = PIECES * LANES
    chunk = min(MOE_CHUNK, n)
    rows = MOE_ROWS
    tail = -(-rows // 1024) * 1024
    starts, tok, gate = _dispatch(eidx, wts, chunk, tail)
    nc = n // chunk
    n_e, _, de = wg.shape
    assert n_e % 2 == 0
    full = lambda arr: pl.BlockSpec(arr.shape, lambda c, e, s_r: (0,) * arr.ndim)
    lists = pl.BlockSpec((chunk * TOP_K + tail,), lambda c, e, s_r: (c,), memory_space=pltpu.SMEM)
    grid_spec = pltpu.PrefetchScalarGridSpec(
        num_scalar_prefetch=1,
        grid=(nc, n_e // 2),
        in_specs=[
            lists, lists,
            pl.BlockSpec((chunk * PIECES, LANES), lambda c, e, s_r: (c, 0)),
            pl.BlockSpec((2, d, de), lambda c, e, s_r: (e, 0, 0)),
            pl.BlockSpec((2, d, de), lambda c, e, s_r: (e, 0, 0)),
            pl.BlockSpec((2, de, d), lambda c, e, s_r: (e, 0, 0)),
            full(sg), full(su), full(sd), full(g), full(beta),
        ],
        out_specs=pl.BlockSpec((chunk, d), lambda c, e, s_r: (c, 0)),
        scratch_shapes=[pltpu.VMEM(((chunk + 1) * PIECES, LANES), F32)]
        + [pltpu.VMEM((rows * PIECES, LANES), F32) for _ in range(4)],
    )
    return pl.pallas_call(
        functools.partial(_moe_kernel, chunk=chunk, rows=rows, head=min(MOE_HEAD, rows), alpha=alpha),
        grid_spec=grid_spec,
        out_shape=jax.ShapeDtypeStruct((n, d), F32),
        compiler_params=_cparams("arbitrary", "arbitrary"),
        name="moe_experts",
    )(starts, tok, gate, x_tm, wg, wu, wd, sg, su, sd, g, beta)


def _rope_tables(positions):
    pos = positions.reshape(-1).astype(F32)[:, None]
    n = pos.shape[0]

    def cs(dim):
        inv = ROPE_THETA ** (-jnp.arange(0, dim, 2, dtype=F32) / dim)
        ang = pos * inv
        return jnp.cos(ang), jnp.sin(ang)

    c, s = cs(HEAD_DIM)
    c64 = jnp.concatenate([c, c, c, c], -1)
    s64 = jnp.concatenate([-s, s, -s, s], -1)
    c, s = cs(MLA_ROPE)
    one = jnp.ones((n, MLA_NOPE), F32)
    pad = LANES - MLA_NOPE - MLA_ROPE
    cm = jnp.concatenate([one, c, c, jnp.ones((n, pad), F32)], -1)
    sm = jnp.concatenate([0 * one, -s, s, jnp.zeros((n, pad), F32)], -1)
    return (c64, s64), (cm, sm)


def _even_mixer(xf, b, s, tabs, w_in, g_qa, w_qb, g_kva, w_kvb):
    d = xf.shape[1]
    tab64, tabm = tabs
    na = 3 * 512
    kpe_lo = na + 256 + 128
    wpad = jnp.concatenate([w_in[:, :kpe_lo], jnp.zeros((d, MLA_NOPE), F32), w_in[:, kpe_lo:],
                            jnp.zeros((d, LANES - MLA_NOPE - MLA_ROPE), F32)], 1).astype(BF16)
    nh = w_qb.shape[1] // (MLA_NOPE + MLA_ROPE)
    wqb = jnp.pad(w_qb.reshape(-1, nh, MLA_NOPE + MLA_ROPE), ((0, 0), (0, 0), (0, LANES - MLA_NOPE - MLA_ROPE)))
    wqb = wqb.reshape(-1, nh * LANES).astype(BF16)
    wkv3 = w_kvb.reshape(-1, nh, MLA_NOPE + MLA_V)
    wkk = jnp.pad(wkv3[:, :, :MLA_NOPE], ((0, 0), (0, 0), (0, LANES - MLA_NOPE))).reshape(-1, nh * LANES).astype(BF16)
    wkv = wkv3[:, :, MLA_NOPE:].reshape(-1, nh * MLA_V).astype(BF16)
    segs = [(0, 512, True, QK_SCALE), (512, 1024, True, 1.0), (1024, 1536, False, 1.0)]
    qa, ka, va, qm, km, vm = _projection(
        xf, wpad, segs, tab64,
        mla_args=(tabm[0], tabm[1], g_qa.reshape(1, -1), wqb, g_kva.reshape(1, -1), wkk, wkv),
        mla_out_widths=(nh * LANES, nh * LANES, nh * MLA_V))
    o_a = _moba_attention(qa, ka, va, b, s)
    o_b = _mla_attention(qm, km, vm, b, s)
    return o_a, o_b


def _odd_mixer(xf, b, s, tabs, w_in, sinks, lam_p, g_subln, lambda_init):
    segs = [(0, 512, True, QK_SCALE), (512, 640, True, 1.0), (640, 768, False, 1.0),
            (768, 1280, True, QK_SCALE), (1280, 1792, True, 1.0), (1792, 2304, False, 1.0)]
    qc, kc, vc, qd, kd, vd = _projection(xf, w_in.astype(BF16), segs, tabs[0])
    o_c = _swa_attention(qc, kc, vc, sinks.astype(F32), b, s)
    o_d = _diff_attention(qd, kd, vd, lam_p.astype(F32), g_subln.reshape(1, -1), lambda_init, b, s)
    return o_c, o_d


def kernel(x, positions, ev_w_in, ev_w_out, mla_g_qa, mla_w_qb, mla_g_kva, mla_w_kvb, od_w_in, od_w_out, swa_sinks, diff_lambda, diff_g_subln, ln_g, ln_b, moe_w_router, moe_bias, moe_w_gate, moe_w_up, moe_w_down, sh_w_gate, sh_w_up, sh_w_down):
    b, s, d = x.shape
    depth = ln_g.shape[0]
    alpha = float((2 * depth) ** 0.25)
    xf = x.reshape(b * s, d)
    tabs = _rope_tables(positions)
    for l in range(depth):
        i = l // 2
        if l % 2 == 0:
            o1, o2 = _even_mixer(xf, b, s, tabs, ev_w_in[i], mla_g_qa[i], mla_w_qb[i], mla_g_kva[i], mla_w_kvb[i])
            w_out = ev_w_out[i]
        else:
            lambda_init = 0.8 - 0.6 * float(np.exp(-0.3 * l))
            o1, o2 = _odd_mixer(xf, b, s, tabs, od_w_in[i], swa_sinks[i], diff_lambda[i], diff_g_subln[i], lambda_init)
            w_out = od_w_out[i]
        x_tm = _out_ln(xf, o1, o2, w_out.astype(BF16), ln_g[l, 0].reshape(1, d), ln_b[l, 0].reshape(1, d), alpha)
        eidx, wts = _router(x_tm, moe_w_router[l], moe_bias[l])
        xf = _moe_ln(x_tm, eidx, wts, moe_w_gate[l].astype(BF16), moe_w_up[l].astype(BF16),
                     moe_w_down[l].astype(BF16), sh_w_gate[l].astype(BF16), sh_w_up[l].astype(BF16),
                     sh_w_down[l].astype(BF16), ln_g[l, 1].reshape(1, d), ln_b[l, 1].reshape(1, d), alpha)
    return xf.reshape(b, s, d)
```

```python
import functools

import numpy as np
import jax
import jax.numpy as jnp
from jax import lax
from jax.experimental import pallas as pl
from jax.experimental.pallas import tpu as pltpu

F32 = jnp.float32
BF16 = jnp.bfloat16
I32 = jnp.int32

LANES = 128
NEG = -1e30

HEAD_DIM = 64
MOBA_BLOCK = 256
MOBA_TOPK = 3
MLA_NOPE = 64
MLA_ROPE = 32
MLA_V = 64
SWA_WINDOW = 128
N_EXPERTS = 64
N_GROUPS = 8
TOPK_GROUPS = 4
TOP_K = 8
ROUTED_SCALE = 2.5
ROPE_THETA = 10000.0
QK_SCALE = HEAD_DIM ** -0.5
MLA_SCALE = (MLA_NOPE + MLA_ROPE) ** -0.5

PROJ_ROWS = 512
ATTN_TILE = 1024
SWA_ROWS = 512
ROUTER_ROWS = 512
MOE_CHUNK = 2048
MOE_ROWS = 320
MOE_HEAD = 256
VMEM_LIMIT = 56 * 1024 * 1024

_NT = (((1,), (1,)), ((), ()))


def _cparams(*sem):
    return pltpu.CompilerParams(dimension_semantics=sem, vmem_limit_bytes=VMEM_LIMIT)


def _lane_iota(rows):
    return lax.broadcasted_iota(I32, (rows, LANES), 1)


def _rope(x, c, s, first_half, half):
    rot = jnp.where(first_half, pltpu.roll(x, LANES - half, 1), pltpu.roll(x, half, 1))
    return x * c + rot * s


def _rms(x, g, eps):
    return x * lax.rsqrt(jnp.mean(x * x, -1, keepdims=True) + eps) * g


def _proj_kernel(segs, mla, x_ref, w_ref, c64_ref, s64_ref, *refs):
    xb = x_ref[...].astype(BF16)
    rows = xb.shape[0]
    lane = _lane_iota(rows)
    fh64 = (lane % HEAD_DIM) < HEAD_DIM // 2
    c64 = c64_ref[...]
    s64 = s64_ref[...]

    def seg(lo, hi):
        return jnp.dot(xb, w_ref[:, lo:hi], preferred_element_type=F32)

    if mla:
        cm_ref, sm_ref, gq_ref, wqb_ref, gkv_ref, wkk_ref, wkv_ref = refs[:7]
        refs = refs[7:]
    for (lo, hi, rope, scale), o_ref in zip(segs, refs):
        h = seg(lo, hi)
        if rope:
            for g in range((hi - lo) // LANES):
                sl = slice(g * LANES, (g + 1) * LANES)
                o_ref[:, sl] = (_rope(h[:, sl], c64, s64, fh64, HEAD_DIM // 2) * scale).astype(BF16)
        else:
            o_ref[...] = h.astype(BF16)
    if mla:
        qm_ref, km_ref, vm_ref = refs[len(segs):]
        cm = cm_ref[...]
        sm = sm_ref[...]
        fhm = (lane >= MLA_NOPE) & (lane < MLA_NOPE + MLA_ROPE // 2)
        lo = segs[-1][1]
        qn = _rms(seg(lo, lo + 256), gq_ref[...], 1e-6).astype(BF16)
        qm = jnp.dot(qn, wqb_ref[...], preferred_element_type=F32)
        kvn = _rms(seg(lo + 256, lo + 384), gkv_ref[...], 1e-6).astype(BF16)
        kk = jnp.dot(kvn, wkk_ref[...], preferred_element_type=F32)
        kpe = _rope(seg(lo + 384, lo + 512), cm, sm, fhm, MLA_ROPE // 2)
        for g in range(qm.shape[1] // LANES):
            sl = slice(g * LANES, (g + 1) * LANES)
            qm_ref[:, sl] = (_rope(qm[:, sl], cm, sm, fhm, MLA_ROPE // 2) * MLA_SCALE).astype(BF16)
            km_ref[:, sl] = (kk[:, sl] + kpe).astype(BF16)
        vm_ref[...] = jnp.dot(kvn, wkv_ref[...], preferred_element_type=F32).astype(BF16)


def _projection(xf, w, segs, tabs, mla_args=None, mla_out_widths=()):
    n, d = xf.shape
    tm = min(PROJ_ROWS, n)
    row = lambda width: pl.BlockSpec((tm, width), lambda i: (i, 0))
    full = lambda a: pl.BlockSpec(a.shape, lambda i: (0,) * a.ndim)
    ins = [xf, w, tabs[0], tabs[1]]
    in_specs = [row(d), full(w), row(LANES), row(LANES)]
    if mla_args is not None:
        cm, sm, gq, wqb, gkv, wkk, wkv = mla_args
        ins += [cm, sm, gq, wqb, gkv, wkk, wkv]
        in_specs += [row(LANES), row(LANES), full(gq), full(wqb), full(gkv), full(wkk), full(wkv)]
    widths = [hi - lo for lo, hi, _, _ in segs] + list(mla_out_widths)
    return pl.pallas_call(
        functools.partial(_proj_kernel, tuple(segs), mla_args is not None),
        grid=(n // tm,),
        in_specs=in_specs,
        out_specs=[row(wd) for wd in widths],
        out_shape=[jax.ShapeDtypeStruct((n, wd), BF16) for wd in widths],
        compiler_params=_cparams("parallel"),
        name="projection",
    )(*ins)


def _diag_strip(sj, j, block, tri):
    t = sj.shape[0]
    r0 = j * LANES
    lo = (r0 // block) * block
    parts = []
    if lo > 0:
        parts.append(sj[:lo])
    if r0 > lo:
        parts.append(jnp.full((r0 - lo, LANES), NEG, F32))
    parts.append(sj[r0:r0 + LANES] + tri)
    if r0 + LANES < t:
        parts.append(sj[r0 + LANES:])
    return jnp.concatenate(parts, 0)


def _flash_update(c, q, k, v, diag_block, state, first):
    m_ref, l_ref, acc_ref = state
    s = lax.dot_general(q, k, _NT, preferred_element_type=F32)
    t, tk = s.shape
    tiles = [s[:, j * LANES:(j + 1) * LANES] for j in range(tk // LANES)]
    if diag_block is None:
        mx = jnp.max(s, -1, keepdims=True)
    else:
        tri = jnp.where(lax.broadcasted_iota(I32, (LANES, LANES), 1) > lax.broadcasted_iota(I32, (LANES, LANES), 0),
                        NEG, 0.0).astype(F32)
        tiles = [_diag_strip(sj, j, diag_block, tri) for j, sj in enumerate(tiles)]
        pm = tiles[0]
        for sj in tiles[1:]:
            pm = jnp.maximum(pm, sj)
        mx = jnp.max(pm, -1, keepdims=True)
    if first:
        m_new = jnp.broadcast_to(mx, (t, LANES))
    else:
        m_prev = m_ref[c]
        m_new = jnp.maximum(m_prev, mx)
        alpha = jnp.exp(m_prev - m_new)
    ps, psum = [], None
    for sj in tiles:
        pj = jnp.exp(sj - m_new)
        ps.append(pj.astype(BF16))
        psum = pj if psum is None else psum + pj
    pv = jnp.dot(jnp.concatenate(ps, axis=1), v, preferred_element_type=F32)
    if first:
        l_ref[c] = psum
        acc_ref[c] = pv
    else:
        l_ref[c] = alpha * l_ref[c] + psum
        acc_ref[c] = alpha * acc_ref[c] + pv
    m_ref[c] = m_new


def _flash_out(c, state):
    _, l_ref, acc_ref = state
    return acc_ref[c] / jnp.sum(l_ref[c], -1, keepdims=True)


def _flash_state(chains, t):
    return [pltpu.VMEM((chains, t, LANES), F32) for _ in range(3)]


def _causal_sweep(qi, step):
    step(qi, True)

    def body(j, carry):
        step(j, False)
        return carry

    lax.fori_loop(0, qi, body, 0)


def _mla_kernel(q_ref, k_ref, v_ref, o_ref, m_ref, l_ref, acc_ref, *, t):
    qi = pl.program_id(2)
    state = (m_ref, l_ref, acc_ref)

    def step(j, first):
        ks = pl.multiple_of(j * t, t)
        v = v_ref[pl.ds(ks, t), :]
        for h in range(2):
            hs = slice(h * LANES, (h + 1) * LANES)
            _flash_update(h, q_ref[:, hs], k_ref[pl.ds(ks, t), hs], v, t if first else None, state, first)

    _causal_sweep(qi, step)
    lane = _lane_iota(t)
    o_ref[...] = jnp.where(lane < MLA_V, _flash_out(0, state), _flash_out(1, state)).astype(BF16)


def _mla_attention(qm, km, vm, b, s):
    t = min(ATTN_TILE, s)
    nq = s // t
    hp = qm.shape[1] // (2 * LANES)
    return pl.pallas_call(
        functools.partial(_mla_kernel, t=t),
        grid=(b, hp, nq),
        in_specs=[pl.BlockSpec((t, 2 * LANES), lambda bi, h, qi: (bi * nq + qi, h)),
                  pl.BlockSpec((s, 2 * LANES), lambda bi, h, qi: (bi, h)),
                  pl.BlockSpec((s, LANES), lambda bi, h, qi: (bi, h))],
        out_specs=pl.BlockSpec((t, LANES), lambda bi, h, qi: (bi * nq + qi, h)),
        out_shape=jax.ShapeDtypeStruct((b * s, hp * LANES), BF16),
        scratch_shapes=_flash_state(2, t),
        compiler_params=_cparams("parallel", "parallel", "arbitrary"),
        name="mla_attention",
    )(qm, km, vm)


def _moba_kernel(q_ref, k_ref, v_ref, o_ref, kmean_ref, kaug_ref, qaug_ref, m_ref, l_ref, acc_ref, *, t, nb):
    qi = pl.program_id(2)
    bs = MOBA_BLOCK
    s_len = k_ref.shape[0]
    half = LANES // 2

    @pl.when(qi == 0)
    def _():
        kmean_ref[...] = jnp.zeros_like(kmean_ref)
        for n in range(nb):
            blk = k_ref[n * bs:(n + 1) * bs, :].astype(F32)
            kmean_ref[n:n + 1, :] = jnp.sum(blk, 0, keepdims=True) * (1.0 / bs)
        lane_s = _lane_iota(s_len)
        row_blk = lax.broadcasted_iota(I32, (s_len, LANES), 0) // bs
        kf = k_ref[...].astype(F32)
        kaug_ref[0] = jnp.where(lane_s < half, kf, (lane_s - half == row_blk).astype(F32)).astype(BF16)
        kaug_ref[1] = jnp.where(lane_s >= half, kf, (lane_s == row_blk).astype(F32)).astype(BF16)

    lane = _lane_iota(t)
    nbp = -(-nb // 8) * 8
    blk = lax.broadcasted_iota(I32, (nbp, t), 0)
    own = qi * (t // bs) + lax.broadcasted_iota(I32, (nbp, t), 1) // bs
    qf = q_ref[...].astype(F32)
    kmean = kmean_ref[0:nbp, :]
    for h in range(2):
        head = (lane >= h * half) & (lane < (h + 1) * half)
        qh = jnp.where(head, qf, 0.0)
        gate = lax.dot_general(kmean, qh, _NT, preferred_element_type=F32, precision=lax.Precision.HIGHEST)
        g = jnp.where(blk < own, gate, -jnp.inf)
        sel = blk == own
        for _ in range(MOBA_TOPK):
            mx = jnp.max(g, 0, keepdims=True)
            idx = jnp.min(jnp.where(g == mx, blk, nbp), 0, keepdims=True)
            hit = blk == idx
            sel = sel | (hit & (blk < own))
            g = jnp.where(hit, -jnp.inf, g)
        selb = jnp.where(sel | (blk >= nb), 0.0, NEG).astype(F32)
        selb = jnp.concatenate([selb, jnp.zeros((LANES - nbp, t), F32)], 0).T
        if h == 0:
            selb = pltpu.roll(selb, half, 1)
        qaug_ref[h] = jnp.where(head, qf, selb).astype(BF16)

    state = (m_ref, l_ref, acc_ref)

    def step(j, first):
        ks = pl.multiple_of(j * t, t)
        v = v_ref[pl.ds(ks, t), :]
        for h in range(2):
            _flash_update(h, qaug_ref[h], kaug_ref[h, pl.ds(ks, t), :], v, bs if first else None, state, first)

    _causal_sweep(qi, step)
    o_ref[...] = jnp.where(lane < half, _flash_out(0, state), _flash_out(1, state)).astype(BF16)


def _moba_attention(q, k, v, b, s):
    bs = MOBA_BLOCK
    nb = s // bs
    assert nb <= LANES // 2, "block gate must fit the spare half of the lanes"
    t = min(ATTN_TILE, s)
    nq = s // t
    hp = q.shape[1] // LANES
    return pl.pallas_call(
        functools.partial(_moba_kernel, t=t, nb=nb),
        grid=(b, hp, nq),
        in_specs=[pl.BlockSpec((t, LANES), lambda bi, h, qi: (bi * nq + qi, h)),
                  pl.BlockSpec((s, LANES), lambda bi, h, qi: (bi, h)),
                  pl.BlockSpec((s, LANES), lambda bi, h, qi: (bi, h))],
        out_specs=pl.BlockSpec((t, LANES), lambda bi, h, qi: (bi * nq + qi, h)),
        out_shape=jax.ShapeDtypeStruct((b * s, hp * LANES), BF16),
        scratch_shapes=[pltpu.VMEM((LANES, LANES), F32), pltpu.VMEM((2, s, LANES), BF16),
                        pltpu.VMEM((2, t, LANES), BF16)] + _flash_state(2, t),
        compiler_params=_cparams("parallel", "parallel", "arbitrary"),
        name="moba_attention",
    )(q, k, v)


def _diff_kernel(q_ref, k_ref, v_ref, lam_ref, g_ref, o_ref, qs_ref, m_ref, l_ref, acc_ref, *, t, lambda_init):
    qi = pl.program_id(2)
    lane = _lane_iota(t)
    qf = q_ref[...].astype(F32)
    qs_ref[0] = jnp.where(lane < HEAD_DIM, qf, 0.0).astype(BF16)
    qs_ref[1] = jnp.where(lane >= HEAD_DIM, qf, 0.0).astype(BF16)
    state = (m_ref, l_ref, acc_ref)

    def step(j, first):
        ks = pl.multiple_of(j * t, t)
        k = k_ref[pl.ds(ks, t), :]
        v = v_ref[pl.ds(ks, t), :]
        for c in range(2):
            _flash_update(c, qs_ref[c], k, v, t if first else None, state, first)

    _causal_sweep(qi, step)
    lp = lam_ref[...]
    lam = (jnp.exp(jnp.sum(lp[0:1] * lp[1:2], -1, keepdims=True))
           - jnp.exp(jnp.sum(lp[2:3] * lp[3:4], -1, keepdims=True)) + lambda_init)
    o = _flash_out(0, state) - lam * _flash_out(1, state)
    o_ref[...] = (_rms(o, g_ref[...], 1e-5) * (1.0 - lambda_init)).astype(BF16)


def _diff_attention(q, k, v, lam_p, g_subln, lambda_init, b, s):
    t = min(ATTN_TILE, s)
    nq = s // t
    nh = q.shape[1] // LANES
    return pl.pallas_call(
        functools.partial(_diff_kernel, t=t, lambda_init=lambda_init),
        grid=(b, nh, nq),
        in_specs=[pl.BlockSpec((t, LANES), lambda bi, h, qi: (bi * nq + qi, h)),
                  pl.BlockSpec((s, LANES), lambda bi, h, qi: (bi, h)),
                  pl.BlockSpec((s, LANES), lambda bi, h, qi: (bi, h)),
                  pl.BlockSpec(lam_p.shape, lambda bi, h, qi: (0, 0)),
                  pl.BlockSpec(g_subln.shape, lambda bi, h, qi: (0, 0))],
        out_specs=pl.BlockSpec((t, LANES), lambda bi, h, qi: (bi * nq + qi, h)),
        out_shape=jax.ShapeDtypeStruct((b * s, nh * LANES), BF16),
        scratch_shapes=[pltpu.VMEM((2, t, LANES), BF16)] + _flash_state(2, t),
        compiler_params=_cparams("parallel", "parallel", "arbitrary"),
        name="diff_attention",
    )(q, k, v, lam_p, g_subln)


def _swa_kernel(sink_ref, q_ref, k_ref, v_ref, o_ref, *, rows):
    ti = pl.program_id(1)
    w = SWA_WINDOW
    nsub = rows // w
    n_q = q_ref.shape[1] // HEAD_DIM
    n_kv = k_ref.shape[1] // HEAD_DIM
    grp = n_q // n_kv
    lane_k = lax.broadcasted_iota(I32, (2 * w, LANES), 1)
    lane_o = _lane_iota(w)
    qrow = lax.broadcasted_iota(I32, (grp * w, 2 * w), 0) % w
    kcol = lax.broadcasted_iota(I32, (grp * w, 2 * w), 1)
    for i in range(nsub):
        sb = ti * nsub + i
        start = pl.multiple_of(jnp.maximum(sb - 1, 0) * w, w)
        kb = k_ref[pl.ds(start, 2 * w), :]
        vb = v_ref[pl.ds(start, 2 * w), :]
        rel = qrow + (sb * w - start) - kcol
        bias = jnp.where((rel >= 0) & (rel < w), 0.0, NEG).astype(F32)
        qf = q_ref[i * w:(i + 1) * w, :].astype(F32)
        v_ones = jnp.concatenate([vb, jnp.ones((2 * w, LANES), BF16)], axis=1)
        head_out = [None] * n_q
        for kvh in range(n_kv):
            km = jnp.where(lane_k // HEAD_DIM == kvh, kb.astype(F32), 0.0).astype(BF16)
            qs, sinks = [], []
            for a in range(grp):
                h = kvh * grp + a
                qp = qf[:, (h // 2) * LANES:(h // 2 + 1) * LANES]
                if h % 2 != kvh:
                    qp = pltpu.roll(qp, HEAD_DIM, 1)
                qs.append(qp.astype(BF16))
                sinks.append(jnp.full((w, LANES), sink_ref[h], F32))
            sink = jnp.concatenate(sinks, 0)
            s = lax.dot_general(jnp.concatenate(qs, 0), km, _NT, preferred_element_type=F32) + bias
            m = jnp.maximum(jnp.max(s, -1, keepdims=True), sink)
            p = jnp.concatenate([jnp.exp(s[:, j * LANES:(j + 1) * LANES] - m) for j in range(2 * w // LANES)], 1)
            pv = jnp.dot(p.astype(BF16), v_ones, preferred_element_type=F32)
            o = pv[:, :LANES] / (pv[:, LANES:] + jnp.exp(sink - m))
            for a in range(grp):
                h = kvh * grp + a
                oh = o[a * w:(a + 1) * w, :]
                if h % 2 != kvh:
                    oh = pltpu.roll(oh, HEAD_DIM, 1)
                head_out[h] = oh
        for hp in range(n_q // 2):
            pair = jnp.where(lane_o < HEAD_DIM, head_out[2 * hp], head_out[2 * hp + 1])
            o_ref[i * w:(i + 1) * w, hp * LANES:(hp + 1) * LANES] = pair.astype(BF16)


def _swa_attention(q, k, v, sinks, b, s):
    rows = min(SWA_ROWS, s)
    nt = s // rows
    return pl.pallas_call(
        functools.partial(_swa_kernel, rows=rows),
        grid=(b, nt),
        in_specs=[pl.BlockSpec(memory_space=pltpu.SMEM),
                  pl.BlockSpec((rows, q.shape[1]), lambda bi, ti: (bi * nt + ti, 0)),
                  pl.BlockSpec((s, k.shape[1]), lambda bi, ti: (bi, 0)),
                  pl.BlockSpec((s, v.shape[1]), lambda bi, ti: (bi, 0))],
        out_specs=pl.BlockSpec((rows, q.shape[1]), lambda bi, ti: (bi * nt + ti, 0)),
        out_shape=jax.ShapeDtypeStruct(q.shape, BF16),
        compiler_params=_cparams("parallel", "arbitrary"),
        name="swa_attention",
    )(sinks, q, k, v)


def _layer_norm(y, g, b):
    mu = jnp.mean(y, -1, keepdims=True)
    yc = y - mu
    var = jnp.mean(yc * yc, -1, keepdims=True)
    return yc * lax.rsqrt(var + 1e-5) * g + b


PIECES = 8


def _tm_load(ref, r0, nrows):
    return jnp.concatenate([ref[pl.ds(r0 * PIECES + p, nrows, stride=PIECES), :] for p in range(PIECES)], axis=1)


def _tm_store(ref, r0, val):
    for p in range(PIECES):
        ref[pl.ds(r0 * PIECES + p, val.shape[0], stride=PIECES), :] = val[:, p * LANES:(p + 1) * LANES]


def _out_ln_kernel(x_ref, a_ref, b_ref, w_ref, g_ref, beta_ref, otm_ref, *, alpha):
    ka = a_ref.shape[1]
    mix = (jnp.dot(a_ref[...], w_ref[:ka, :], preferred_element_type=F32)
           + jnp.dot(b_ref[...], w_ref[ka:, :], preferred_element_type=F32))
    _tm_store(otm_ref, 0, _layer_norm(alpha * x_ref[...] + mix, g_ref[...], beta_ref[...]))


def _out_ln(xf, a, b, w, g, beta, alpha):
    n, d = xf.shape
    assert d == PIECES * LANES
    tm = min(PROJ_ROWS, n)
    row = lambda width: pl.BlockSpec((tm, width), lambda i: (i, 0))
    full = lambda arr: pl.BlockSpec(arr.shape, lambda i: (0, 0))
    return pl.pallas_call(
        functools.partial(_out_ln_kernel, alpha=alpha),
        grid=(n // tm,),
        in_specs=[row(d), row(a.shape[1]), row(b.shape[1]), full(w), full(g), full(beta)],
        out_specs=pl.BlockSpec((tm * PIECES, LANES), lambda i: (i, 0)),
        out_shape=jax.ShapeDtypeStruct((n * PIECES, LANES), F32),
        compiler_params=_cparams("parallel"),
        name="out_proj_ln",
    )(xf, a, b, w, g, beta)


def _row_argmax(v, row, nrow):
    mx = jnp.max(v, 0, keepdims=True)
    idx = jnp.min(jnp.where(v == mx, row, nrow), 0, keepdims=True)
    return mx, idx


def _router_kernel(x_ref, w_ref, bias_ref, eidx_ref, wts_ref):
    tm = x_ref.shape[0] // PIECES
    logits = jnp.dot(_tm_load(x_ref, 0, tm), w_ref[...], preferred_element_type=F32,
                     precision=lax.Precision.HIGHEST)
    scores = jax.nn.sigmoid(logits.T[:N_EXPERTS, :])
    biased = scores + bias_ref[...]
    per = N_EXPERTS // N_GROUPS
    row_g = lax.broadcasted_iota(I32, (per, tm), 0)
    gsc = []
    for g in range(N_GROUPS):
        v = biased[g * per:(g + 1) * per, :]
        m1, i1 = _row_argmax(v, row_g, per)
        m2 = jnp.max(jnp.where(row_g == i1, -jnp.inf, v), 0, keepdims=True)
        gsc.append(m1 + m2)
    gscore = jnp.concatenate(gsc, 0)
    row_n = lax.broadcasted_iota(I32, (N_GROUPS, tm), 0)
    gok = jnp.zeros((N_GROUPS, tm), F32)
    for _ in range(TOPK_GROUPS):
        _, ig = _row_argmax(gscore, row_n, N_GROUPS)
        hit = row_n == ig
        gok = jnp.where(hit, 1.0, gok)
        gscore = jnp.where(hit, -jnp.inf, gscore)
    cand = jnp.concatenate([jnp.where(gok[g:g + 1, :] > 0.0, biased[g * per:(g + 1) * per, :], -jnp.inf)
                            for g in range(N_GROUPS)], 0)
    row_e = lax.broadcasted_iota(I32, (N_EXPERTS, tm), 0)
    ids, ws = [], []
    for _ in range(TOP_K):
        _, ie = _row_argmax(cand, row_e, N_EXPERTS)
        hit = row_e == ie
        ids.append(ie)
        ws.append(jnp.sum(jnp.where(hit, scores, 0.0), 0, keepdims=True))
        cand = jnp.where(hit, -jnp.inf, cand)
    w = jnp.concatenate(ws, 0)
    eidx_ref[...] = jnp.concatenate(ids, 0)
    wts_ref[...] = w / jnp.sum(w, 0, keepdims=True) * ROUTED_SCALE


def _router(x_tm, w_router, e_bias):
    n = x_tm.shape[0] // PIECES
    tm = min(ROUTER_ROWS, n)
    wpad = jnp.pad(w_router, ((0, 0), (0, LANES - N_EXPERTS)))
    bcol = e_bias.astype(F32).reshape(N_EXPERTS, 1)
    out = pl.BlockSpec((TOP_K, tm), lambda i: (0, i))
    return pl.pallas_call(
        _router_kernel,
        grid=(n // tm,),
        in_specs=[pl.BlockSpec((tm * PIECES, LANES), lambda i: (i, 0)), pl.BlockSpec(wpad.shape, lambda i: (0, 0)),
                  pl.BlockSpec(bcol.shape, lambda i: (0, 0))],
        out_specs=[out, out],
        out_shape=[jax.ShapeDtypeStruct((TOP_K, n), I32), jax.ShapeDtypeStruct((TOP_K, n), F32)],
        compiler_params=_cparams("parallel"),
        name="router",
    )(x_tm, wpad, bcol)


def _dispatch(eidx_t, wts_t, chunk, tail):
    n = eidx_t.shape[1]
    nc = n // chunk
    a = chunk * TOP_K
    ec = eidx_t.reshape(TOP_K, nc, chunk).transpose(1, 0, 2).reshape(nc, a)
    wc = wts_t.reshape(TOP_K, nc, chunk).transpose(1, 0, 2).reshape(nc, a)
    ex = jnp.arange(N_EXPERTS, dtype=I32)
    counts = jnp.sum((ec[:, :, None] == ex[None, None, :]).astype(I32), axis=1)
    ends = jnp.cumsum(counts, axis=1).astype(I32)
    starts = jnp.concatenate([jnp.zeros((nc, 1), I32), ends], axis=1)
    mult = 1 << int(chunk - 1).bit_length()
    tok_ids = jnp.broadcast_to(jnp.arange(a, dtype=I32)[None, :] % chunk, (nc, a))
    keys, gates = lax.sort((ec * mult + tok_ids, wc), dimension=1, num_keys=1)
    rows = jnp.pad((keys % mult) * PIECES, ((0, 0), (0, tail)))
    gates = jnp.pad(gates, ((0, 0), (0, tail)))
    return starts.reshape(-1), rows.reshape(-1), gates.reshape(-1)


def _swiglu(xb, wg, wu):
    hg = jnp.dot(xb, wg, preferred_element_type=F32)
    hu = jnp.dot(xb, wu, preferred_element_type=F32)
    return hg * jax.nn.sigmoid(hg) * hu


def _moe_kernel(start_ref, tok_ref, gate_ref, x_ref, wg_ref, wu_ref, wd_ref, sg_ref, su_ref, sd_ref,
                g_ref, beta_ref, o_ref, acc_ref, xga_ref, xgb_ref, oga_ref, ogb_ref,
                *, chunk, rows, head, alpha):
    c = pl.program_id(0)
    pair = pl.program_id(1)
    n_e = 2 * pl.num_programs(1)
    group = 8
    sub = min(512, chunk)

    @pl.when((c == 0) & (pair == 0))
    def _():
        for ref in (xga_ref, xgb_ref):
            ref[...] = jnp.zeros_like(ref)

    @pl.when(pair == 0)
    def _():
        for s0 in range(0, chunk, sub):
            xb = _tm_load(x_ref, s0, sub).astype(BF16)
            h = _swiglu(xb, sg_ref[...], su_ref[...]).astype(BF16)
            _tm_store(acc_ref, s0, jnp.dot(h, sd_ref[...], preferred_element_type=F32))
        acc_ref[chunk * PIECES:(chunk + 1) * PIECES, :] = jnp.zeros((PIECES, LANES), F32)

    def run(ei):
        return start_ref[c * (n_e + 1) + ei], start_ref[c * (n_e + 1) + ei + 1]

    def at_row(first):
        return pl.ds(pl.multiple_of(first, PIECES), PIECES)

    def gather_group(xg, base, r0, _):
        for i in range(group):
            xg[at_row((r0 + i) * PIECES), :] = x_ref[at_row(tok_ref[base + r0 + i]), :]

    def scatter_group(og, base, nrow, r0, mask):
        firsts, vals = [], []
        for i in range(group):
            idx = base + r0 + i
            first = tok_ref[idx]
            if mask:
                first = jnp.where(r0 + i < nrow, first, chunk * PIECES)
            firsts.append(first)
            vals.append(acc_ref[at_row(first), :] + gate_ref[idx] * og[at_row((r0 + i) * PIECES), :])
        for i in range(group):
            acc_ref[at_row(firsts[i]), :] = vals[i]

    def sweep(fn, r_lo, r_hi):
        n = jnp.maximum(r_hi - r_lo, 0)
        nfull = n // group

        def body(gi, carry):
            fn(pl.multiple_of(r_lo + gi * group, group), False)
            return carry

        lax.fori_loop(0, nfull, body, 0)
        pl.when(n % group != 0)(lambda: fn(pl.multiple_of(r_lo + nfull * group, group), True))

    def expert(k, xg, og):
        xb = _tm_load(xg, 0, rows).astype(BF16)
        h = _swiglu(xb, wg_ref[k], wu_ref[k]).astype(BF16)
        _tm_store(og, 0, jnp.dot(h, wd_ref[k], preferred_element_type=F32))

    def step(k, xg_cur, xg_next, og_cur, og_prev):
        e = 2 * pair + k
        lo, hi = run(e)
        n_cur = jnp.minimum(hi - lo, rows)
        lo_next, _ = run(jnp.minimum(e + 1, n_e - 1))
        lo_prev, hi_prev = run(jnp.maximum(e - 1, 0))
        n_prev = jnp.where(e > 0, jnp.minimum(hi_prev - lo_prev, rows), 0)
        sweep(lambda r0, p: gather_group(xg_cur, lo, r0, p), jnp.where(e == 0, 0, head), n_cur)
        for r0 in range(0, head, group):
            gather_group(xg_next, lo_next, r0, True)
        expert(k, xg_cur, og_cur)
        for r0 in range(0, head, group):
            scatter_group(og_prev, lo_prev, n_prev, r0, True)
        sweep(lambda r0, p: scatter_group(og_cur, lo, n_cur, r0, p), head, n_cur)

        def later_block(bi, carry):
            base = lo + bi * rows
            nrow = jnp.minimum(hi - base, rows)
            sweep(lambda r0, p: gather_group(xg_cur, base, r0, p), 0, nrow)
            expert(k, xg_cur, og_prev)
            sweep(lambda r0, p: scatter_group(og_prev, base, nrow, r0, p), 0, nrow)
            return carry

        lax.fori_loop(1, (hi - lo + rows - 1) // rows, later_block, 0)
        if k == 1:
            pl.when(e == n_e - 1)(
                lambda: sweep(lambda r0, p: scatter_group(og_cur, lo, n_cur, r0, p), 0, jnp.minimum(n_cur, head)))

    step(0, xga_ref, xgb_ref, oga_ref, ogb_ref)
    step(1, xgb_ref, xga_ref, ogb_ref, oga_ref)

    @pl.when(pair == pl.num_programs(1) - 1)
    def _():
        for s0 in range(0, chunk, sub):
            z = alpha * _tm_load(x_ref, s0, sub) + _tm_load(acc_ref, s0, sub)
            o_ref[s0:s0 + sub, :] = _layer_norm(z, g_ref[...], beta_ref[...])


def _moe_ln(x_tm, eidx, wts, layer, wg, wu, wd, sg, su, sd, g, beta, alpha):
    n_layers, n_e = wg.shape[:2]
    wg, wu, wd = (w.reshape((n_layers * n_e,) + w.shape[2:]) for w in (wg, wu, wd))
    first = layer * (n_e // 2)
    n = x_tm.shape[0] // PIECES
    d = PIECES * LANES
    chunk = min(MOE_CHUNK, n)
    rows = MOE_ROWS
    tail = -(-rows // 1024) * 1024
    starts, tok, gate = _dispatch(eidx, wts, chunk, tail)
    nc = n // chunk
    de = wg.shape[2]
    assert n_e % 2 == 0
    full = lambda arr: pl.BlockSpec(arr.shape, lambda c, e, s_r: (0,) * arr.ndim)
    lists = pl.BlockSpec((chunk * TOP_K + tail,), lambda c, e, s_r: (c,), memory_space=pltpu.SMEM)
    grid_spec = pltpu.PrefetchScalarGridSpec(
        num_scalar_prefetch=1,
        grid=(nc, n_e // 2),
        in_specs=[
            lists, lists,
            pl.BlockSpec((chunk * PIECES, LANES), lambda c, e, s_r: (c, 0)),
            pl.BlockSpec((2, d, de), lambda c, e, s_r: (first + e, 0, 0)),
            pl.BlockSpec((2, d, de), lambda c, e, s_r: (first + e, 0, 0)),
            pl.BlockSpec((2, de, d), lambda c, e, s_r: (first + e, 0, 0)),
            full(sg), full(su), full(sd), full(g), full(beta),
        ],
        out_specs=pl.BlockSpec((chunk, d), lambda c, e, s_r: (c, 0)),
        scratch_shapes=[pltpu.VMEM(((chunk + 1) * PIECES, LANES), F32)]
        + [pltpu.VMEM((rows * PIECES, LANES), F32) for _ in range(4)],
    )
    return pl.pallas_call(
        functools.partial(_moe_kernel, chunk=chunk, rows=rows, head=min(MOE_HEAD, rows), alpha=alpha),
        grid_spec=grid_spec,
        out_shape=jax.ShapeDtypeStruct((n, d), F32),
        compiler_params=_cparams("arbitrary", "arbitrary"),
        name="moe_experts",
    )(starts, tok, gate, x_tm, wg, wu, wd, sg, su, sd, g, beta)


def _rope_tables(positions):
    pos = positions.reshape(-1).astype(F32)[:, None]
    n = pos.shape[0]

    def cs(dim):
        inv = ROPE_THETA ** (-jnp.arange(0, dim, 2, dtype=F32) / dim)
        ang = pos * inv
        return jnp.cos(ang), jnp.sin(ang)

    c, s = cs(HEAD_DIM)
    c64 = jnp.concatenate([c, c, c, c], -1)
    s64 = jnp.concatenate([-s, s, -s, s], -1)
    c, s = cs(MLA_ROPE)
    one = jnp.ones((n, MLA_NOPE), F32)
    pad = LANES - MLA_NOPE - MLA_ROPE
    cm = jnp.concatenate([one, c, c, jnp.ones((n, pad), F32)], -1)
    sm = jnp.concatenate([0 * one, -s, s, jnp.zeros((n, pad), F32)], -1)
    return (c64, s64), (cm, sm)


def _even_mixer(xf, b, s, tabs, w_in, g_qa, w_qb, g_kva, w_kvb):
    d = xf.shape[1]
    tab64, tabm = tabs
    na = 3 * 512
    kpe_lo = na + 256 + 128
    wpad = jnp.concatenate([w_in[:, :kpe_lo], jnp.zeros((d, MLA_NOPE), F32), w_in[:, kpe_lo:],
                            jnp.zeros((d, LANES - MLA_NOPE - MLA_ROPE), F32)], 1).astype(BF16)
    nh = w_qb.shape[1] // (MLA_NOPE + MLA_ROPE)
    wqb = jnp.pad(w_qb.reshape(-1, nh, MLA_NOPE + MLA_ROPE), ((0, 0), (0, 0), (0, LANES - MLA_NOPE - MLA_ROPE)))
    wqb = wqb.reshape(-1, nh * LANES).astype(BF16)
    wkv3 = w_kvb.reshape(-1, nh, MLA_NOPE + MLA_V)
    wkk = jnp.pad(wkv3[:, :, :MLA_NOPE], ((0, 0), (0, 0), (0, LANES - MLA_NOPE))).reshape(-1, nh * LANES).astype(BF16)
    wkv = wkv3[:, :, MLA_NOPE:].reshape(-1, nh * MLA_V).astype(BF16)
    segs = [(0, 512, True, QK_SCALE), (512, 1024, True, 1.0), (1024, 1536, False, 1.0)]
    qa, ka, va, qm, km, vm = _projection(
        xf, wpad, segs, tab64,
        mla_args=(tabm[0], tabm[1], g_qa.reshape(1, -1), wqb, g_kva.reshape(1, -1), wkk, wkv),
        mla_out_widths=(nh * LANES, nh * LANES, nh * MLA_V))
    o_a = _moba_attention(qa, ka, va, b, s)
    o_b = _mla_attention(qm, km, vm, b, s)
    return o_a, o_b


def _odd_mixer(xf, b, s, tabs, w_in, sinks, lam_p, g_subln, lambda_init):
    segs = [(0, 512, True, QK_SCALE), (512, 640, True, 1.0), (640, 768, False, 1.0),
            (768, 1280, True, QK_SCALE), (1280, 1792, True, 1.0), (1792, 2304, False, 1.0)]
    qc, kc, vc, qd, kd, vd = _projection(xf, w_in.astype(BF16), segs, tabs[0])
    o_c = _swa_attention(qc, kc, vc, sinks.astype(F32), b, s)
    o_d = _diff_attention(qd, kd, vd, lam_p.astype(F32), g_subln.reshape(1, -1), lambda_init, b, s)
    return o_c, o_d


def kernel(x, positions, ev_w_in, ev_w_out, mla_g_qa, mla_w_qb, mla_g_kva, mla_w_kvb, od_w_in, od_w_out, swa_sinks, diff_lambda, diff_g_subln, ln_g, ln_b, moe_w_router, moe_bias, moe_w_gate, moe_w_up, moe_w_down, sh_w_gate, sh_w_up, sh_w_down):
    b, s, d = x.shape
    depth = ln_g.shape[0]
    alpha = float((2 * depth) ** 0.25)
    xf = x.reshape(b * s, d)
    tabs = _rope_tables(positions)
    wg_all, wu_all, wd_all = (w.astype(BF16) for w in (moe_w_gate, moe_w_up, moe_w_down))
    for l in range(depth):
        i = l // 2
        if l % 2 == 0:
            o1, o2 = _even_mixer(xf, b, s, tabs, ev_w_in[i], mla_g_qa[i], mla_w_qb[i], mla_g_kva[i], mla_w_kvb[i])
            w_out = ev_w_out[i]
        else:
            lambda_init = 0.8 - 0.6 * float(np.exp(-0.3 * l))
            o1, o2 = _odd_mixer(xf, b, s, tabs, od_w_in[i], swa_sinks[i], diff_lambda[i], diff_g_subln[i], lambda_init)
            w_out = od_w_out[i]
        x_tm = _out_ln(xf, o1, o2, w_out.astype(BF16), ln_g[l, 0].reshape(1, d), ln_b[l, 0].reshape(1, d), alpha)
        eidx, wts = _router(x_tm, moe_w_router[l], moe_bias[l])
        xf = _moe_ln(x_tm, eidx, wts, l, wg_all, wu_all, wd_all, sh_w_gate[l].astype(BF16), sh_w_up[l].astype(BF16),
                     sh_w_down[l].astype(BF16), ln_g[l, 1].reshape(1, d), ln_b[l, 1].reshape(1, d), alpha)
    return xf.reshape(b, s, d)
```

```python
import functools

import numpy as np
import jax
import jax.numpy as jnp
from jax import lax
from jax.experimental import pallas as pl
from jax.experimental.pallas import tpu as pltpu

F32 = jnp.float32
BF16 = jnp.bfloat16
I32 = jnp.int32

LANES = 128
NEG = -1e30

HEAD_DIM = 64
MOBA_BLOCK = 256
MOBA_TOPK = 3
MLA_NOPE = 64
MLA_ROPE = 32
MLA_V = 64
SWA_WINDOW = 128
N_EXPERTS = 64
N_GROUPS = 8
TOPK_GROUPS = 4
TOP_K = 8
ROUTED_SCALE = 2.5
ROPE_THETA = 10000.0
QK_SCALE = HEAD_DIM ** -0.5
MLA_SCALE = (MLA_NOPE + MLA_ROPE) ** -0.5

PROJ_ROWS = 512
ATTN_TILE = 1024
SWA_ROWS = 512
ROUTER_ROWS = 512
MOE_CHUNK = 2048
MOE_ROWS = 320
MOE_HEAD = 320
VMEM_LIMIT = 56 * 1024 * 1024

_NT = (((1,), (1,)), ((), ()))


def _cparams(*sem):
    return pltpu.CompilerParams(dimension_semantics=sem, vmem_limit_bytes=VMEM_LIMIT)


def _lane_iota(rows):
    return lax.broadcasted_iota(I32, (rows, LANES), 1)


def _rope(x, c, s, first_half, half):
    rot = jnp.where(first_half, pltpu.roll(x, LANES - half, 1), pltpu.roll(x, half, 1))
    return x * c + rot * s


def _rms(x, g, eps):
    return x * lax.rsqrt(jnp.mean(x * x, -1, keepdims=True) + eps) * g


def _proj_kernel(segs, mla, x_ref, w_ref, c64_ref, s64_ref, *refs):
    xb = x_ref[...].astype(BF16)
    rows = xb.shape[0]
    lane = _lane_iota(rows)
    fh64 = (lane % HEAD_DIM) < HEAD_DIM // 2
    c64 = c64_ref[...]
    s64 = s64_ref[...]

    def seg(lo, hi):
        return jnp.dot(xb, w_ref[:, lo:hi], preferred_element_type=F32)

    if mla:
        cm_ref, sm_ref, gq_ref, wqb_ref, gkv_ref, wkk_ref, wkv_ref = refs[:7]
        refs = refs[7:]
    for (lo, hi, rope, scale), o_ref in zip(segs, refs):
        h = seg(lo, hi)
        if rope:
            for g in range((hi - lo) // LANES):
                sl = slice(g * LANES, (g + 1) * LANES)
                o_ref[:, sl] = (_rope(h[:, sl], c64, s64, fh64, HEAD_DIM // 2) * scale).astype(BF16)
        else:
            o_ref[...] = h.astype(BF16)
    if mla:
        qm_ref, km_ref, vm_ref = refs[len(segs):]
        cm = cm_ref[...]
        sm = sm_ref[...]
        fhm = (lane >= MLA_NOPE) & (lane < MLA_NOPE + MLA_ROPE // 2)
        lo = segs[-1][1]
        qn = _rms(seg(lo, lo + 256), gq_ref[...], 1e-6).astype(BF16)
        qm = jnp.dot(qn, wqb_ref[...], preferred_element_type=F32)
        kvn = _rms(seg(lo + 256, lo + 384), gkv_ref[...], 1e-6).astype(BF16)
        kk = jnp.dot(kvn, wkk_ref[...], preferred_element_type=F32)
        kpe = _rope(seg(lo + 384, lo + 512), cm, sm, fhm, MLA_ROPE // 2)
        for g in range(qm.shape[1] // LANES):
            sl = slice(g * LANES, (g + 1) * LANES)
            qm_ref[:, sl] = (_rope(qm[:, sl], cm, sm, fhm, MLA_ROPE // 2) * MLA_SCALE).astype(BF16)
            km_ref[:, sl] = (kk[:, sl] + kpe).astype(BF16)
        vm_ref[...] = jnp.dot(kvn, wkv_ref[...], preferred_element_type=F32).astype(BF16)


def _projection(xf, w, segs, tabs, mla_args=None, mla_out_widths=()):
    n, d = xf.shape
    tm = min(PROJ_ROWS, n)
    row = lambda width: pl.BlockSpec((tm, width), lambda i: (i, 0))
    full = lambda a: pl.BlockSpec(a.shape, lambda i: (0,) * a.ndim)
    ins = [xf, w, tabs[0], tabs[1]]
    in_specs = [row(d), full(w), row(LANES), row(LANES)]
    if mla_args is not None:
        cm, sm, gq, wqb, gkv, wkk, wkv = mla_args
        ins += [cm, sm, gq, wqb, gkv, wkk, wkv]
        in_specs += [row(LANES), row(LANES), full(gq), full(wqb), full(gkv), full(wkk), full(wkv)]
    widths = [hi - lo for lo, hi, _, _ in segs] + list(mla_out_widths)
    return pl.pallas_call(
        functools.partial(_proj_kernel, tuple(segs), mla_args is not None),
        grid=(n // tm,),
        in_specs=in_specs,
        out_specs=[row(wd) for wd in widths],
        out_shape=[jax.ShapeDtypeStruct((n, wd), BF16) for wd in widths],
        compiler_params=_cparams("parallel"),
        name="projection",
    )(*ins)


def _diag_strip(sj, j, block, tri):
    t = sj.shape[0]
    r0 = j * LANES
    lo = (r0 // block) * block
    parts = []
    if lo > 0:
        parts.append(sj[:lo])
    if r0 > lo:
        parts.append(jnp.full((r0 - lo, LANES), NEG, F32))
    parts.append(sj[r0:r0 + LANES] + tri)
    if r0 + LANES < t:
        parts.append(sj[r0 + LANES:])
    return jnp.concatenate(parts, 0)


def _flash_update(c, q, k, v, diag_block, state, first):
    m_ref, l_ref, acc_ref = state
    s = lax.dot_general(q, k, _NT, preferred_element_type=F32)
    t, tk = s.shape
    tiles = [s[:, j * LANES:(j + 1) * LANES] for j in range(tk // LANES)]
    if diag_block is None:
        mx = jnp.max(s, -1, keepdims=True)
    else:
        tri = jnp.where(lax.broadcasted_iota(I32, (LANES, LANES), 1) > lax.broadcasted_iota(I32, (LANES, LANES), 0),
                        NEG, 0.0).astype(F32)
        tiles = [_diag_strip(sj, j, diag_block, tri) for j, sj in enumerate(tiles)]
        pm = tiles[0]
        for sj in tiles[1:]:
            pm = jnp.maximum(pm, sj)
        mx = jnp.max(pm, -1, keepdims=True)
    if first:
        m_new = jnp.broadcast_to(mx, (t, LANES))
    else:
        m_prev = m_ref[c]
        m_new = jnp.maximum(m_prev, mx)
        alpha = jnp.exp(m_prev - m_new)
    ps, psum = [], None
    for sj in tiles:
        pj = jnp.exp(sj - m_new)
        ps.append(pj.astype(BF16))
        psum = pj if psum is None else psum + pj
    pv = jnp.dot(jnp.concatenate(ps, axis=1), v, preferred_element_type=F32)
    if first:
        l_ref[c] = psum
        acc_ref[c] = pv
    else:
        l_ref[c] = alpha * l_ref[c] + psum
        acc_ref[c] = alpha * acc_ref[c] + pv
    m_ref[c] = m_new


def _flash_out(c, state):
    _, l_ref, acc_ref = state
    return acc_ref[c] / jnp.sum(l_ref[c], -1, keepdims=True)


def _flash_state(chains, t):
    return [pltpu.VMEM((chains, t, LANES), F32) for _ in range(3)]


def _causal_sweep(qi, step):
    step(qi, True)

    def body(j, carry):
        step(j, False)
        return carry

    lax.fori_loop(0, qi, body, 0)


def _mla_kernel(q_ref, k_ref, v_ref, o_ref, m_ref, l_ref, acc_ref, *, t):
    qi = pl.program_id(2)
    state = (m_ref, l_ref, acc_ref)

    def step(j, first):
        ks = pl.multiple_of(j * t, t)
        v = v_ref[pl.ds(ks, t), :]
        for h in range(2):
            hs = slice(h * LANES, (h + 1) * LANES)
            _flash_update(h, q_ref[:, hs], k_ref[pl.ds(ks, t), hs], v, t if first else None, state, first)

    _causal_sweep(qi, step)
    lane = _lane_iota(t)
    o_ref[...] = jnp.where(lane < MLA_V, _flash_out(0, state), _flash_out(1, state)).astype(BF16)


def _mla_attention(qm, km, vm, b, s):
    t = min(ATTN_TILE, s)
    nq = s // t
    hp = qm.shape[1] // (2 * LANES)
    return pl.pallas_call(
        functools.partial(_mla_kernel, t=t),
        grid=(b, hp, nq),
        in_specs=[pl.BlockSpec((t, 2 * LANES), lambda bi, h, qi: (bi * nq + qi, h)),
                  pl.BlockSpec((s, 2 * LANES), lambda bi, h, qi: (bi, h)),
                  pl.BlockSpec((s, LANES), lambda bi, h, qi: (bi, h))],
        out_specs=pl.BlockSpec((t, LANES), lambda bi, h, qi: (bi * nq + qi, h)),
        out_shape=jax.ShapeDtypeStruct((b * s, hp * LANES), BF16),
        scratch_shapes=_flash_state(2, t),
        compiler_params=_cparams("parallel", "parallel", "arbitrary"),
        name="mla_attention",
    )(qm, km, vm)


def _moba_kernel(q_ref, k_ref, v_ref, o_ref, kmean_ref, kaug_ref, qaug_ref, m_ref, l_ref, acc_ref, *, t, nb):
    qi = pl.program_id(2)
    bs = MOBA_BLOCK
    s_len = k_ref.shape[0]
    half = LANES // 2

    @pl.when(qi == 0)
    def _():
        kmean_ref[...] = jnp.zeros_like(kmean_ref)
        for n in range(nb):
            blk = k_ref[n * bs:(n + 1) * bs, :].astype(F32)
            kmean_ref[n:n + 1, :] = jnp.sum(blk, 0, keepdims=True) * (1.0 / bs)
        lane_s = _lane_iota(s_len)
        row_blk = lax.broadcasted_iota(I32, (s_len, LANES), 0) // bs
        kf = k_ref[...].astype(F32)
        kaug_ref[0] = jnp.where(lane_s < half, kf, (lane_s - half == row_blk).astype(F32)).astype(BF16)
        kaug_ref[1] = jnp.where(lane_s >= half, kf, (lane_s == row_blk).astype(F32)).astype(BF16)

    lane = _lane_iota(t)
    nbp = -(-nb // 8) * 8
    blk = lax.broadcasted_iota(I32, (nbp, t), 0)
    own = qi * (t // bs) + lax.broadcasted_iota(I32, (nbp, t), 1) // bs
    qf = q_ref[...].astype(F32)
    kmean = kmean_ref[0:nbp, :]
    for h in range(2):
        head = (lane >= h * half) & (lane < (h + 1) * half)
        qh = jnp.where(head, qf, 0.0)
        gate = lax.dot_general(kmean, qh, _NT, preferred_element_type=F32, precision=lax.Precision.HIGHEST)
        g = jnp.where(blk < own, gate, -jnp.inf)
        sel = blk == own
        for _ in range(MOBA_TOPK):
            mx = jnp.max(g, 0, keepdims=True)
            idx = jnp.min(jnp.where(g == mx, blk, nbp), 0, keepdims=True)
            hit = blk == idx
            sel = sel | (hit & (blk < own))
            g = jnp.where(hit, -jnp.inf, g)
        selb = jnp.where(sel | (blk >= nb), 0.0, NEG).astype(F32)
        selb = jnp.concatenate([selb, jnp.zeros((LANES - nbp, t), F32)], 0).T
        if h == 0:
            selb = pltpu.roll(selb, half, 1)
        qaug_ref[h] = jnp.where(head, qf, selb).astype(BF16)

    state = (m_ref, l_ref, acc_ref)

    def step(j, first):
        ks = pl.multiple_of(j * t, t)
        v = v_ref[pl.ds(ks, t), :]
        for h in range(2):
            _flash_update(h, qaug_ref[h], kaug_ref[h, pl.ds(ks, t), :], v, bs if first else None, state, first)

    _causal_sweep(qi, step)
    o_ref[...] = jnp.where(lane < half, _flash_out(0, state), _flash_out(1, state)).astype(BF16)


def _moba_attention(q, k, v, b, s):
    bs = MOBA_BLOCK
    nb = s // bs
    assert nb <= LANES // 2, "block gate must fit the spare half of the lanes"
    t = min(ATTN_TILE, s)
    nq = s // t
    hp = q.shape[1] // LANES
    return pl.pallas_call(
        functools.partial(_moba_kernel, t=t, nb=nb),
        grid=(b, hp, nq),
        in_specs=[pl.BlockSpec((t, LANES), lambda bi, h, qi: (bi * nq + qi, h)),
                  pl.BlockSpec((s, LANES), lambda bi, h, qi: (bi, h)),
                  pl.BlockSpec((s, LANES), lambda bi, h, qi: (bi, h))],
        out_specs=pl.BlockSpec((t, LANES), lambda bi, h, qi: (bi * nq + qi, h)),
        out_shape=jax.ShapeDtypeStruct((b * s, hp * LANES), BF16),
        scratch_shapes=[pltpu.VMEM((LANES, LANES), F32), pltpu.VMEM((2, s, LANES), BF16),
                        pltpu.VMEM((2, t, LANES), BF16)] + _flash_state(2, t),
        compiler_params=_cparams("parallel", "parallel", "arbitrary"),
        name="moba_attention",
    )(q, k, v)


def _diff_kernel(q_ref, k_ref, v_ref, lam_ref, g_ref, o_ref, qs_ref, m_ref, l_ref, acc_ref, *, t, lambda_init):
    qi = pl.program_id(2)
    lane = _lane_iota(t)
    qf = q_ref[...].astype(F32)
    qs_ref[0] = jnp.where(lane < HEAD_DIM, qf, 0.0).astype(BF16)
    qs_ref[1] = jnp.where(lane >= HEAD_DIM, qf, 0.0).astype(BF16)
    state = (m_ref, l_ref, acc_ref)

    def step(j, first):
        ks = pl.multiple_of(j * t, t)
        k = k_ref[pl.ds(ks, t), :]
        v = v_ref[pl.ds(ks, t), :]
        for c in range(2):
            _flash_update(c, qs_ref[c], k, v, t if first else None, state, first)

    _causal_sweep(qi, step)
    lp = lam_ref[...]
    lam = (jnp.exp(jnp.sum(lp[0:1] * lp[1:2], -1, keepdims=True))
           - jnp.exp(jnp.sum(lp[2:3] * lp[3:4], -1, keepdims=True)) + lambda_init)
    o = _flash_out(0, state) - lam * _flash_out(1, state)
    o_ref[...] = (_rms(o, g_ref[...], 1e-5) * (1.0 - lambda_init)).astype(BF16)


def _diff_attention(q, k, v, lam_p, g_subln, lambda_init, b, s):
    t = min(ATTN_TILE, s)
    nq = s // t
    nh = q.shape[1] // LANES
    return pl.pallas_call(
        functools.partial(_diff_kernel, t=t, lambda_init=lambda_init),
        grid=(b, nh, nq),
        in_specs=[pl.BlockSpec((t, LANES), lambda bi, h, qi: (bi * nq + qi, h)),
                  pl.BlockSpec((s, LANES), lambda bi, h, qi: (bi, h)),
                  pl.BlockSpec((s, LANES), lambda bi, h, qi: (bi, h)),
                  pl.BlockSpec(lam_p.shape, lambda bi, h, qi: (0, 0)),
                  pl.BlockSpec(g_subln.shape, lambda bi, h, qi: (0, 0))],
        out_specs=pl.BlockSpec((t, LANES), lambda bi, h, qi: (bi * nq + qi, h)),
        out_shape=jax.ShapeDtypeStruct((b * s, nh * LANES), BF16),
        scratch_shapes=[pltpu.VMEM((2, t, LANES), BF16)] + _flash_state(2, t),
        compiler_params=_cparams("parallel", "parallel", "arbitrary"),
        name="diff_attention",
    )(q, k, v, lam_p, g_subln)


def _swa_kernel(sink_ref, q_ref, k_ref, v_ref, o_ref, *, rows):
    ti = pl.program_id(1)
    w = SWA_WINDOW
    nsub = rows // w
    n_q = q_ref.shape[1] // HEAD_DIM
    n_kv = k_ref.shape[1] // HEAD_DIM
    grp = n_q // n_kv
    lane_k = lax.broadcasted_iota(I32, (2 * w, LANES), 1)
    lane_o = _lane_iota(w)
    qrow = lax.broadcasted_iota(I32, (grp * w, 2 * w), 0) % w
    kcol = lax.broadcasted_iota(I32, (grp * w, 2 * w), 1)
    for i in range(nsub):
        sb = ti * nsub + i
        start = pl.multiple_of(jnp.maximum(sb - 1, 0) * w, w)
        kb = k_ref[pl.ds(start, 2 * w), :]
        vb = v_ref[pl.ds(start, 2 * w), :]
        rel = qrow + (sb * w - start) - kcol
        bias = jnp.where((rel >= 0) & (rel < w), 0.0, NEG).astype(F32)
        qf = q_ref[i * w:(i + 1) * w, :].astype(F32)
        v_ones = jnp.concatenate([vb, jnp.ones((2 * w, LANES), BF16)], axis=1)
        head_out = [None] * n_q
        for kvh in range(n_kv):
            km = jnp.where(lane_k // HEAD_DIM == kvh, kb.astype(F32), 0.0).astype(BF16)
            qs, sinks = [], []
            for a in range(grp):
                h = kvh * grp + a
                qp = qf[:, (h // 2) * LANES:(h // 2 + 1) * LANES]
                if h % 2 != kvh:
                    qp = pltpu.roll(qp, HEAD_DIM, 1)
                qs.append(qp.astype(BF16))
                sinks.append(jnp.full((w, LANES), sink_ref[h], F32))
            sink = jnp.concatenate(sinks, 0)
            s = lax.dot_general(jnp.concatenate(qs, 0), km, _NT, preferred_element_type=F32) + bias
            m = jnp.maximum(jnp.max(s, -1, keepdims=True), sink)
            p = jnp.concatenate([jnp.exp(s[:, j * LANES:(j + 1) * LANES] - m) for j in range(2 * w // LANES)], 1)
            pv = jnp.dot(p.astype(BF16), v_ones, preferred_element_type=F32)
            o = pv[:, :LANES] / (pv[:, LANES:] + jnp.exp(sink - m))
            for a in range(grp):
                h = kvh * grp + a
                oh = o[a * w:(a + 1) * w, :]
                if h % 2 != kvh:
                    oh = pltpu.roll(oh, HEAD_DIM, 1)
                head_out[h] = oh
        for hp in range(n_q // 2):
            pair = jnp.where(lane_o < HEAD_DIM, head_out[2 * hp], head_out[2 * hp + 1])
            o_ref[i * w:(i + 1) * w, hp * LANES:(hp + 1) * LANES] = pair.astype(BF16)


def _swa_attention(q, k, v, sinks, b, s):
    rows = min(SWA_ROWS, s)
    nt = s // rows
    return pl.pallas_call(
        functools.partial(_swa_kernel, rows=rows),
        grid=(b, nt),
        in_specs=[pl.BlockSpec(memory_space=pltpu.SMEM),
                  pl.BlockSpec((rows, q.shape[1]), lambda bi, ti: (bi * nt + ti, 0)),
                  pl.BlockSpec((s, k.shape[1]), lambda bi, ti: (bi, 0)),
                  pl.BlockSpec((s, v.shape[1]), lambda bi, ti: (bi, 0))],
        out_specs=pl.BlockSpec((rows, q.shape[1]), lambda bi, ti: (bi * nt + ti, 0)),
        out_shape=jax.ShapeDtypeStruct(q.shape, BF16),
        compiler_params=_cparams("parallel", "arbitrary"),
        name="swa_attention",
    )(sinks, q, k, v)


def _layer_norm(y, g, b):
    mu = jnp.mean(y, -1, keepdims=True)
    yc = y - mu
    var = jnp.mean(yc * yc, -1, keepdims=True)
    return yc * lax.rsqrt(var + 1e-5) * g + b


PIECES = 8


def _tm_load(ref, r0, nrows):
    return jnp.concatenate([ref[pl.ds(r0 * PIECES + p, nrows, stride=PIECES), :] for p in range(PIECES)], axis=1)


def _tm_store(ref, r0, val):
    for p in range(PIECES):
        ref[pl.ds(r0 * PIECES + p, val.shape[0], stride=PIECES), :] = val[:, p * LANES:(p + 1) * LANES]


def _out_ln_kernel(x_ref, a_ref, b_ref, w_ref, g_ref, beta_ref, otm_ref, *, alpha):
    ka = a_ref.shape[1]
    mix = (jnp.dot(a_ref[...], w_ref[:ka, :], preferred_element_type=F32)
           + jnp.dot(b_ref[...], w_ref[ka:, :], preferred_element_type=F32))
    _tm_store(otm_ref, 0, _layer_norm(alpha * x_ref[...] + mix, g_ref[...], beta_ref[...]))


def _out_ln(xf, a, b, w, g, beta, alpha):
    n, d = xf.shape
    assert d == PIECES * LANES
    tm = min(PROJ_ROWS, n)
    row = lambda width: pl.BlockSpec((tm, width), lambda i: (i, 0))
    full = lambda arr: pl.BlockSpec(arr.shape, lambda i: (0, 0))
    return pl.pallas_call(
        functools.partial(_out_ln_kernel, alpha=alpha),
        grid=(n // tm,),
        in_specs=[row(d), row(a.shape[1]), row(b.shape[1]), full(w), full(g), full(beta)],
        out_specs=pl.BlockSpec((tm * PIECES, LANES), lambda i: (i, 0)),
        out_shape=jax.ShapeDtypeStruct((n * PIECES, LANES), F32),
        compiler_params=_cparams("parallel"),
        name="out_proj_ln",
    )(xf, a, b, w, g, beta)


def _row_argmax(v, row, nrow):
    mx = jnp.max(v, 0, keepdims=True)
    idx = jnp.min(jnp.where(v == mx, row, nrow), 0, keepdims=True)
    return mx, idx


def _router_kernel(x_ref, w_ref, bias_ref, eidx_ref, wts_ref):
    tm = x_ref.shape[0] // PIECES
    logits = jnp.dot(_tm_load(x_ref, 0, tm), w_ref[...], preferred_element_type=F32,
                     precision=lax.Precision.HIGHEST)
    scores = jax.nn.sigmoid(logits.T[:N_EXPERTS, :])
    biased = scores + bias_ref[...]
    per = N_EXPERTS // N_GROUPS
    row_g = lax.broadcasted_iota(I32, (per, tm), 0)
    gsc = []
    for g in range(N_GROUPS):
        v = biased[g * per:(g + 1) * per, :]
        m1, i1 = _row_argmax(v, row_g, per)
        m2 = jnp.max(jnp.where(row_g == i1, -jnp.inf, v), 0, keepdims=True)
        gsc.append(m1 + m2)
    gscore = jnp.concatenate(gsc, 0)
    row_n = lax.broadcasted_iota(I32, (N_GROUPS, tm), 0)
    gok = jnp.zeros((N_GROUPS, tm), F32)
    for _ in range(TOPK_GROUPS):
        _, ig = _row_argmax(gscore, row_n, N_GROUPS)
        hit = row_n == ig
        gok = jnp.where(hit, 1.0, gok)
        gscore = jnp.where(hit, -jnp.inf, gscore)
    cand = jnp.concatenate([jnp.where(gok[g:g + 1, :] > 0.0, biased[g * per:(g + 1) * per, :], -jnp.inf)
                            for g in range(N_GROUPS)], 0)
    row_e = lax.broadcasted_iota(I32, (N_EXPERTS, tm), 0)
    ids, ws = [], []
    for _ in range(TOP_K):
        _, ie = _row_argmax(cand, row_e, N_EXPERTS)
        hit = row_e == ie
        ids.append(ie)
        ws.append(jnp.sum(jnp.where(hit, scores, 0.0), 0, keepdims=True))
        cand = jnp.where(hit, -jnp.inf, cand)
    w = jnp.concatenate(ws, 0)
    eidx_ref[...] = jnp.concatenate(ids, 0)
    wts_ref[...] = w / jnp.sum(w, 0, keepdims=True) * ROUTED_SCALE


def _router(x_tm, w_router, e_bias):
    n = x_tm.shape[0] // PIECES
    tm = min(ROUTER_ROWS, n)
    wpad = jnp.pad(w_router, ((0, 0), (0, LANES - N_EXPERTS)))
    bcol = e_bias.astype(F32).reshape(N_EXPERTS, 1)
    out = pl.BlockSpec((TOP_K, tm), lambda i: (0, i))
    return pl.pallas_call(
        _router_kernel,
        grid=(n // tm,),
        in_specs=[pl.BlockSpec((tm * PIECES, LANES), lambda i: (i, 0)), pl.BlockSpec(wpad.shape, lambda i: (0, 0)),
                  pl.BlockSpec(bcol.shape, lambda i: (0, 0))],
        out_specs=[out, out],
        out_shape=[jax.ShapeDtypeStruct((TOP_K, n), I32), jax.ShapeDtypeStruct((TOP_K, n), F32)],
        compiler_params=_cparams("parallel"),
        name="router",
    )(x_tm, wpad, bcol)


def _dispatch(eidx_t, wts_t, chunk, tail):
    n = eidx_t.shape[1]
    nc = n // chunk
    a = chunk * TOP_K
    ec = eidx_t.reshape(TOP_K, nc, chunk).transpose(1, 0, 2).reshape(nc, a)
    wc = wts_t.reshape(TOP_K, nc, chunk).transpose(1, 0, 2).reshape(nc, a)
    ex = jnp.arange(N_EXPERTS, dtype=I32)
    counts = jnp.sum((ec[:, :, None] == ex[None, None, :]).astype(I32), axis=1)
    ends = jnp.cumsum(counts, axis=1).astype(I32)
    starts = jnp.concatenate([jnp.zeros((nc, 1), I32), ends], axis=1)
    mult = 1 << int(chunk - 1).bit_length()
    tok_ids = jnp.broadcast_to(jnp.arange(a, dtype=I32)[None, :] % chunk, (nc, a))
    keys, gates = lax.sort((ec * mult + tok_ids, wc), dimension=1, num_keys=1)
    rows = jnp.pad((keys % mult) * PIECES, ((0, 0), (0, tail)))
    gates = jnp.pad(gates, ((0, 0), (0, tail)))
    return starts.reshape(-1), rows.reshape(-1), gates.reshape(-1)


def _swiglu(xb, wg, wu):
    hg = jnp.dot(xb, wg, preferred_element_type=F32)
    hu = jnp.dot(xb, wu, preferred_element_type=F32)
    return hg * jax.nn.sigmoid(hg) * hu


def _moe_kernel(start_ref, tok_ref, gate_ref, x_ref, wg_ref, wu_ref, wd_ref, sg_ref, su_ref, sd_ref,
                g_ref, beta_ref, o_ref, acc_ref, xga_ref, xgb_ref, oga_ref, ogb_ref,
                *, chunk, rows, head, alpha):
    c = pl.program_id(0)
    pair = pl.program_id(1)
    n_e = 2 * pl.num_programs(1)
    group = 8
    sub = min(512, chunk)

    @pl.when((c == 0) & (pair == 0))
    def _():
        for ref in (xga_ref, xgb_ref):
            ref[...] = jnp.zeros_like(ref)

    @pl.when(pair == 0)
    def _():
        for s0 in range(0, chunk, sub):
            xb = _tm_load(x_ref, s0, sub).astype(BF16)
            h = _swiglu(xb, sg_ref[...], su_ref[...]).astype(BF16)
            _tm_store(acc_ref, s0, jnp.dot(h, sd_ref[...], preferred_element_type=F32))
        acc_ref[chunk * PIECES:(chunk + 1) * PIECES, :] = jnp.zeros((PIECES, LANES), F32)

    def run(ei):
        return start_ref[c * (n_e + 1) + ei], start_ref[c * (n_e + 1) + ei + 1]

    def at_row(first):
        return pl.ds(pl.multiple_of(first, PIECES), PIECES)

    def gather_group(xg, base, r0, _):
        for i in range(group):
            xg[at_row((r0 + i) * PIECES), :] = x_ref[at_row(tok_ref[base + r0 + i]), :]

    def scatter_group(og, base, nrow, r0, mask):
        firsts, vals = [], []
        for i in range(group):
            idx = base + r0 + i
            first = tok_ref[idx]
            if mask:
                first = jnp.where(r0 + i < nrow, first, chunk * PIECES)
            firsts.append(first)
            vals.append(acc_ref[at_row(first), :] + gate_ref[idx] * og[at_row((r0 + i) * PIECES), :])
        for i in range(group):
            acc_ref[at_row(firsts[i]), :] = vals[i]

    def sweep(fn, r_lo, r_hi):
        n = jnp.maximum(r_hi - r_lo, 0)
        nfull = n // group

        def body(gi, carry):
            fn(pl.multiple_of(r_lo + gi * group, group), False)
            return carry

        lax.fori_loop(0, nfull, body, 0)
        pl.when(n % group != 0)(lambda: fn(pl.multiple_of(r_lo + nfull * group, group), True))

    def expert(k, xg, og):
        xb = _tm_load(xg, 0, rows).astype(BF16)
        h = _swiglu(xb, wg_ref[k], wu_ref[k]).astype(BF16)
        _tm_store(og, 0, jnp.dot(h, wd_ref[k], preferred_element_type=F32))

    def step(k, xg_cur, xg_next, og_cur, og_prev):
        e = 2 * pair + k
        lo, hi = run(e)
        n_cur = jnp.minimum(hi - lo, rows)
        lo_next, _ = run(jnp.minimum(e + 1, n_e - 1))
        lo_prev, hi_prev = run(jnp.maximum(e - 1, 0))
        n_prev = jnp.where(e > 0, jnp.minimum(hi_prev - lo_prev, rows), 0)
        sweep(lambda r0, p: gather_group(xg_cur, lo, r0, p), jnp.where(e == 0, 0, head), n_cur)
        for r0 in range(0, head, group):
            gather_group(xg_next, lo_next, r0, True)
        expert(k, xg_cur, og_cur)
        for r0 in range(0, head, group):
            scatter_group(og_prev, lo_prev, n_prev, r0, True)
        sweep(lambda r0, p: scatter_group(og_cur, lo, n_cur, r0, p), head, n_cur)

        def later_block(bi, carry):
            base = lo + bi * rows
            nrow = jnp.minimum(hi - base, rows)
            sweep(lambda r0, p: gather_group(xg_cur, base, r0, p), 0, nrow)
            expert(k, xg_cur, og_prev)
            sweep(lambda r0, p: scatter_group(og_prev, base, nrow, r0, p), 0, nrow)
            return carry

        lax.fori_loop(1, (hi - lo + rows - 1) // rows, later_block, 0)
        if k == 1:
            pl.when(e == n_e - 1)(
                lambda: sweep(lambda r0, p: scatter_group(og_cur, lo, n_cur, r0, p), 0, jnp.minimum(n_cur, head)))

    step(0, xga_ref, xgb_ref, oga_ref, ogb_ref)
    step(1, xgb_ref, xga_ref, ogb_ref, oga_ref)

    @pl.when(pair == pl.num_programs(1) - 1)
    def _():
        for s0 in range(0, chunk, sub):
            z = alpha * _tm_load(x_ref, s0, sub) + _tm_load(acc_ref, s0, sub)
            o_ref[s0:s0 + sub, :] = _layer_norm(z, g_ref[...], beta_ref[...])


def _moe_ln(x_tm, eidx, wts, layer, wg, wu, wd, sg, su, sd, g, beta, alpha):
    n_layers, n_e = wg.shape[:2]
    wg, wu, wd = (w.reshape((n_layers * n_e,) + w.shape[2:]) for w in (wg, wu, wd))
    first = layer * (n_e // 2)
    n = x_tm.shape[0] // PIECES
    d = PIECES * LANES
    chunk = min(MOE_CHUNK, n)
    rows = MOE_ROWS
    tail = -(-rows // 1024) * 1024
    starts, tok, gate = _dispatch(eidx, wts, chunk, tail)
    nc = n // chunk
    de = wg.shape[2]
    assert n_e % 2 == 0
    full = lambda arr: pl.BlockSpec(arr.shape, lambda c, e, s_r: (0,) * arr.ndim)
    lists = pl.BlockSpec((chunk * TOP_K + tail,), lambda c, e, s_r: (c,), memory_space=pltpu.SMEM)
    grid_spec = pltpu.PrefetchScalarGridSpec(
        num_scalar_prefetch=1,
        grid=(nc, n_e // 2),
        in_specs=[
            lists, lists,
            pl.BlockSpec((chunk * PIECES, LANES), lambda c, e, s_r: (c, 0)),
            pl.BlockSpec((2, d, de), lambda c, e, s_r: (first + e, 0, 0)),
            pl.BlockSpec((2, d, de), lambda c, e, s_r: (first + e, 0, 0)),
            pl.BlockSpec((2, de, d), lambda c, e, s_r: (first + e, 0, 0)),
            full(sg), full(su), full(sd), full(g), full(beta),
        ],
        out_specs=pl.BlockSpec((chunk, d), lambda c, e, s_r: (c, 0)),
        scratch_shapes=[pltpu.VMEM(((chunk + 1) * PIECES, LANES), F32)]
        + [pltpu.VMEM((rows * PIECES, LANES), F32) for _ in range(4)],
    )
    return pl.pallas_call(
        functools.partial(_moe_kernel, chunk=chunk, rows=rows, head=min(MOE_HEAD, rows), alpha=alpha),
        grid_spec=grid_spec,
        out_shape=jax.ShapeDtypeStruct((n, d), F32),
        compiler_params=_cparams("arbitrary", "arbitrary"),
        name="moe_experts",
    )(starts, tok, gate, x_tm, wg, wu, wd, sg, su, sd, g, beta)


def _rope_tables(positions):
    pos = positions.reshape(-1).astype(F32)[:, None]
    n = pos.shape[0]

    def cs(dim):
        inv = ROPE_THETA ** (-jnp.arange(0, dim, 2, dtype=F32) / dim)
        ang = pos * inv
        return jnp.cos(ang), jnp.sin(ang)

    c, s = cs(HEAD_DIM)
    c64 = jnp.concatenate([c, c, c, c], -1)
    s64 = jnp.concatenate([-s, s, -s, s], -1)
    c, s = cs(MLA_ROPE)
    one = jnp.ones((n, MLA_NOPE), F32)
    pad = LANES - MLA_NOPE - MLA_ROPE
    cm = jnp.concatenate([one, c, c, jnp.ones((n, pad), F32)], -1)
    sm = jnp.concatenate([0 * one, -s, s, jnp.zeros((n, pad), F32)], -1)
    return (c64, s64), (cm, sm)


def _even_mixer(xf, b, s, tabs, w_in, g_qa, w_qb, g_kva, w_kvb):
    d = xf.shape[1]
    tab64, tabm = tabs
    na = 3 * 512
    kpe_lo = na + 256 + 128
    wpad = jnp.concatenate([w_in[:, :kpe_lo], jnp.zeros((d, MLA_NOPE), F32), w_in[:, kpe_lo:],
                            jnp.zeros((d, LANES - MLA_NOPE - MLA_ROPE), F32)], 1).astype(BF16)
    nh = w_qb.shape[1] // (MLA_NOPE + MLA_ROPE)
    wqb = jnp.pad(w_qb.reshape(-1, nh, MLA_NOPE + MLA_ROPE), ((0, 0), (0, 0), (0, LANES - MLA_NOPE - MLA_ROPE)))
    wqb = wqb.reshape(-1, nh * LANES).astype(BF16)
    wkv3 = w_kvb.reshape(-1, nh, MLA_NOPE + MLA_V)
    wkk = jnp.pad(wkv3[:, :, :MLA_NOPE], ((0, 0), (0, 0), (0, LANES - MLA_NOPE))).reshape(-1, nh * LANES).astype(BF16)
    wkv = wkv3[:, :, MLA_NOPE:].reshape(-1, nh * MLA_V).astype(BF16)
    segs = [(0, 512, True, QK_SCALE), (512, 1024, True, 1.0), (1024, 1536, False, 1.0)]
    qa, ka, va, qm, km, vm = _projection(
        xf, wpad, segs, tab64,
        mla_args=(tabm[0], tabm[1], g_qa.reshape(1, -1), wqb, g_kva.reshape(1, -1), wkk, wkv),
        mla_out_widths=(nh * LANES, nh * LANES, nh * MLA_V))
    o_a = _moba_attention(qa, ka, va, b, s)
    o_b = _mla_attention(qm, km, vm, b, s)
    return o_a, o_b


def _odd_mixer(xf, b, s, tabs, w_in, sinks, lam_p, g_subln, lambda_init):
    segs = [(0, 512, True, QK_SCALE), (512, 640, True, 1.0), (640, 768, False, 1.0),
            (768, 1280, True, QK_SCALE), (1280, 1792, True, 1.0), (1792, 2304, False, 1.0)]
    qc, kc, vc, qd, kd, vd = _projection(xf, w_in.astype(BF16), segs, tabs[0])
    o_c = _swa_attention(qc, kc, vc, sinks.astype(F32), b, s)
    o_d = _diff_attention(qd, kd, vd, lam_p.astype(F32), g_subln.reshape(1, -1), lambda_init, b, s)
    return o_c, o_d


def kernel(x, positions, ev_w_in, ev_w_out, mla_g_qa, mla_w_qb, mla_g_kva, mla_w_kvb, od_w_in, od_w_out, swa_sinks, diff_lambda, diff_g_subln, ln_g, ln_b, moe_w_router, moe_bias, moe_w_gate, moe_w_up, moe_w_down, sh_w_gate, sh_w_up, sh_w_down):
    b, s, d = x.shape
    depth = ln_g.shape[0]
    alpha = float((2 * depth) ** 0.25)
    xf = x.reshape(b * s, d)
    tabs = _rope_tables(positions)
    wg_all, wu_all, wd_all = (w.astype(BF16) for w in (moe_w_gate, moe_w_up, moe_w_down))
    for l in range(depth):
        i = l // 2
        if l % 2 == 0:
            o1, o2 = _even_mixer(xf, b, s, tabs, ev_w_in[i], mla_g_qa[i], mla_w_qb[i], mla_g_kva[i], mla_w_kvb[i])
            w_out = ev_w_out[i]
        else:
            lambda_init = 0.8 - 0.6 * float(np.exp(-0.3 * l))
            o1, o2 = _odd_mixer(xf, b, s, tabs, od_w_in[i], swa_sinks[i], diff_lambda[i], diff_g_subln[i], lambda_init)
            w_out = od_w_out[i]
        x_tm = _out_ln(xf, o1, o2, w_out.astype(BF16), ln_g[l, 0].reshape(1, d), ln_b[l, 0].reshape(1, d), alpha)
        eidx, wts = _router(x_tm, moe_w_router[l], moe_bias[l])
        xf = _moe_ln(x_tm, eidx, wts, l, wg_all, wu_all, wd_all, sh_w_gate[l].astype(BF16), sh_w_up[l].astype(BF16),
                     sh_w_down[l].astype(BF16), ln_g[l, 1].reshape(1, d), ln_b[l, 1].reshape(1, d), alpha)
    return xf.reshape(b, s, d)
```
